```python
import jax, jax.numpy as jnp
from jax import lax
import numpy as np

D_MODEL = 1024
BATCH = 4
SEQ = 8192
DEPTH = 4

CHUNK = 64
D_MIX = D_MODEL
N_GROUPS = 4
GW = D_MIX // N_GROUPS
HEAD_DIM = 64
N_HEADS_GROUP = GW // HEAD_DIM
ATTN_LEFT_CHUNKS = 8
N_BAND = ATTN_LEFT_CHUNKS + 1
REL_CLIP = 256
ATTN_SCALE = HEAD_DIM ** -0.5
NEG_INF = -1e30
RWKV_W_RANK = 32
RWKV_A_RANK = 32
RWKV_G_RANK = 64
RWKV_GN_EPS = HEAD_DIM * 1e-5
LRU_CONV = 4
LRU_C = 8.0
LRU_BLOCKS = N_HEADS_GROUP
LRU_BLOCK_DIM = GW // LRU_BLOCKS
ATTN_COLS = 3 * GW
HGRN_COLS = 4 * GW
RWKV_COLS = 3 * GW + RWKV_W_RANK + RWKV_A_RANK + RWKV_G_RANK
LRU_COLS = 2 * GW
D_IN = ATTN_COLS + HGRN_COLS + RWKV_COLS + LRU_COLS
MLP_HIDDEN = 4 * D_MODEL
RMS_EPS = 1e-6

kernel_name = 'hybrid_chunk_causal_block'


def split_cols(t, sizes):
    offs = np.cumsum(sizes)[:-1].tolist()
    return jnp.split(t, offs, axis=-1)


def rms_norm(x, gain):
    xf = x.astype(jnp.float32)
    y = xf * lax.rsqrt(jnp.mean(xf * xf, axis=-1, keepdims=True) + RMS_EPS)
    return (y * gain.astype(jnp.float32)).astype(x.dtype)


def to_heads(t):
    return t.reshape(t.shape[0], t.shape[1], N_HEADS_GROUP, HEAD_DIM)


def to_chunks(t):
    b, s, _ = t.shape
    return t.reshape(b, s // CHUNK, CHUNK, N_HEADS_GROUP, HEAD_DIM).transpose(1, 0, 3, 2, 4)


def from_chunks(t):
    nc, b = t.shape[0], t.shape[1]
    return t.transpose(1, 0, 3, 2, 4).reshape(b, nc * CHUNK, GW)


def chunk_attention(q, k, v, rel_bias):
    b, s, _ = q.shape
    nc = s // CHUNK
    shp = (b, nc, CHUNK, N_HEADS_GROUP, HEAD_DIM)
    qc, kc, vc = q.reshape(shp), k.reshape(shp), v.reshape(shp)
    pad = ((0, 0), (ATTN_LEFT_CHUNKS, 0), (0, 0), (0, 0), (0, 0))
    band = jnp.arange(nc)[:, None] + jnp.arange(N_BAND)[None, :]
    kb = jnp.pad(kc, pad)[:, band].reshape(b, nc, N_BAND * CHUNK, N_HEADS_GROUP, HEAD_DIM)
    vb = jnp.pad(vc, pad)[:, band].reshape(b, nc, N_BAND * CHUNK, N_HEADS_GROUP, HEAD_DIM)
    scores = jnp.einsum('bcqhd,bckhd->bchqk', qc, kb) * ATTN_SCALE
    key_off = ((jnp.arange(N_BAND) - ATTN_LEFT_CHUNKS)[:, None] * CHUNK
               + jnp.arange(CHUNK)[None, :]).reshape(-1)
    rel = key_off[None, :] - jnp.arange(CHUNK)[:, None]
    bias = rel_bias.astype(jnp.float32)[:, jnp.clip(rel, -REL_CLIP, REL_CLIP) + REL_CLIP]
    valid = jnp.repeat(band >= ATTN_LEFT_CHUNKS, CHUNK, axis=1)
    scores = jnp.where(valid[None, :, None, None, :], scores + bias[None, None], NEG_INF)
    probs = jax.nn.softmax(scores, axis=-1)
    out = jnp.einsum('bchqk,bckhd->bcqhd', probs, vb)
    return out.reshape(b, s, GW)


def hgrn2(q_raw, f_raw, i_raw, g_raw, lb, norm_gain):
    b = q_raw.shape[0]
    log_f = jnp.logaddexp(jnp.log(lb), jnp.log1p(-lb) + jax.nn.log_sigmoid(f_raw))
    key = (1.0 - lb) * jax.nn.sigmoid(-f_raw)
    q = jax.nn.silu(q_raw)
    tri = jnp.tril(jnp.ones((CHUNK, CHUNK), dtype=bool))

    def step(state, inp):
        qc, kc, vc, gc = inp
        cum = jnp.cumsum(gc, axis=2)
        last = cum[:, :, -1:, :]
        o_inter = jnp.einsum('bhtk,bhkv->bhtv', qc * jnp.exp(cum), state)
        diff = cum[:, :, :, None, :] - cum[:, :, None, :, :]
        decay = jnp.exp(jnp.where(tri[None, None, :, :, None], diff, -jnp.inf))
        att = jnp.einsum('bhtk,bhsk,bhtsk->bhts', qc, kc, decay)
        o_intra = jnp.einsum('bhts,bhsv->bhtv', att, vc)
        new_state = (jnp.exp(last).transpose(0, 1, 3, 2) * state
                     + jnp.einsum('bhsk,bhsv->bhkv', kc * jnp.exp(last - cum), vc))
        return new_state, o_inter + o_intra

    state0 = jnp.zeros((b, N_HEADS_GROUP, HEAD_DIM, HEAD_DIM), jnp.float32)
    _, o = lax.scan(step, state0, (to_chunks(q), to_chunks(key), to_chunks(i_raw), to_chunks(log_f)))
    o = to_heads(from_chunks(o))
    o = o * lax.rsqrt(jnp.mean(o * o, axis=-1, keepdims=True) + RMS_EPS)
    return o.reshape(g_raw.shape) * norm_gain * jax.nn.silu(g_raw)


def rwkv7(pc, mu, w0, w2, a0, a2, g2, k_k, k_a, r_k, ln_w, ln_b):
    b, s, _ = pc.shape
    prev = jnp.pad(pc, ((0, 0), (1, 0), (0, 0)))[:, :-1]
    xs = pc + mu * (prev - pc)
    r, k, v, w_lo, a_lo, g_lo = split_cols(xs, [GW, GW, GW, RWKV_W_RANK, RWKV_A_RANK, RWKV_G_RANK])
    w_pre = -jax.nn.softplus(-(w0 + jnp.tanh(w_lo) @ w2)) - 0.5
    decay = jnp.exp(-jnp.exp(w_pre))
    a = jax.nn.sigmoid(a0 + a_lo @ a2)
    g = jax.nn.sigmoid(g_lo) @ g2
    kk = to_heads(k * k_k)
    kk = kk / jnp.maximum(jnp.sqrt(jnp.sum(kk * kk, axis=-1, keepdims=True)), 1e-12)
    k = k * (1.0 + (a - 1.0) * k_a)
    r_h, w_h, k_h, v_h, a_h = to_heads(r), to_heads(decay), to_heads(k), to_heads(v), to_heads(a)

    def step(state, inp):
        rt, wt, kt, vt, kkt, at = inp
        sa = jnp.einsum('bhvk,bhk->bhv', state, -kkt)
        state = (state * wt[:, :, None, :] + sa[..., None] * (kkt * at)[:, :, None, :]
                 + vt[..., None] * kt[:, :, None, :])
        return state, jnp.einsum('bhvk,bhk->bhv', state, rt)

    seq_first = lambda t: t.transpose(1, 0, 2, 3)
    state0 = jnp.zeros((b, N_HEADS_GROUP, HEAD_DIM, HEAD_DIM), jnp.float32)
    _, y = lax.scan(step, state0, (seq_first(r_h), seq_first(w_h), seq_first(k_h),
                                   seq_first(v_h), seq_first(kk), seq_first(a_h)))
    y = seq_first(y)
    mean = jnp.mean(y, axis=-1, keepdims=True)
    var = jnp.mean(jnp.square(y - mean), axis=-1, keepdims=True)
    y = ((y - mean) * lax.rsqrt(var + RWKV_GN_EPS)).reshape(b, s, GW) * ln_w + ln_b
    bonus = (jnp.sum(r_h * k_h * r_k, axis=-1, keepdims=True) * v_h).reshape(b, s, GW)
    return (y + bonus) * g


def rglru(xb, gb, conv_w, conv_b, wa, ba, wx, bx, lam):
    b, s, _ = xb.shape
    conv = lax.conv_general_dilated(xb, conv_w.astype(xb.dtype)[:, None, :], window_strides=(1,),
                                    padding=[(LRU_CONV - 1, 0)],
                                    dimension_numbers=('NWC', 'WIO', 'NWC'),
                                    feature_group_count=GW) + conv_b
    xh = conv.reshape(b, s, LRU_BLOCKS, LRU_BLOCK_DIM)
    gate_r = jax.nn.sigmoid(jnp.einsum('bsnd,nde->bsne', xh, wa).reshape(b, s, GW) + ba)
    gate_i = jax.nn.sigmoid(jnp.einsum('bsnd,nde->bsne', xh, wx).reshape(b, s, GW) + bx)
    log_a = -LRU_C * gate_r * jax.nn.softplus(-lam)
    a = jnp.exp(log_a)
    inp = jnp.sqrt(-jnp.expm1(2.0 * log_a)) * (gate_i * conv)

    def combine(c1, c2):
        a1, b1 = c1
        a2, b2 = c2
        return a1 * a2, a2 * b1 + b2

    _, h = lax.associative_scan(combine, (a, inp), axis=1)
    return h * jax.nn.gelu(gb)


def setup_inputs(seed: int = 0) -> dict:
    key = jax.random.key(seed)
    ks = jax.random.split(key, 32)
    L = DEPTH

    def nrm(k, shape, scale):
        return jax.random.normal(k, shape, jnp.float32) * scale

    u = jax.random.uniform(ks[27], (L, GW), jnp.float32, 0.9, 0.999)
    a_root = u ** (1.0 / LRU_C)
    lam = jnp.log(a_root) - jnp.log1p(-a_root)
    return {
        'x': nrm(ks[0], (BATCH, SEQ, D_MODEL), 1.0),
        'norm_mix_pre': 1.0 + nrm(ks[1], (L, D_MODEL), 0.05),
        'norm_mix_post': 1.0 + nrm(ks[2], (L, D_MODEL), 0.05),
        'norm_mlp_pre': 1.0 + nrm(ks[3], (L, D_MODEL), 0.05),
        'norm_mlp_post': 1.0 + nrm(ks[4], (L, D_MODEL), 0.05),
        'w_in': nrm(ks[5], (L, D_MODEL, D_IN), D_MODEL ** -0.5),
        'w_out': nrm(ks[6], (L, D_MIX, D_MODEL), D_MIX ** -0.5),
        'attn_rel_bias': nrm(ks[7], (L, N_HEADS_GROUP, 2 * REL_CLIP + 1), 0.2),
        'hgrn_lb_logits': nrm(ks[8], (L, GW), 0.5),
        'hgrn_norm': 1.0 + nrm(ks[9], (L, GW), 0.05),
        'rwkv_mu': jax.random.uniform(ks[10], (L, RWKV_COLS), jnp.float32, 0.2, 0.8),
        'rwkv_w0': jax.random.uniform(ks[11], (L, GW), jnp.float32, -6.0, 1.0),
        'rwkv_w2': nrm(ks[12], (L, RWKV_W_RANK, GW), 0.5 * RWKV_W_RANK ** -0.5),
        'rwkv_a0': nrm(ks[13], (L, GW), 0.1),
        'rwkv_a2': nrm(ks[14], (L, RWKV_A_RANK, GW), 0.5 * RWKV_A_RANK ** -0.5),
        'rwkv_g2': nrm(ks[15], (L, RWKV_G_RANK, GW), RWKV_G_RANK ** -0.5),
        'rwkv_k_k': 0.85 + nrm(ks[16], (L, GW), 0.05),
        'rwkv_k_a': 1.0 + nrm(ks[17], (L, GW), 0.05),
        'rwkv_r_k': nrm(ks[18], (L, N_HEADS_GROUP, HEAD_DIM), 0.1),
        'rwkv_ln_w': 1.0 + nrm(ks[19], (L, GW), 0.05),
        'rwkv_ln_b': nrm(ks[20], (L, GW), 0.02),
        'lru_conv_w': nrm(ks[21], (L, LRU_CONV, GW), 0.5 * LRU_CONV ** -0.5),
        'lru_conv_b': nrm(ks[22], (L, GW), 0.02),
        'lru_wa': nrm(ks[23], (L, LRU_BLOCKS, LRU_BLOCK_DIM, LRU_BLOCK_DIM), LRU_BLOCK_DIM ** -0.5),
        'lru_ba': nrm(ks[24], (L, GW), 0.02),
        'lru_wx': nrm(ks[25], (L, LRU_BLOCKS, LRU_BLOCK_DIM, LRU_BLOCK_DIM), LRU_BLOCK_DIM ** -0.5),
        'lru_bx': nrm(ks[26], (L, GW), 0.02),
        'lru_lambda': lam,
        'mlp_w1': nrm(ks[28], (L, D_MODEL, MLP_HIDDEN), D_MODEL ** -0.5),
        'mlp_w2': nrm(ks[29], (L, MLP_HIDDEN, D_MODEL), MLP_HIDDEN ** -0.5),
    }


def reference(x, norm_mix_pre, norm_mix_post, norm_mlp_pre, norm_mlp_post, w_in, w_out,
              attn_rel_bias, hgrn_lb_logits, hgrn_norm, rwkv_mu, rwkv_w0, rwkv_w2, rwkv_a0, rwkv_a2,
              rwkv_g2, rwkv_k_k, rwkv_k_a, rwkv_r_k, rwkv_ln_w, rwkv_ln_b, lru_conv_w, lru_conv_b,
              lru_wa, lru_ba, lru_wx, lru_bx, lru_lambda, mlp_w1, mlp_w2):
    dt = x.dtype
    f32 = jnp.float32
    lb_sm = jax.nn.softmax(hgrn_lb_logits.astype(f32), axis=0)
    lb_all = jnp.maximum(jnp.cumsum(lb_sm, axis=0) - lb_sm[0:1], 0.0)
    for l in range(DEPTH):
        h = rms_norm(x, norm_mix_pre[l])
        proj = (h @ w_in[l]).astype(f32)
        pa, pb, pc, pd = split_cols(proj, [ATTN_COLS, HGRN_COLS, RWKV_COLS, LRU_COLS])
        qa, ka, va = split_cols(pa, [GW, GW, GW])
        ya = chunk_attention(qa, ka, va, attn_rel_bias[l])
        qb, fb, ib, gb = split_cols(pb, [GW, GW, GW, GW])
        yb = hgrn2(qb, fb, ib, gb, lb_all[l], hgrn_norm[l].astype(f32))
        yc = rwkv7(pc, rwkv_mu[l].astype(f32), rwkv_w0[l].astype(f32), rwkv_w2[l].astype(f32),
                   rwkv_a0[l].astype(f32), rwkv_a2[l].astype(f32), rwkv_g2[l].astype(f32),
                   rwkv_k_k[l].astype(f32), rwkv_k_a[l].astype(f32), rwkv_r_k[l].astype(f32),
                   rwkv_ln_w[l].astype(f32), rwkv_ln_b[l].astype(f32))
        xd, gd = split_cols(pd, [GW, GW])
        yd = rglru(xd, gd, lru_conv_w[l].astype(f32), lru_conv_b[l].astype(f32), lru_wa[l].astype(f32),
                   lru_ba[l].astype(f32), lru_wx[l].astype(f32), lru_bx[l].astype(f32),
                   lru_lambda[l].astype(f32))
        mix = jnp.concatenate([ya, yb, yc, yd], axis=-1).astype(dt)
        x = x + rms_norm(mix @ w_out[l], norm_mix_post[l])
        h = rms_norm(x, norm_mlp_pre[l])
        ff = jnp.square(jax.nn.relu(h @ mlp_w1[l])) @ mlp_w2[l]
        x = x + rms_norm(ff, norm_mlp_post[l])
    return x
```

```python
import functools
import math

import jax
import jax.numpy as jnp
from jax import lax
from jax.experimental import pallas as pl
from jax.experimental.pallas import tpu as pltpu

F32 = jnp.float32
BF16 = jnp.bfloat16

GW = 256
HEAD_DIM = 64
N_HEADS = 4
CHUNK = 64
ATTN_LEFT_CHUNKS = 8
BAND = (ATTN_LEFT_CHUNKS + 1) * CHUNK
REL_CLIP = 256
ATTN_SCALE = HEAD_DIM ** -0.5
NEG_INF = -1e30
RMS_EPS = 1e-6
RWKV_GN_EPS = HEAD_DIM * 1e-5
RWKV_LORA = 128
LRU_C = 8.0
LRU_CONV = 4
HGRN_SUB = 16

ATTN_COLS = 3 * GW
HGRN_COLS = 4 * GW
RWKV_COLS = 3 * GW + RWKV_LORA
LRU_COLS = 2 * GW

VMEM_LIMIT = 56 * 1024 * 1024


def _cparams(*sem):
    return pltpu.CompilerParams(dimension_semantics=sem, vmem_limit_bytes=VMEM_LIMIT)


def _dot(a, b):
    return jnp.dot(a.astype(BF16), b.astype(BF16), preferred_element_type=F32)


def _dot_nt(a, b):
    return lax.dot_general(a.astype(BF16), b.astype(BF16), (((1,), (1,)), ((), ())),
                           preferred_element_type=F32)


def _dot_tn(a, b):
    return lax.dot_general(a.astype(BF16), b.astype(BF16), (((0,), (0,)), ((), ())),
                           preferred_element_type=F32)


def _split(a):
    hi = a.astype(BF16)
    lo = (a - hi.astype(F32)).astype(BF16)
    return hi, lo


def _dot_x3(a, b, kind="nn"):
    f = {"nn": _dot, "nt": _dot_nt, "tn": _dot_tn}[kind]
    ah, al = _split(a)
    bh, bl = _split(b)
    return f(ah, bh) + (f(ah, bl) + f(al, bh))


def _dot_lhs2(a, b):
    ah, al = _split(a)
    return _dot(ah, b) + _dot(al, b)


def _rms(x, gain):
    return x * lax.rsqrt(jnp.mean(x * x, axis=-1, keepdims=True) + RMS_EPS) * gain


def _sigmoid(x):
    return 1.0 / (1.0 + jnp.exp(-x))


def _softplus(x):
    return jnp.maximum(x, 0.0) + jnp.log1p(jnp.exp(-jnp.abs(x)))


def _expm1(z):
    u = jnp.exp(z)
    um1 = u - 1.0
    near = um1 * z / jnp.where(u == 1.0, 1.0, jnp.log(u))
    return jnp.where(jnp.abs(z) > 0.5, um1, jnp.where(u == 1.0, z, near))


def _head_block_mask(n):
    r = lax.broadcasted_iota(jnp.int32, (n, n), 0) // HEAD_DIM
    c = lax.broadcasted_iota(jnp.int32, (n, n), 1) // HEAD_DIM
    return r == c


def _cumsum_rows(x):
    n = x.shape[0]
    row = lax.broadcasted_iota(jnp.int32, (n, 1), 0)
    d = 1
    while d < n:
        x = x + jnp.where(row >= d, pltpu.roll(x, d, 0), 0.0)
        d *= 2
    return x


def _stack_heads(x):
    lane_head = lax.broadcasted_iota(jnp.int32, (1, GW), 1) // HEAD_DIM
    return jnp.concatenate([jnp.where(lane_head == h, x, 0.0) for h in range(N_HEADS)], axis=0)


def _unstack_heads(xbd, c):
    return xbd[0:c] + xbd[c:2 * c] + xbd[2 * c:3 * c] + xbd[3 * c:4 * c]


def _inproj_body(x_ref, g_ref, w_ref, oa_ref, ob_ref, oc_ref, od_ref):
    h = _rms(x_ref[...], g_ref[...]).astype(BF16)
    o0, o1, o2 = ATTN_COLS, ATTN_COLS + HGRN_COLS, ATTN_COLS + HGRN_COLS + RWKV_COLS
    oa_ref[...] = jnp.dot(h, w_ref[:, 0:o0], preferred_element_type=F32).astype(BF16)
    ob_ref[...] = jnp.dot(h, w_ref[:, o0:o1], preferred_element_type=F32)
    oc_ref[...] = jnp.dot(h, w_ref[:, o1:o2], preferred_element_type=F32)
    od_ref[...] = jnp.dot(h, w_ref[:, o2:], preferred_element_type=F32)


def _inproj(x2, gain, w, tm):
    t, d = x2.shape
    d_in = w.shape[1]
    return pl.pallas_call(
        _inproj_body,
        grid=(t // tm,),
        in_specs=[pl.BlockSpec((tm, d), lambda i: (i, 0)),
                  pl.BlockSpec((1, d), lambda i: (0, 0)),
                  pl.BlockSpec((d, d_in), lambda i: (0, 0))],
        out_specs=[pl.BlockSpec((tm, ATTN_COLS), lambda i: (i, 0)),
                   pl.BlockSpec((tm, HGRN_COLS), lambda i: (i, 0)),
                   pl.BlockSpec((tm, RWKV_COLS), lambda i: (i, 0)),
                   pl.BlockSpec((tm, LRU_COLS), lambda i: (i, 0))],
        out_shape=[jax.ShapeDtypeStruct((t, ATTN_COLS), BF16),
                   jax.ShapeDtypeStruct((t, HGRN_COLS), F32),
                   jax.ShapeDtypeStruct((t, RWKV_COLS), F32),
                   jax.ShapeDtypeStruct((t, LRU_COLS), F32)],
        compiler_params=_cparams("parallel"),
        name="inproj",
    )(x2, gain, w)


def _attn_body(q_ref, k_ref, v_ref, bias_ref, o_ref, *, chunks):
    i = pl.program_id(1)
    lane_head = lax.broadcasted_iota(jnp.int32, (1, GW), 1) // HEAD_DIM
    kpos = lax.broadcasted_iota(jnp.int32, (CHUNK, BAND), 1)

    def chunk(j, carry):
        c = i * chunks + j
        q = q_ref[0, pl.ds(pl.multiple_of(j * CHUNK, CHUNK), CHUNK), :]
        r0 = pl.multiple_of(c * CHUNK, CHUNK)
        kw = k_ref[0, pl.ds(r0, BAND), :]
        vw = v_ref[0, pl.ds(r0, BAND), :]
        valid = kpos >= (ATTN_LEFT_CHUNKS - c) * CHUNK
        acc = jnp.zeros((CHUNK, GW), F32)
        for h in range(N_HEADS):
            mh = lane_head == h
            s = _dot_nt(jnp.where(mh, q, jnp.zeros_like(q)), kw) * ATTN_SCALE + bias_ref[h]
            s = jnp.where(valid, s, NEG_INF)
            p = jnp.exp(s - jnp.max(s, axis=-1, keepdims=True))
            p = p / jnp.sum(p, axis=-1, keepdims=True)
            acc = acc + _dot(p, jnp.where(mh, vw, jnp.zeros_like(vw)))
        o_ref[0, pl.ds(pl.multiple_of(j * CHUNK, CHUNK), CHUNK), :] = acc.astype(BF16)
        return carry

    lax.fori_loop(0, chunks, chunk, 0)


def _attention(pa3, bias, rb):
    b, s, _ = pa3.shape
    pad = ATTN_LEFT_CHUNKS * CHUNK
    kp = jnp.pad(pa3[:, :, GW:2 * GW], ((0, 0), (pad, 0), (0, 0)))
    vp = jnp.pad(pa3[:, :, 2 * GW:3 * GW], ((0, 0), (pad, 0), (0, 0)))
    return pl.pallas_call(
        functools.partial(_attn_body, chunks=rb // CHUNK),
        grid=(b, s // rb),
        in_specs=[pl.BlockSpec((1, rb, GW), lambda bi, i: (bi, i, 0)),
                  pl.BlockSpec((1, s + pad, GW), lambda bi, i: (bi, 0, 0)),
                  pl.BlockSpec((1, s + pad, GW), lambda bi, i: (bi, 0, 0)),
                  pl.BlockSpec((N_HEADS, CHUNK, BAND), lambda bi, i: (0, 0, 0))],
        out_specs=pl.BlockSpec((1, rb, GW), lambda bi, i: (bi, i, 0)),
        out_shape=jax.ShapeDtypeStruct((b, s, GW), BF16),
        compiler_params=_cparams("parallel", "arbitrary"),
        name="attn",
    )(pa3, kp, vp, bias)


def _attn_bias_table(rel_bias):
    key_off = jnp.arange(BAND) - ATTN_LEFT_CHUNKS * CHUNK
    rel = key_off[None, :] - jnp.arange(CHUNK)[:, None]
    return rel_bias.astype(F32)[:, jnp.clip(rel, -REL_CLIP, REL_CLIP) + REL_CLIP]


def _hgrn_body(q_ref, f_ref, i_ref, g_ref, par_ref, o_ref, st_ref, oacc_ref, *, rb):
    @pl.when(pl.program_id(1) == 0)
    def _():
        st_ref[...] = jnp.zeros_like(st_ref)

    m = HGRN_SUB
    log_lb = par_ref[0:1, :]
    log_1m_lb = par_ref[1:2, :]
    one_m_lb = par_ref[2:3, :]
    gain = par_ref[3:4, :]
    bd = _head_block_mask(GW)
    ones_bd = jnp.where(bd, 1.0, 0.0).astype(BF16)
    trow = lax.broadcasted_iota(jnp.int32, (m, 1), 0)

    def sub(j, carry):
        rows = pl.ds(pl.multiple_of(j * m, m), m)
        qr = q_ref[0, rows, :]
        fr = f_ref[0, rows, :]
        v = i_ref[0, rows, :]
        log_sig = jnp.minimum(fr, 0.0) - jnp.log1p(jnp.exp(-jnp.abs(fr)))
        bt = log_1m_lb + log_sig
        log_f = jnp.maximum(log_lb, bt) + jnp.log1p(jnp.exp(-jnp.abs(log_lb - bt)))
        key = one_m_lb * _sigmoid(-fr)
        q = qr * _sigmoid(qr)
        cum = _cumsum_rows(log_f)
        last = cum[m - 1:m, :]
        st = st_ref[...]
        o = _dot_nt(q * jnp.exp(cum), st)
        ps = []
        for s in range(m):
            dec = jnp.exp(jnp.minimum(cum - cum[s:s + 1, :], 0.0))
            ps.append(jnp.where(trow >= s, q * dec * key[s:s + 1, :], 0.0))
        a_exp = _dot_lhs2(jnp.concatenate(ps, axis=0), ones_bd)
        for s in range(m):
            o = o + a_exp[s * m:(s + 1) * m, :] * v[s:s + 1, :]
        oacc_ref[rows, :] = o
        kd = key * jnp.exp(last - cum)
        st_ref[...] = st * jnp.exp(last) + jnp.where(bd, _dot_tn(v, kd), 0.0)
        return carry

    lax.fori_loop(0, rb // m, sub, 0)
    o = oacc_ref[...]
    ms = _dot_lhs2(o * o, ones_bd) * (1.0 / HEAD_DIM)
    g = g_ref[0]
    o_ref[0] = (o * lax.rsqrt(ms + RMS_EPS) * gain * (g * _sigmoid(g))).astype(BF16)


def _hgrn(pb3, par, rb):
    b, s, _ = pb3.shape
    col = lambda n: pl.BlockSpec((1, rb, GW), lambda bi, i, n=n: (bi, i, n))
    return pl.pallas_call(
        functools.partial(_hgrn_body, rb=rb),
        grid=(b, s // rb),
        in_specs=[col(0), col(1), col(2), col(3),
                  pl.BlockSpec((8, GW), lambda bi, i: (0, 0))],
        out_specs=pl.BlockSpec((1, rb, GW), lambda bi, i: (bi, i, 0)),
        out_shape=jax.ShapeDtypeStruct((b, s, GW), BF16),
        scratch_shapes=[pltpu.VMEM((GW, GW), F32), pltpu.VMEM((rb, GW), F32)],
        compiler_params=_cparams("parallel", "arbitrary"),
        name="hgrn",
    )(pb3, pb3, pb3, pb3, par)


def _rwkv_chunk(r, logw, k, v, alpha, beta, ht, bd, strict, incl, eye):
    c = CHUNK
    cum = _cumsum_rows(logw)
    e_last = jnp.exp(cum[c - 1:c, :])
    a_t = alpha * jnp.exp(cum - logw)
    r_t = r * jnp.exp(cum)
    e_inv = jnp.exp(-cum)
    b_t = beta * e_inv
    k_t = k * e_inv
    abd, rbd = _stack_heads(a_t), _stack_heads(r_t)
    bbd, kbd, vbd = _stack_heads(b_t), _stack_heads(k_t), _stack_heads(v)
    l_ab = jnp.where(strict, _dot_x3(abd, bbd, "nt"), 0.0)
    l_ak = jnp.where(strict, _dot_x3(abd, kbd, "nt"), 0.0)
    m_rb = jnp.where(incl, _dot_x3(rbd, bbd, "nt"), 0.0)
    m_rk = jnp.where(incl, _dot_x3(rbd, kbd, "nt"), 0.0)
    x = l_ab
    inv = eye + l_ab
    for _ in range(5):
        x = _dot_x3(x, x)
        inv = inv + _dot_x3(inv, x)
    w = _dot_x3(abd, ht, "nt") + _dot_x3(l_ak, vbd)
    u = _dot_x3(inv, w)
    y = _dot_x3(rbd, ht, "nt") + _dot_x3(m_rb, u) + _dot_x3(m_rk, vbd)
    y = _unstack_heads(y, c)
    u = _unstack_heads(u, c)
    upd = _dot_x3(u, b_t * e_last, "tn") + _dot_x3(v, k_t * e_last, "tn")
    ht_new = ht * e_last + jnp.where(bd, upd, 0.0)
    return y, ht_new


def _rwkv_body(pc_ref, par_ref, w2_ref, a2_ref, g2_ref, o_ref, ht_ref, prev_ref, *, rb):
    @pl.when(pl.program_id(1) == 0)
    def _():
        ht_ref[...] = jnp.zeros_like(ht_ref)
        prev_ref[...] = jnp.zeros_like(prev_ref)

    c = CHUNK
    mu_main = [par_ref[n:n + 1, :] for n in range(3)]
    mu_lo = par_ref[3:4, 0:RWKV_LORA]
    w0, a0 = par_ref[4:5, :], par_ref[5:6, :]
    k_k, k_a, r_k = par_ref[6:7, :], par_ref[7:8, :], par_ref[8:9, :]
    ln_w, ln_b = par_ref[9:10, :], par_ref[10:11, :]
    bd = _head_block_mask(GW)
    ones_bd = jnp.where(bd, 1.0, 0.0).astype(BF16)
    rr = lax.broadcasted_iota(jnp.int32, (GW, GW), 0)
    cc = lax.broadcasted_iota(jnp.int32, (GW, GW), 1)
    strict = bd & (rr > cc)
    incl = bd & (rr >= cc)
    eye = jnp.where(rr == cc, 1.0, 0.0)
    row0 = lax.broadcasted_iota(jnp.int32, (c, 1), 0) == 0

    def chunk(j, carry):
        rows = pl.ds(pl.multiple_of(j * c, c), c)
        pc = pc_ref[0, rows, :]
        prev = jnp.where(row0, prev_ref[...], pltpu.roll(pc, 1, 0))
        prev_ref[...] = pc[c - 1:c, :]
        xs = [pc[:, n * GW:(n + 1) * GW] for n in range(3)]
        ps = [prev[:, n * GW:(n + 1) * GW] for n in range(3)]
        r, k, v = [x + mu * (p - x) for x, p, mu in zip(xs, ps, mu_main)]
        lo, plo = pc[:, 3 * GW:], prev[:, 3 * GW:]
        lo = lo + mu_lo * (plo - lo)
        w_pre = -_softplus(-(w0 + _dot(jnp.tanh(lo), w2_ref[...]))) - 0.5
        logw = -jnp.exp(w_pre)
        a = _sigmoid(a0 + _dot(lo, a2_ref[...]))
        g = _dot(_sigmoid(lo), g2_ref[...])
        kk = k * k_k
        kk = kk / jnp.maximum(jnp.sqrt(_dot_lhs2(kk * kk, ones_bd)), 1e-12)
        k = k * (1.0 + (a - 1.0) * k_a)
        y, ht_new = _rwkv_chunk(r, logw, k, v, -kk, kk * a, ht_ref[...], bd, strict, incl, eye)
        ht_ref[...] = ht_new
        mean = _dot_lhs2(y, ones_bd) * (1.0 / HEAD_DIM)
        yc = y - mean
        var = _dot_lhs2(yc * yc, ones_bd) * (1.0 / HEAD_DIM)
        yn = yc * lax.rsqrt(var + RWKV_GN_EPS) * ln_w + ln_b
        bonus = _dot_lhs2(r * k * r_k, ones_bd) * v
        o_ref[0, rows, :] = ((yn + bonus) * g).astype(BF16)
        return carry

    lax.fori_loop(0, rb // c, chunk, 0)


def _rwkv(pc3, par, w2p, a2p, g2p, rb):
    b, s, _ = pc3.shape
    full = lambda shp: pl.BlockSpec(shp, lambda bi, i: (0,) * len(shp))
    return pl.pallas_call(
        functools.partial(_rwkv_body, rb=rb),
        grid=(b, s // rb),
        in_specs=[pl.BlockSpec((1, rb, RWKV_COLS), lambda bi, i: (bi, i, 0)),
                  full((16, GW)), full((RWKV_LORA, GW)), full((RWKV_LORA, GW)), full((RWKV_LORA, GW))],
        out_specs=pl.BlockSpec((1, rb, GW), lambda bi, i: (bi, i, 0)),
        out_shape=jax.ShapeDtypeStruct((b, s, GW), BF16),
        scratch_shapes=[pltpu.VMEM((GW, GW), F32), pltpu.VMEM((1, RWKV_COLS), F32)],
        compiler_params=_cparams("parallel", "arbitrary"),
        name="rwkv",
    )(pc3, par, w2p, a2p, g2p)


def _lru_body(x_ref, g_ref, par_ref, wa_ref, wx_ref, o_ref, ext_ref, h_ref, *, rb):
    @pl.when(pl.program_id(1) == 0)
    def _():
        ext_ref[0:8, :] = jnp.zeros((8, GW), F32)
        h_ref[...] = jnp.zeros_like(h_ref)

    conv_b, ba, bx, lam = par_ref[4:5, :], par_ref[5:6, :], par_ref[6:7, :], par_ref[7:8, :]
    x = x_ref[0]
    ext_ref[8:8 + rb, :] = x
    conv = conv_b + par_ref[0:1, :] * ext_ref[pl.ds(8 - 3, rb), :]
    for j in range(1, LRU_CONV):
        conv = conv + par_ref[j:j + 1, :] * ext_ref[pl.ds(8 - 3 + j, rb), :]
    ext_ref[0:8, :] = x[rb - 8:rb, :]
    gate_r = _sigmoid(_dot(conv, wa_ref[...]) + ba)
    gate_i = _sigmoid(_dot(conv, wx_ref[...]) + bx)
    log_a = -LRU_C * gate_r * _softplus(-lam)
    a = jnp.exp(log_a)
    inp = jnp.sqrt(-_expm1(2.0 * log_a)) * (gate_i * conv)
    row = lax.broadcasted_iota(jnp.int32, (rb, 1), 0)
    d = 1
    while d < rb:
        m = row >= d
        inp = jnp.where(m, a * pltpu.roll(inp, d, 0) + inp, inp)
        a = jnp.where(m, a * pltpu.roll(a, d, 0), a)
        d *= 2
    h = inp + a * h_ref[...]
    h_ref[...] = h[rb - 1:rb, :]
    g = g_ref[0]
    gelu = 0.5 * g * (1.0 + jnp.tanh(math.sqrt(2.0 / math.pi) * (g + 0.044715 * (g * g * g))))
    o_ref[0] = (h * gelu).astype(BF16)


def _lru(pd3, par, wa_bd, wx_bd, rb):
    b, s, _ = pd3.shape
    full = lambda shp: pl.BlockSpec(shp, lambda bi, i: (0,) * len(shp))
    return pl.pallas_call(
        functools.partial(_lru_body, rb=rb),
        grid=(b, s // rb),
        in_specs=[pl.BlockSpec((1, rb, GW), lambda bi, i: (bi, i, 0)),
                  pl.BlockSpec((1, rb, GW), lambda bi, i: (bi, i, 1)),
                  full((8, GW)), full((GW, GW)), full((GW, GW))],
        out_specs=pl.BlockSpec((1, rb, GW), lambda bi, i: (bi, i, 0)),
        out_shape=jax.ShapeDtypeStruct((b, s, GW), BF16),
        scratch_shapes=[pltpu.VMEM((rb + 8, GW), F32), pltpu.VMEM((1, GW), F32)],
        compiler_params=_cparams("parallel", "arbitrary"),
        name="lru",
    )(pd3, pd3, par, wa_bd, wx_bd)


def _outproj_body(ya_ref, yb_ref, yc_ref, yd_ref, x_ref, w_ref, g_ref, o_ref):
    acc = jnp.dot(ya_ref[...], w_ref[0:GW, :], preferred_element_type=F32)
    acc = acc + jnp.dot(yb_ref[...], w_ref[GW:2 * GW, :], preferred_element_type=F32)
    acc = acc + jnp.dot(yc_ref[...], w_ref[2 * GW:3 * GW, :], preferred_element_type=F32)
    acc = acc + jnp.dot(yd_ref[...], w_ref[3 * GW:4 * GW, :], preferred_element_type=F32)
    o_ref[...] = x_ref[...] + _rms(acc, g_ref[...])


def _outproj(ya, yb, yc, yd, x2, w, gain, tm):
    t, d = x2.shape
    mix = pl.BlockSpec((tm, GW), lambda i: (i, 0))
    return pl.pallas_call(
        _outproj_body,
        grid=(t // tm,),
        in_specs=[mix, mix, mix, mix,
                  pl.BlockSpec((tm, d), lambda i: (i, 0)),
                  pl.BlockSpec((4 * GW, d), lambda i: (0, 0)),
                  pl.BlockSpec((1, d), lambda i: (0, 0))],
        out_specs=pl.BlockSpec((tm, d), lambda i: (i, 0)),
        out_shape=jax.ShapeDtypeStruct((t, d), F32),
        compiler_params=_cparams("parallel"),
        name="outproj",
    )(ya, yb, yc, yd, x2, w, gain)


def _mlp_body(x_ref, g1_ref, w1_ref, w2_ref, g2_ref, o_ref, h_ref, acc_ref):
    kk = pl.program_id(1)

    @pl.when(kk == 0)
    def _():
        h_ref[...] = _rms(x_ref[...], g1_ref[...]).astype(BF16)
        acc_ref[...] = jnp.zeros_like(acc_ref)

    a = jnp.dot(h_ref[...], w1_ref[...], preferred_element_type=F32)
    a = jnp.square(jnp.maximum(a, 0.0)).astype(BF16)
    acc_ref[...] += jnp.dot(a, w2_ref[...], preferred_element_type=F32)

    @pl.when(kk == pl.num_programs(1) - 1)
    def _():
        o_ref[...] = x_ref[...] + _rms(acc_ref[...], g2_ref[...])


def _mlp(x2, g1, w1, w2, g2, tm, tk):
    t, d = x2.shape
    hid = w1.shape[1]
    return pl.pallas_call(
        _mlp_body,
        grid=(t // tm, hid // tk),
        in_specs=[pl.BlockSpec((tm, d), lambda i, k: (i, 0)),
                  pl.BlockSpec((1, d), lambda i, k: (0, 0)),
                  pl.BlockSpec((d, tk), lambda i, k: (0, k)),
                  pl.BlockSpec((tk, d), lambda i, k: (k, 0)),
                  pl.BlockSpec((1, d), lambda i, k: (0, 0))],
        out_specs=pl.BlockSpec((tm, d), lambda i, k: (i, 0)),
        out_shape=jax.ShapeDtypeStruct((t, d), F32),
        scratch_shapes=[pltpu.VMEM((tm, d), BF16), pltpu.VMEM((tm, d), F32)],
        compiler_params=_cparams("parallel", "arbitrary"),
        name="mlp",
    )(x2, g1, w1, w2, g2)


def _rows(vectors, n_rows):
    tab = jnp.stack([v.astype(F32) for v in vectors])
    return jnp.pad(tab, ((0, n_rows - tab.shape[0]), (0, 0)))


def _block_diag(w):
    out = jnp.zeros((GW, GW), w.dtype)
    for n in range(w.shape[0]):
        out = out.at[n * HEAD_DIM:(n + 1) * HEAD_DIM, n * HEAD_DIM:(n + 1) * HEAD_DIM].set(w[n])
    return out


def _pick_block(n, want):
    while n % want:
        want //= 2
    return want


def kernel(x, norm_mix_pre, norm_mix_post, norm_mlp_pre, norm_mlp_post, w_in, w_out, attn_rel_bias, hgrn_lb_logits, hgrn_norm, rwkv_mu, rwkv_w0, rwkv_w2, rwkv_a0, rwkv_a2, rwkv_g2, rwkv_k_k, rwkv_k_a, rwkv_r_k, rwkv_ln_w, rwkv_ln_b, lru_conv_w, lru_conv_b, lru_wa, lru_ba, lru_wx, lru_bx, lru_lambda, mlp_w1, mlp_w2):
    b, s, d = x.shape
    depth = w_in.shape[0]
    t = b * s
    tm = _pick_block(t, 512)
    rb = _pick_block(s, 512)
    tk = _pick_block(mlp_w1.shape[-1], 1024)

    lb_sm = jax.nn.softmax(hgrn_lb_logits.astype(F32), axis=0)
    lb_all = jnp.maximum(jnp.cumsum(lb_sm, axis=0) - lb_sm[0:1], 0.0)

    x2 = x.reshape(t, d)
    for l in range(depth):
        pa, pb, pc, pd = _inproj(x2, norm_mix_pre[l].reshape(1, d), w_in[l].astype(BF16), tm)

        ya = _attention(pa.reshape(b, s, ATTN_COLS), _attn_bias_table(attn_rel_bias[l]), rb)

        lb = lb_all[l]
        hgrn_par = _rows([jnp.log(lb), jnp.log1p(-lb), 1.0 - lb, hgrn_norm[l]], 8)
        yb = _hgrn(pb.reshape(b, s, HGRN_COLS), hgrn_par, rb)

        mu = rwkv_mu[l].astype(F32)
        mu_lo = jnp.pad(mu[3 * GW:], (0, GW - RWKV_LORA))
        rwkv_par = _rows([mu[0:GW], mu[GW:2 * GW], mu[2 * GW:3 * GW], mu_lo, rwkv_w0[l], rwkv_a0[l],
                          rwkv_k_k[l], rwkv_k_a[l], rwkv_r_k[l].reshape(GW), rwkv_ln_w[l], rwkv_ln_b[l]], 16)
        zeros = lambda n: jnp.zeros((n, GW), F32)
        w2p = jnp.concatenate([rwkv_w2[l].astype(F32), zeros(96)], axis=0).astype(BF16)
        a2p = jnp.concatenate([zeros(32), rwkv_a2[l].astype(F32), zeros(64)], axis=0).astype(BF16)
        g2p = jnp.concatenate([zeros(64), rwkv_g2[l].astype(F32)], axis=0).astype(BF16)
        yc = _rwkv(pc.reshape(b, s, RWKV_COLS), rwkv_par, w2p, a2p, g2p, rb)

        cw = lru_conv_w[l].astype(F32)
        lru_par = _rows([cw[0], cw[1], cw[2], cw[3], lru_conv_b[l], lru_ba[l], lru_bx[l], lru_lambda[l]], 8)
        yd = _lru(pd.reshape(b, s, LRU_COLS), lru_par, _block_diag(lru_wa[l]).astype(BF16),
                  _block_diag(lru_wx[l]).astype(BF16), rb)

        x2 = _outproj(ya.reshape(t, GW), yb.reshape(t, GW), yc.reshape(t, GW), yd.reshape(t, GW),
                      x2, w_out[l].astype(BF16), norm_mix_post[l].reshape(1, d), tm)
        x2 = _mlp(x2, norm_mlp_pre[l].reshape(1, d), mlp_w1[l].astype(BF16), mlp_w2[l].astype(BF16),
                  norm_mlp_post[l].reshape(1, d), tm, tk)
    return x2.reshape(b, s, d)
```

```python
import functools
import math

import numpy as np

import jax
import jax.numpy as jnp
from jax import lax
from jax.experimental import pallas as pl
from jax.experimental.pallas import tpu as pltpu

F32 = jnp.float32
BF16 = jnp.bfloat16

GW = 256
HEAD_DIM = 64
N_HEADS = 4
CHUNK = 64
ATTN_LEFT_CHUNKS = 8
BAND = (ATTN_LEFT_CHUNKS + 1) * CHUNK
REL_CLIP = 256
ATTN_SCALE = HEAD_DIM ** -0.5
NEG_INF = -1e30
RMS_EPS = 1e-6
RWKV_GN_EPS = HEAD_DIM * 1e-5
RWKV_LORA = 128
LRU_C = 8.0
LRU_CONV = 4
HGRN_SUB = 16

ATTN_COLS = 3 * GW
HGRN_COLS = 4 * GW
RWKV_COLS = 3 * GW + RWKV_LORA
LRU_COLS = 2 * GW

VMEM_LIMIT = 56 * 1024 * 1024
RWKV_PASSES = (1, (3, 3, 3, 3, 3), 1, 1, 1)


def _cparams(*sem):
    return pltpu.CompilerParams(dimension_semantics=sem, vmem_limit_bytes=VMEM_LIMIT)


def _dot(a, b):
    return jnp.dot(a.astype(BF16), b.astype(BF16), preferred_element_type=F32)


def _dot_nt(a, b):
    return lax.dot_general(a.astype(BF16), b.astype(BF16), (((1,), (1,)), ((), ())),
                           preferred_element_type=F32)


def _dot_tn(a, b):
    return lax.dot_general(a.astype(BF16), b.astype(BF16), (((0,), (0,)), ((), ())),
                           preferred_element_type=F32)


def _split(a):
    hi = a.astype(BF16)
    lo = (a - hi.astype(F32)).astype(BF16)
    return hi, lo


def _mm(a, b, kind="nn", passes=1):
    f = {"nn": _dot, "nt": _dot_nt, "tn": _dot_tn}[kind]
    if passes == 1:
        return f(a, b)
    ah, al = _split(a)
    bh, bl = _split(b)
    return f(ah, bh) + (f(ah, bl) + f(al, bh))


def _dot_lhs2(a, b):
    ah, al = _split(a)
    return _dot(ah, b) + _dot(al, b)


def _rms(x, gain):
    return x * lax.rsqrt(jnp.mean(x * x, axis=-1, keepdims=True) + RMS_EPS) * gain


def _sigmoid(x):
    return 1.0 / (1.0 + jnp.exp(-x))


def _softplus(x):
    return jnp.maximum(x, 0.0) + jnp.log1p(jnp.exp(-jnp.abs(x)))


def _expm1(z):
    u = jnp.exp(z)
    um1 = u - 1.0
    near = um1 * z / jnp.where(u == 1.0, 1.0, jnp.log(u))
    return jnp.where(jnp.abs(z) > 0.5, um1, jnp.where(u == 1.0, z, near))


def _head_block_mask(n):
    r = lax.broadcasted_iota(jnp.int32, (n, n), 0) // HEAD_DIM
    c = lax.broadcasted_iota(jnp.int32, (n, n), 1) // HEAD_DIM
    return r == c


def _cumsum_rows(x):
    n = x.shape[0]
    row = lax.broadcasted_iota(jnp.int32, (n, 1), 0)
    d = 1
    while d < n:
        x = x + jnp.where(row >= d, pltpu.roll(x, d, 0), 0.0)
        d *= 2
    return x


def _stack_heads(x):
    lane_head = lax.broadcasted_iota(jnp.int32, (1, GW), 1) // HEAD_DIM
    return jnp.concatenate([jnp.where(lane_head == h, x, 0.0) for h in range(N_HEADS)], axis=0)


def _unstack_heads(xbd, c):
    return xbd[0:c] + xbd[c:2 * c] + xbd[2 * c:3 * c] + xbd[3 * c:4 * c]


def _inproj_body(x_ref, g_ref, w_ref, oa_ref, ob_ref, oc_ref, od_ref):
    h = _rms(x_ref[...], g_ref[...]).astype(BF16)
    o0, o1, o2 = ATTN_COLS, ATTN_COLS + HGRN_COLS, ATTN_COLS + HGRN_COLS + RWKV_COLS
    oa_ref[...] = jnp.dot(h, w_ref[:, 0:o0], preferred_element_type=F32).astype(BF16)
    ob_ref[...] = jnp.dot(h, w_ref[:, o0:o1], preferred_element_type=F32)
    oc_ref[...] = jnp.dot(h, w_ref[:, o1:o2], preferred_element_type=F32)
    od_ref[...] = jnp.dot(h, w_ref[:, o2:], preferred_element_type=F32)


def _inproj(x2, gain, w, tm):
    t, d = x2.shape
    d_in = w.shape[1]
    return pl.pallas_call(
        _inproj_body,
        grid=(t // tm,),
        in_specs=[pl.BlockSpec((tm, d), lambda i: (i, 0)),
                  pl.BlockSpec((1, d), lambda i: (0, 0)),
                  pl.BlockSpec((d, d_in), lambda i: (0, 0))],
        out_specs=[pl.BlockSpec((tm, ATTN_COLS), lambda i: (i, 0)),
                   pl.BlockSpec((tm, HGRN_COLS), lambda i: (i, 0)),
                   pl.BlockSpec((tm, RWKV_COLS), lambda i: (i, 0)),
                   pl.BlockSpec((tm, LRU_COLS), lambda i: (i, 0))],
        out_shape=[jax.ShapeDtypeStruct((t, ATTN_COLS), BF16),
                   jax.ShapeDtypeStruct((t, HGRN_COLS), F32),
                   jax.ShapeDtypeStruct((t, RWKV_COLS), F32),
                   jax.ShapeDtypeStruct((t, LRU_COLS), F32)],
        compiler_params=_cparams("parallel"),
        name="inproj",
    )(x2, gain, w)


def _attn_body(q_ref, k_ref, v_ref, bias_ref, o_ref, *, chunks):
    i = pl.program_id(1)
    lane_head = lax.broadcasted_iota(jnp.int32, (1, GW), 1) // HEAD_DIM
    kpos = lax.broadcasted_iota(jnp.int32, (1, BAND), 1)

    def chunk(j, carry):
        c = i * chunks + j
        rows = pl.ds(pl.multiple_of(j * CHUNK, CHUNK), CHUNK)
        q = q_ref[0, rows, :]
        r0 = pl.multiple_of(c * CHUNK, CHUNK)
        kw = k_ref[0, pl.ds(r0, BAND), :]
        vw = v_ref[0, pl.ds(r0, BAND), :]
        qbd = jnp.concatenate([jnp.where(lane_head == h, q, jnp.zeros_like(q)) for h in range(N_HEADS)],
                              axis=0)
        s = _dot_nt(qbd, kw) * ATTN_SCALE + bias_ref[...]
        s = jnp.where(kpos >= (ATTN_LEFT_CHUNKS - c) * CHUNK, s, NEG_INF)
        p = jnp.exp(s - jnp.max(s, axis=-1, keepdims=True))
        p = p / jnp.sum(p, axis=-1, keepdims=True)
        obd = _dot(p, vw)
        o = jnp.zeros((CHUNK, GW), F32)
        for h in range(N_HEADS):
            o = o + jnp.where(lane_head == h, obd[h * CHUNK:(h + 1) * CHUNK, :], 0.0)
        o_ref[0, rows, :] = o.astype(BF16)
        return carry

    lax.fori_loop(0, chunks, chunk, 0, unroll=2)


def _attention(pa3, bias, rb):
    b, s, _ = pa3.shape
    pad = ATTN_LEFT_CHUNKS * CHUNK
    kp = jnp.pad(pa3[:, :, GW:2 * GW], ((0, 0), (pad, 0), (0, 0)))
    vp = jnp.pad(pa3[:, :, 2 * GW:3 * GW], ((0, 0), (pad, 0), (0, 0)))
    return pl.pallas_call(
        functools.partial(_attn_body, chunks=rb // CHUNK),
        grid=(b, s // rb),
        in_specs=[pl.BlockSpec((1, rb, GW), lambda bi, i: (bi, i, 0)),
                  pl.BlockSpec((1, s + pad, GW), lambda bi, i: (bi, 0, 0)),
                  pl.BlockSpec((1, s + pad, GW), lambda bi, i: (bi, 0, 0)),
                  pl.BlockSpec((N_HEADS * CHUNK, BAND), lambda bi, i: (0, 0))],
        out_specs=pl.BlockSpec((1, rb, GW), lambda bi, i: (bi, i, 0)),
        out_shape=jax.ShapeDtypeStruct((b, s, GW), BF16),
        compiler_params=_cparams("parallel", "arbitrary"),
        name="attn",
    )(pa3, kp, vp, bias)


def _attn_bias_table(rel_bias):
    rel = np.arange(BAND + CHUNK - 1) - (CHUNK - 1) - ATTN_LEFT_CHUNKS * CHUNK
    ext = rel_bias.astype(F32)[:, np.clip(rel, -REL_CLIP, REL_CLIP) + REL_CLIP]
    tab = jnp.stack([ext[:, CHUNK - 1 - q:CHUNK - 1 - q + BAND] for q in range(CHUNK)], axis=1)
    return tab.reshape(N_HEADS * CHUNK, BAND)


def _hgrn_body(q_ref, f_ref, i_ref, g_ref, par_ref, o_ref, st_ref, oacc_ref, q_s, key_s, cum_s, *, rb):
    @pl.when(pl.program_id(1) == 0)
    def _():
        st_ref[...] = jnp.zeros_like(st_ref)

    m = HGRN_SUB
    log_lb = par_ref[0:1, :]
    log_1m_lb = par_ref[1:2, :]
    one_m_lb = par_ref[2:3, :]
    gain = par_ref[3:4, :]
    bd = _head_block_mask(GW)
    ones_bd = jnp.where(bd, 1.0, 0.0).astype(BF16)
    trow = lax.broadcasted_iota(jnp.int32, (m, 1), 0)

    fr = f_ref[0]
    log_sig = jnp.minimum(fr, 0.0) - jnp.log1p(jnp.exp(-jnp.abs(fr)))
    bt = log_1m_lb + log_sig
    cum = jnp.maximum(log_lb, bt) + jnp.log1p(jnp.exp(-jnp.abs(log_lb - bt)))
    sub_row = lax.broadcasted_iota(jnp.int32, (rb, 1), 0) % m
    d = 1
    while d < m:
        cum = cum + jnp.where(sub_row >= d, pltpu.roll(cum, d, 0), 0.0)
        d *= 2
    cum_s[...] = cum
    key_s[...] = one_m_lb * _sigmoid(-fr)
    qr = q_ref[0]
    q_s[...] = qr * _sigmoid(qr)

    def sub(j, carry):
        rows = pl.ds(pl.multiple_of(j * m, m), m)
        q, key, cum, v = q_s[rows, :], key_s[rows, :], cum_s[rows, :], i_ref[0, rows, :]
        last = cum[m - 1:m, :]
        st = st_ref[...]
        o = _dot_nt(q * jnp.exp(cum), st)
        ps = []
        for s in range(m):
            dec = jnp.exp(jnp.minimum(cum - cum[s:s + 1, :], 0.0))
            ps.append(jnp.where(trow >= s, q * dec * key[s:s + 1, :], 0.0))
        a_exp = _dot(jnp.concatenate(ps, axis=0), ones_bd)
        for s in range(m):
            o = o + a_exp[s * m:(s + 1) * m, :] * v[s:s + 1, :]
        oacc_ref[rows, :] = o
        kd = key * jnp.exp(last - cum)
        st_ref[...] = st * jnp.exp(last) + jnp.where(bd, _dot_tn(v, kd), 0.0)
        return carry

    lax.fori_loop(0, rb // m, sub, 0, unroll=2)
    o = oacc_ref[...]
    ms = _dot_lhs2(o * o, ones_bd) * (1.0 / HEAD_DIM)
    g = g_ref[0]
    o_ref[0] = (o * lax.rsqrt(ms + RMS_EPS) * gain * (g * _sigmoid(g))).astype(BF16)


def _hgrn(pb3, par, rb):
    b, s, _ = pb3.shape
    col = lambda n: pl.BlockSpec((1, rb, GW), lambda bi, i, n=n: (bi, i, n))
    return pl.pallas_call(
        functools.partial(_hgrn_body, rb=rb),
        grid=(b, s // rb),
        in_specs=[col(0), col(1), col(2), col(3),
                  pl.BlockSpec((8, GW), lambda bi, i: (0, 0))],
        out_specs=pl.BlockSpec((1, rb, GW), lambda bi, i: (bi, i, 0)),
        out_shape=jax.ShapeDtypeStruct((b, s, GW), BF16),
        scratch_shapes=[pltpu.VMEM((GW, GW), F32)] + [pltpu.VMEM((rb, GW), F32)] * 4,
        compiler_params=_cparams("parallel", "arbitrary"),
        name="hgrn",
    )(pb3, pb3, pb3, pb3, par)


def _rwkv_chunk(r, logw, k, v, alpha, beta, ht, bd, strict, strict_cat, incl_cat, eye):
    c = CHUNK
    p_pair, p_inv, p_app, p_out, p_state = RWKV_PASSES
    cum = _cumsum_rows(logw)
    e_last = jnp.exp(cum[c - 1:c, :])
    a_t = alpha * jnp.exp(cum - logw)
    r_t = r * jnp.exp(cum)
    e_inv = jnp.exp(-cum)
    b_t = beta * e_inv
    k_t = k * e_inv
    ar = jnp.concatenate([a_t, r_t], axis=0)
    bbd, kbd, vbd = _stack_heads(b_t), _stack_heads(k_t), _stack_heads(v)
    pair = _mm(ar, jnp.concatenate([bbd, kbd], axis=0), "nt", p_pair)
    l_ab = jnp.where(strict, jnp.concatenate([pair[0:c, 0:GW]] * N_HEADS, axis=0), 0.0)
    lm_k = jnp.where(jnp.concatenate([strict_cat, incl_cat], axis=0), pair[:, GW:], 0.0)
    m_rb = jnp.where(incl_cat, pair[c:, 0:GW], 0.0)
    x = l_ab
    inv = eye + l_ab
    for p_step in p_inv:
        x = _mm(x, x, "nn", p_step)
        inv = inv + _mm(inv, x, "nn", p_step)
    wy = _mm(ar, ht, "nt", p_app) + _mm(lm_k, vbd, "nn", p_app)
    ubd = _mm(inv, _stack_heads(wy[0:c]), "nn", p_app)
    y = wy[c:] + _mm(m_rb, ubd, "nn", p_out)
    u = _unstack_heads(ubd, c)
    upd = _mm(jnp.concatenate([u, v], axis=0), jnp.concatenate([b_t, k_t], axis=0) * e_last, "tn", p_state)
    ht_new = ht * e_last + jnp.where(bd, upd, 0.0)
    return y, ht_new


def _rwkv_body(pc_ref, par_ref, w2_ref, a2_ref, g2_ref, o_ref, ht_ref, prev_ref,
               r_s, w_s, k_s, v_s, al_s, be_s, y_s, *, rb):
    @pl.when(pl.program_id(1) == 0)
    def _():
        ht_ref[...] = jnp.zeros_like(ht_ref)
        prev_ref[...] = jnp.zeros_like(prev_ref)

    c = CHUNK
    mu_main = [par_ref[n:n + 1, :] for n in range(3)]
    mu_lo = par_ref[3:4, 0:RWKV_LORA]
    w0, a0 = par_ref[4:5, :], par_ref[5:6, :]
    k_k, k_a, r_k = par_ref[6:7, :], par_ref[7:8, :], par_ref[8:9, :]
    ln_w, ln_b = par_ref[9:10, :], par_ref[10:11, :]
    bd = _head_block_mask(GW)
    ones_bd = jnp.where(bd, 1.0, 0.0).astype(BF16)

    pc = pc_ref[0]
    row0 = lax.broadcasted_iota(jnp.int32, (rb, 1), 0) == 0
    prev = jnp.where(row0, prev_ref[...], pltpu.roll(pc, 1, 0))
    prev_ref[...] = pc[rb - 1:rb, :]
    xs = [pc[:, n * GW:(n + 1) * GW] for n in range(3)]
    ps = [prev[:, n * GW:(n + 1) * GW] for n in range(3)]
    r, k, v = [x + mu * (p - x) for x, p, mu in zip(xs, ps, mu_main)]
    lo, plo = pc[:, 3 * GW:], prev[:, 3 * GW:]
    lo = lo + mu_lo * (plo - lo)
    w_pre = -_softplus(-(w0 + _dot(jnp.tanh(lo), w2_ref[...]))) - 0.5
    a = _sigmoid(a0 + _dot(lo, a2_ref[...]))
    g = _dot(_sigmoid(lo), g2_ref[...])
    kk = k * k_k
    kk = kk / jnp.maximum(jnp.sqrt(_dot_lhs2(kk * kk, ones_bd)), 1e-12)
    k = k * (1.0 + (a - 1.0) * k_a)
    r_s[...] = r
    w_s[...] = -jnp.exp(w_pre)
    k_s[...] = k
    v_s[...] = v
    al_s[...] = -kk
    be_s[...] = kk * a

    rr = lax.broadcasted_iota(jnp.int32, (GW, GW), 0)
    cc = lax.broadcasted_iota(jnp.int32, (GW, GW), 1)
    strict = bd & (rr > cc)
    eye = jnp.where(rr == cc, 1.0, 0.0)
    tt = lax.broadcasted_iota(jnp.int32, (c, GW), 0)
    ss = lax.broadcasted_iota(jnp.int32, (c, GW), 1) % c
    strict_cat, incl_cat = tt > ss, tt >= ss

    def chunk(j, carry):
        rows = pl.ds(pl.multiple_of(j * c, c), c)
        y, ht_new = _rwkv_chunk(r_s[rows, :], w_s[rows, :], k_s[rows, :], v_s[rows, :], al_s[rows, :],
                                be_s[rows, :], ht_ref[...], bd, strict, strict_cat, incl_cat, eye)
        ht_ref[...] = ht_new
        y_s[rows, :] = y
        return carry

    lax.fori_loop(0, rb // c, chunk, 0, unroll=2)

    y = y_s[...]
    mean = _dot_lhs2(y, ones_bd) * (1.0 / HEAD_DIM)
    yc = y - mean
    var = _dot_lhs2(yc * yc, ones_bd) * (1.0 / HEAD_DIM)
    yn = yc * lax.rsqrt(var + RWKV_GN_EPS) * ln_w + ln_b
    bonus = _dot_lhs2(r * k * r_k, ones_bd) * v
    o_ref[0] = ((yn + bonus) * g).astype(BF16)


def _rwkv(pc3, par, w2p, a2p, g2p, rb):
    b, s, _ = pc3.shape
    full = lambda shp: pl.BlockSpec(shp, lambda bi, i: (0,) * len(shp))
    return pl.pallas_call(
        functools.partial(_rwkv_body, rb=rb),
        grid=(b, s // rb),
        in_specs=[pl.BlockSpec((1, rb, RWKV_COLS), lambda bi, i: (bi, i, 0)),
                  full((16, GW)), full((RWKV_LORA, GW)), full((RWKV_LORA, GW)), full((RWKV_LORA, GW))],
        out_specs=pl.BlockSpec((1, rb, GW), lambda bi, i: (bi, i, 0)),
        out_shape=jax.ShapeDtypeStruct((b, s, GW), BF16),
        scratch_shapes=[pltpu.VMEM((GW, GW), F32), pltpu.VMEM((1, RWKV_COLS), F32)]
        + [pltpu.VMEM((rb, GW), F32)] * 7,
        compiler_params=_cparams("parallel", "arbitrary"),
        name="rwkv",
    )(pc3, par, w2p, a2p, g2p)


def _lru_body(x_ref, g_ref, par_ref, wa_ref, wx_ref, o_ref, ext_ref, h_ref, *, rb):
    @pl.when(pl.program_id(1) == 0)
    def _():
        ext_ref[0:8, :] = jnp.zeros((8, GW), F32)
        h_ref[...] = jnp.zeros_like(h_ref)

    conv_b, ba, bx, lam = par_ref[4:5, :], par_ref[5:6, :], par_ref[6:7, :], par_ref[7:8, :]
    x = x_ref[0]
    ext_ref[8:8 + rb, :] = x
    conv = conv_b + par_ref[0:1, :] * ext_ref[pl.ds(8 - 3, rb), :]
    for j in range(1, LRU_CONV):
        conv = conv + par_ref[j:j + 1, :] * ext_ref[pl.ds(8 - 3 + j, rb), :]
    ext_ref[0:8, :] = x[rb - 8:rb, :]
    gate_r = _sigmoid(_dot(conv, wa_ref[...]) + ba)
    gate_i = _sigmoid(_dot(conv, wx_ref[...]) + bx)
    log_a = -LRU_C * gate_r * _softplus(-lam)
    a = jnp.exp(log_a)
    inp = jnp.sqrt(-_expm1(2.0 * log_a)) * (gate_i * conv)
    row = lax.broadcasted_iota(jnp.int32, (rb, 1), 0)
    d = 1
    while d < rb:
        m = row >= d
        inp = jnp.where(m, a * pltpu.roll(inp, d, 0) + inp, inp)
        a = jnp.where(m, a * pltpu.roll(a, d, 0), a)
        d *= 2
    h = inp + a * h_ref[...]
    h_ref[...] = h[rb - 1:rb, :]
    g = g_ref[0]
    gelu = 0.5 * g * (1.0 + jnp.tanh(math.sqrt(2.0 / math.pi) * (g + 0.044715 * (g * g * g))))
    o_ref[0] = (h * gelu).astype(BF16)


def _lru(pd3, par, wa_bd, wx_bd, rb):
    b, s, _ = pd3.shape
    full = lambda shp: pl.BlockSpec(shp, lambda bi, i: (0,) * len(shp))
    return pl.pallas_call(
        functools.partial(_lru_body, rb=rb),
        grid=(b, s // rb),
        in_specs=[pl.BlockSpec((1, rb, GW), lambda bi, i: (bi, i, 0)),
                  pl.BlockSpec((1, rb, GW), lambda bi, i: (bi, i, 1)),
                  full((8, GW)), full((GW, GW)), full((GW, GW))],
        out_specs=pl.BlockSpec((1, rb, GW), lambda bi, i: (bi, i, 0)),
        out_shape=jax.ShapeDtypeStruct((b, s, GW), BF16),
        scratch_shapes=[pltpu.VMEM((rb + 8, GW), F32), pltpu.VMEM((1, GW), F32)],
        compiler_params=_cparams("parallel", "arbitrary"),
        name="lru",
    )(pd3, pd3, par, wa_bd, wx_bd)


def _outproj_body(ya_ref, yb_ref, yc_ref, yd_ref, x_ref, w_ref, g_ref, o_ref):
    acc = jnp.dot(ya_ref[...], w_ref[0:GW, :], preferred_element_type=F32)
    acc = acc + jnp.dot(yb_ref[...], w_ref[GW:2 * GW, :], preferred_element_type=F32)
    acc = acc + jnp.dot(yc_ref[...], w_ref[2 * GW:3 * GW, :], preferred_element_type=F32)
    acc = acc + jnp.dot(yd_ref[...], w_ref[3 * GW:4 * GW, :], preferred_element_type=F32)
    o_ref[...] = x_ref[...] + _rms(acc, g_ref[...])


def _outproj(ya, yb, yc, yd, x2, w, gain, tm):
    t, d = x2.shape
    mix = pl.BlockSpec((tm, GW), lambda i: (i, 0))
    return pl.pallas_call(
        _outproj_body,
        grid=(t // tm,),
        in_specs=[mix, mix, mix, mix,
                  pl.BlockSpec((tm, d), lambda i: (i, 0)),
                  pl.BlockSpec((4 * GW, d), lambda i: (0, 0)),
                  pl.BlockSpec((1, d), lambda i: (0, 0))],
        out_specs=pl.BlockSpec((tm, d), lambda i: (i, 0)),
        out_shape=jax.ShapeDtypeStruct((t, d), F32),
        compiler_params=_cparams("parallel"),
        name="outproj",
    )(ya, yb, yc, yd, x2, w, gain)


def _mlp_body(x_ref, g1_ref, w1_ref, w2_ref, g2_ref, o_ref, h_ref, acc_ref):
    kk = pl.program_id(1)

    @pl.when(kk == 0)
    def _():
        h_ref[...] = _rms(x_ref[...], g1_ref[...]).astype(BF16)
        acc_ref[...] = jnp.zeros_like(acc_ref)

    a = jnp.dot(h_ref[...], w1_ref[...], preferred_element_type=F32)
    a = jnp.square(jnp.maximum(a, 0.0)).astype(BF16)
    acc_ref[...] += jnp.dot(a, w2_ref[...], preferred_element_type=F32)

    @pl.when(kk == pl.num_programs(1) - 1)
    def _():
        o_ref[...] = x_ref[...] + _rms(acc_ref[...], g2_ref[...])


def _mlp(x2, g1, w1, w2, g2, tm, tk):
    t, d = x2.shape
    hid = w1.shape[1]
    return pl.pallas_call(
        _mlp_body,
        grid=(t // tm, hid // tk),
        in_specs=[pl.BlockSpec((tm, d), lambda i, k: (i, 0)),
                  pl.BlockSpec((1, d), lambda i, k: (0, 0)),
                  pl.BlockSpec((d, tk), lambda i, k: (0, k)),
                  pl.BlockSpec((tk, d), lambda i, k: (k, 0)),
                  pl.BlockSpec((1, d), lambda i, k: (0, 0))],
        out_specs=pl.BlockSpec((tm, d), lambda i, k: (i, 0)),
        out_shape=jax.ShapeDtypeStruct((t, d), F32),
        scratch_shapes=[pltpu.VMEM((tm, d), BF16), pltpu.VMEM((tm, d), F32)],
        compiler_params=_cparams("parallel", "arbitrary"),
        name="mlp",
    )(x2, g1, w1, w2, g2)


def _rows(vectors, n_rows):
    tab = jnp.stack([v.astype(F32) for v in vectors])
    return jnp.pad(tab, ((0, n_rows - tab.shape[0]), (0, 0)))


def _block_diag(w):
    out = jnp.zeros((GW, GW), w.dtype)
    for n in range(w.shape[0]):
        out = out.at[n * HEAD_DIM:(n + 1) * HEAD_DIM, n * HEAD_DIM:(n + 1) * HEAD_DIM].set(w[n])
    return out


def _pick_block(n, want):
    while n % want:
        want //= 2
    return want


def kernel(x, norm_mix_pre, norm_mix_post, norm_mlp_pre, norm_mlp_post, w_in, w_out, attn_rel_bias, hgrn_lb_logits, hgrn_norm, rwkv_mu, rwkv_w0, rwkv_w2, rwkv_a0, rwkv_a2, rwkv_g2, rwkv_k_k, rwkv_k_a, rwkv_r_k, rwkv_ln_w, rwkv_ln_b, lru_conv_w, lru_conv_b, lru_wa, lru_ba, lru_wx, lru_bx, lru_lambda, mlp_w1, mlp_w2):
    b, s, d = x.shape
    depth = w_in.shape[0]
    t = b * s
    tm = _pick_block(t, 512)
    rb = _pick_block(s, 512)
    tk = _pick_block(mlp_w1.shape[-1], 1024)

    lb_sm = jax.nn.softmax(hgrn_lb_logits.astype(F32), axis=0)
    lb_all = jnp.maximum(jnp.cumsum(lb_sm, axis=0) - lb_sm[0:1], 0.0)

    x2 = x.reshape(t, d)
    for l in range(depth):
        pa, pb, pc, pd = _inproj(x2, norm_mix_pre[l].reshape(1, d), w_in[l].astype(BF16), tm)

        ya = _attention(pa.reshape(b, s, ATTN_COLS), _attn_bias_table(attn_rel_bias[l]), rb)

        lb = lb_all[l]
        hgrn_par = _rows([jnp.log(lb), jnp.log1p(-lb), 1.0 - lb, hgrn_norm[l]], 8)
        yb = _hgrn(pb.reshape(b, s, HGRN_COLS), hgrn_par, rb)

        mu = rwkv_mu[l].astype(F32)
        mu_lo = jnp.pad(mu[3 * GW:], (0, GW - RWKV_LORA))
        rwkv_par = _rows([mu[0:GW], mu[GW:2 * GW], mu[2 * GW:3 * GW], mu_lo, rwkv_w0[l], rwkv_a0[l],
                          rwkv_k_k[l], rwkv_k_a[l], rwkv_r_k[l].reshape(GW), rwkv_ln_w[l], rwkv_ln_b[l]], 16)
        zeros = lambda n: jnp.zeros((n, GW), F32)
        w2p = jnp.concatenate([rwkv_w2[l].astype(F32), zeros(96)], axis=0).astype(BF16)
        a2p = jnp.concatenate([zeros(32), rwkv_a2[l].astype(F32), zeros(64)], axis=0).astype(BF16)
        g2p = jnp.concatenate([zeros(64), rwkv_g2[l].astype(F32)], axis=0).astype(BF16)
        yc = _rwkv(pc.reshape(b, s, RWKV_COLS), rwkv_par, w2p, a2p, g2p, rb)

        cw = lru_conv_w[l].astype(F32)
        lru_par = _rows([cw[0], cw[1], cw[2], cw[3], lru_conv_b[l], lru_ba[l], lru_bx[l], lru_lambda[l]], 8)
        yd = _lru(pd.reshape(b, s, LRU_COLS), lru_par, _block_diag(lru_wa[l]).astype(BF16),
                  _block_diag(lru_wx[l]).astype(BF16), rb)

        x2 = _outproj(ya.reshape(t, GW), yb.reshape(t, GW), yc.reshape(t, GW), yd.reshape(t, GW),
                      x2, w_out[l].astype(BF16), norm_mix_post[l].reshape(1, d), tm)
        x2 = _mlp(x2, norm_mlp_pre[l].reshape(1, d), mlp_w1[l].astype(BF16), mlp_w2[l].astype(BF16),
                  norm_mlp_post[l].reshape(1, d), tm, tk)
    return x2.reshape(b, s, d)
```

```python
import functools
import math

import numpy as np

import jax
import jax.numpy as jnp
from jax import lax
from jax.experimental import pallas as pl
from jax.experimental.pallas import tpu as pltpu

F32 = jnp.float32
BF16 = jnp.bfloat16

GW = 256
HEAD_DIM = 64
N_HEADS = 4
CHUNK = 64
ATTN_LEFT_CHUNKS = 8
BAND = (ATTN_LEFT_CHUNKS + 1) * CHUNK
REL_CLIP = 256
ATTN_SCALE = HEAD_DIM ** -0.5
NEG_INF = -1e30
RMS_EPS = 1e-6
RWKV_GN_EPS = HEAD_DIM * 1e-5
RWKV_LORA = 128
LRU_C = 8.0
LRU_CONV = 4
HGRN_SUB = 16

ATTN_COLS = 3 * GW
HGRN_COLS = 4 * GW
RWKV_COLS = 3 * GW + RWKV_LORA
LRU_COLS = 2 * GW

VMEM_LIMIT = 56 * 1024 * 1024
RWKV_PASSES = (1, (3, 3, 3, 1, 1), 1)
RWKV_MAP_UNROLL = 4


def _cparams(*sem):
    return pltpu.CompilerParams(dimension_semantics=sem, vmem_limit_bytes=VMEM_LIMIT)


def _dot(a, b):
    return jnp.dot(a.astype(BF16), b.astype(BF16), preferred_element_type=F32)


def _dot_nt(a, b):
    return lax.dot_general(a.astype(BF16), b.astype(BF16), (((1,), (1,)), ((), ())),
                           preferred_element_type=F32)


def _dot_tn(a, b):
    return lax.dot_general(a.astype(BF16), b.astype(BF16), (((0,), (0,)), ((), ())),
                           preferred_element_type=F32)


def _split(a):
    hi = a.astype(BF16)
    lo = (a - hi.astype(F32)).astype(BF16)
    return hi, lo


def _mm(a, b, kind="nn", passes=1):
    f = {"nn": _dot, "nt": _dot_nt, "tn": _dot_tn}[kind]
    if passes == 1:
        return f(a, b)
    ah, al = _split(a)
    bh, bl = _split(b)
    return f(ah, bh) + (f(ah, bl) + f(al, bh))


def _dot_lhs2(a, b):
    ah, al = _split(a)
    return _dot(ah, b) + _dot(al, b)


def _rms(x, gain):
    return x * lax.rsqrt(jnp.mean(x * x, axis=-1, keepdims=True) + RMS_EPS) * gain


def _sigmoid(x):
    return 1.0 / (1.0 + jnp.exp(-x))


def _softplus(x):
    return jnp.maximum(x, 0.0) + jnp.log1p(jnp.exp(-jnp.abs(x)))


def _expm1(z):
    u = jnp.exp(z)
    um1 = u - 1.0
    near = um1 * z / jnp.where(u == 1.0, 1.0, jnp.log(u))
    return jnp.where(jnp.abs(z) > 0.5, um1, jnp.where(u == 1.0, z, near))


def _head_block_mask(n):
    r = lax.broadcasted_iota(jnp.int32, (n, n), 0) // HEAD_DIM
    c = lax.broadcasted_iota(jnp.int32, (n, n), 1) // HEAD_DIM
    return r == c


def _cumsum_rows(x):
    n = x.shape[0]
    row = lax.broadcasted_iota(jnp.int32, (n, 1), 0)
    d = 1
    while d < n:
        x = x + jnp.where(row >= d, pltpu.roll(x, d, 0), 0.0)
        d *= 2
    return x


def _stack_heads(x):
    lane_head = lax.broadcasted_iota(jnp.int32, (1, GW), 1) // HEAD_DIM
    return jnp.concatenate([jnp.where(lane_head == h, x, 0.0) for h in range(N_HEADS)], axis=0)


def _unstack_heads(xbd, c):
    return xbd[0:c] + xbd[c:2 * c] + xbd[2 * c:3 * c] + xbd[3 * c:4 * c]


def _inproj_body(x_ref, g_ref, w_ref, oa_ref, ob_ref, oc_ref, od_ref):
    h = _rms(x_ref[...], g_ref[...]).astype(BF16)
    o0, o1, o2 = ATTN_COLS, ATTN_COLS + HGRN_COLS, ATTN_COLS + HGRN_COLS + RWKV_COLS
    oa_ref[...] = jnp.dot(h, w_ref[:, 0:o0], preferred_element_type=F32).astype(BF16)
    ob_ref[...] = jnp.dot(h, w_ref[:, o0:o1], preferred_element_type=F32)
    oc_ref[...] = jnp.dot(h, w_ref[:, o1:o2], preferred_element_type=F32)
    od_ref[...] = jnp.dot(h, w_ref[:, o2:], preferred_element_type=F32)


def _inproj(x2, gain, w, tm):
    t, d = x2.shape
    d_in = w.shape[1]
    return pl.pallas_call(
        _inproj_body,
        grid=(t // tm,),
        in_specs=[pl.BlockSpec((tm, d), lambda i: (i, 0)),
                  pl.BlockSpec((1, d), lambda i: (0, 0)),
                  pl.BlockSpec((d, d_in), lambda i: (0, 0))],
        out_specs=[pl.BlockSpec((tm, ATTN_COLS), lambda i: (i, 0)),
                   pl.BlockSpec((tm, HGRN_COLS), lambda i: (i, 0)),
                   pl.BlockSpec((tm, RWKV_COLS), lambda i: (i, 0)),
                   pl.BlockSpec((tm, LRU_COLS), lambda i: (i, 0))],
        out_shape=[jax.ShapeDtypeStruct((t, ATTN_COLS), BF16),
                   jax.ShapeDtypeStruct((t, HGRN_COLS), F32),
                   jax.ShapeDtypeStruct((t, RWKV_COLS), F32),
                   jax.ShapeDtypeStruct((t, LRU_COLS), F32)],
        compiler_params=_cparams("parallel"),
        name="inproj",
    )(x2, gain, w)


def _attn_body(q_ref, k_ref, v_ref, bias_ref, o_ref, *, chunks):
    i = pl.program_id(1)
    lane_head = lax.broadcasted_iota(jnp.int32, (1, GW), 1) // HEAD_DIM
    kpos = lax.broadcasted_iota(jnp.int32, (1, BAND), 1)

    def chunk(j, carry):
        c = i * chunks + j
        rows = pl.ds(pl.multiple_of(j * CHUNK, CHUNK), CHUNK)
        q = q_ref[0, rows, :]
        r0 = pl.multiple_of(c * CHUNK, CHUNK)
        kw = k_ref[0, pl.ds(r0, BAND), :]
        vw = v_ref[0, pl.ds(r0, BAND), :]
        qbd = jnp.concatenate([jnp.where(lane_head == h, q, jnp.zeros_like(q)) for h in range(N_HEADS)],
                              axis=0)
        s = _dot_nt(qbd, kw) * ATTN_SCALE + bias_ref[...]
        s = jnp.where(kpos >= (ATTN_LEFT_CHUNKS - c) * CHUNK, s, NEG_INF)
        p = jnp.exp(s - jnp.max(s, axis=-1, keepdims=True))
        p = p / jnp.sum(p, axis=-1, keepdims=True)
        obd = _dot(p, vw)
        o = jnp.zeros((CHUNK, GW), F32)
        for h in range(N_HEADS):
            o = o + jnp.where(lane_head == h, obd[h * CHUNK:(h + 1) * CHUNK, :], 0.0)
        o_ref[0, rows, :] = o.astype(BF16)
        return carry

    lax.fori_loop(0, chunks, chunk, 0, unroll=2)


def _attention(pa3, bias, rb):
    b, s, _ = pa3.shape
    pad = ATTN_LEFT_CHUNKS * CHUNK
    kp = jnp.pad(pa3[:, :, GW:2 * GW], ((0, 0), (pad, 0), (0, 0)))
    vp = jnp.pad(pa3[:, :, 2 * GW:3 * GW], ((0, 0), (pad, 0), (0, 0)))
    return pl.pallas_call(
        functools.partial(_attn_body, chunks=rb // CHUNK),
        grid=(b, s // rb),
        in_specs=[pl.BlockSpec((1, rb, GW), lambda bi, i: (bi, i, 0)),
                  pl.BlockSpec((1, s + pad, GW), lambda bi, i: (bi, 0, 0)),
                  pl.BlockSpec((1, s + pad, GW), lambda bi, i: (bi, 0, 0)),
                  pl.BlockSpec((N_HEADS * CHUNK, BAND), lambda bi, i: (0, 0))],
        out_specs=pl.BlockSpec((1, rb, GW), lambda bi, i: (bi, i, 0)),
        out_shape=jax.ShapeDtypeStruct((b, s, GW), BF16),
        compiler_params=_cparams("parallel", "arbitrary"),
        name="attn",
    )(pa3, kp, vp, bias)


def _attn_bias_table(rel_bias):
    rel = np.arange(BAND + CHUNK - 1) - (CHUNK - 1) - ATTN_LEFT_CHUNKS * CHUNK
    ext = rel_bias.astype(F32)[:, np.clip(rel, -REL_CLIP, REL_CLIP) + REL_CLIP]
    tab = jnp.stack([ext[:, CHUNK - 1 - q:CHUNK - 1 - q + BAND] for q in range(CHUNK)], axis=1)
    return tab.reshape(N_HEADS * CHUNK, BAND)


def _hgrn_body(q_ref, f_ref, i_ref, g_ref, par_ref, o_ref, st_ref, oacc_ref, q_s, key_s, cum_s, *, rb):
    @pl.when(pl.program_id(1) == 0)
    def _():
        st_ref[...] = jnp.zeros_like(st_ref)

    m = HGRN_SUB
    log_lb = par_ref[0:1, :]
    log_1m_lb = par_ref[1:2, :]
    one_m_lb = par_ref[2:3, :]
    gain = par_ref[3:4, :]
    bd = _head_block_mask(GW)
    ones_bd = jnp.where(bd, 1.0, 0.0).astype(BF16)
    trow = lax.broadcasted_iota(jnp.int32, (m, 1), 0)

    fr = f_ref[0]
    log_sig = jnp.minimum(fr, 0.0) - jnp.log1p(jnp.exp(-jnp.abs(fr)))
    bt = log_1m_lb + log_sig
    cum = jnp.maximum(log_lb, bt) + jnp.log1p(jnp.exp(-jnp.abs(log_lb - bt)))
    sub_row = lax.broadcasted_iota(jnp.int32, (rb, 1), 0) % m
    d = 1
    while d < m:
        cum = cum + jnp.where(sub_row >= d, pltpu.roll(cum, d, 0), 0.0)
        d *= 2
    cum_s[...] = cum
    key_s[...] = one_m_lb * _sigmoid(-fr)
    qr = q_ref[0]
    q_s[...] = qr * _sigmoid(qr)

    def sub(j, carry):
        rows = pl.ds(pl.multiple_of(j * m, m), m)
        q, key, cum, v = q_s[rows, :], key_s[rows, :], cum_s[rows, :], i_ref[0, rows, :]
        last = cum[m - 1:m, :]
        st = st_ref[...]
        o = _dot_nt(q * jnp.exp(cum), st)
        ps = []
        for s in range(m):
            dec = jnp.exp(jnp.minimum(cum - cum[s:s + 1, :], 0.0))
            ps.append(jnp.where(trow >= s, q * dec * key[s:s + 1, :], 0.0))
        a_exp = _dot(jnp.concatenate(ps, axis=0), ones_bd)
        for s in range(m):
            o = o + a_exp[s * m:(s + 1) * m, :] * v[s:s + 1, :]
        oacc_ref[rows, :] = o
        kd = key * jnp.exp(last - cum)
        st_ref[...] = st * jnp.exp(last) + jnp.where(bd, _dot_tn(v, kd), 0.0)
        return carry

    lax.fori_loop(0, rb // m, sub, 0, unroll=2)
    o = oacc_ref[...]
    ms = _dot_lhs2(o * o, ones_bd) * (1.0 / HEAD_DIM)
    g = g_ref[0]
    o_ref[0] = (o * lax.rsqrt(ms + RMS_EPS) * gain * (g * _sigmoid(g))).astype(BF16)


def _hgrn(pb3, par, rb):
    b, s, _ = pb3.shape
    col = lambda n: pl.BlockSpec((1, rb, GW), lambda bi, i, n=n: (bi, i, n))
    return pl.pallas_call(
        functools.partial(_hgrn_body, rb=rb),
        grid=(b, s // rb),
        in_specs=[col(0), col(1), col(2), col(3),
                  pl.BlockSpec((8, GW), lambda bi, i: (0, 0))],
        out_specs=pl.BlockSpec((1, rb, GW), lambda bi, i: (bi, i, 0)),
        out_shape=jax.ShapeDtypeStruct((b, s, GW), BF16),
        scratch_shapes=[pltpu.VMEM((GW, GW), F32)] + [pltpu.VMEM((rb, GW), F32)] * 4,
        compiler_params=_cparams("parallel", "arbitrary"),
        name="hgrn",
    )(pb3, pb3, pb3, pb3, par)


def _expand_heads(x_cat, bd):
    return jnp.where(bd, jnp.concatenate([x_cat] * N_HEADS, axis=0), 0.0)


def _rwkv_chunk_maps(r, logw, k, v, alpha, beta, bd, strict_cat, incl_cat, eye_cat):
    c = CHUNK
    p_pair, p_inv, p_app = RWKV_PASSES
    cum = _cumsum_rows(logw)
    e_last = jnp.exp(cum[c - 1:c, :])
    a_t = alpha * jnp.exp(cum - logw)
    r_t = r * jnp.exp(cum)
    e_inv = jnp.exp(-cum)
    b_t = beta * e_inv
    k_t = k * e_inv
    ar = jnp.concatenate([a_t, r_t], axis=0)
    bbd, kbd, vbd = _stack_heads(b_t), _stack_heads(k_t), _stack_heads(v)
    pair = _mm(ar, jnp.concatenate([bbd, kbd], axis=0), "nt", p_pair)
    l_ab = jnp.where(strict_cat, pair[0:c, 0:GW], 0.0)
    lm_k = jnp.where(jnp.concatenate([strict_cat, incl_cat], axis=0), pair[:, GW:], 0.0)
    m_rb = jnp.where(incl_cat, pair[c:, 0:GW], 0.0)
    x = l_ab
    inv = eye_cat + l_ab
    xbd = _expand_heads(x, bd)
    for p_step in p_inv:
        x = _mm(x, xbd, "nn", p_step)
        xbd = _expand_heads(x, bd)
        inv = inv + _mm(inv, xbd, "nn", p_step)
    wy0 = _mm(lm_k, vbd, "nn", p_app)
    sol = _mm(inv, jnp.concatenate([_stack_heads(a_t), _stack_heads(wy0[0:c])], axis=1), "nn", p_app)
    a_s, u0 = sol[:, 0:GW], sol[:, GW:]
    out = _mm(m_rb, jnp.concatenate([_stack_heads(a_s), _stack_heads(u0)], axis=1), "nn", p_app)
    ry = r_t + out[:, 0:GW]
    y0 = wy0[c:] + out[:, GW:]
    bh, kh = b_t * e_last, k_t * e_last
    g = jnp.where(bd, _mm(a_s, bh, "tn", p_app), 0.0)
    c0 = jnp.where(bd, _mm(jnp.concatenate([u0, v], axis=0), jnp.concatenate([bh, kh], axis=0), "tn", p_app), 0.0)
    return ry, y0, g, c0, e_last


def _rwkv_body(pc_ref, par_ref, w2_ref, a2_ref, g2_ref, o_ref, ht_ref, prev_ref,
               r_s, w_s, k_s, v_s, al_s, be_s, ry_s, y_s, bon_s, gate_s, g_s, c_s, el_s, *, rb):
    @pl.when(pl.program_id(1) == 0)
    def _():
        ht_ref[...] = jnp.zeros_like(ht_ref)
        prev_ref[...] = jnp.zeros_like(prev_ref)

    c = CHUNK
    mu_main = [par_ref[n:n + 1, :] for n in range(3)]
    mu_lo = par_ref[3:4, 0:RWKV_LORA]
    w0, a0 = par_ref[4:5, :], par_ref[5:6, :]
    k_k, k_a, r_k = par_ref[6:7, :], par_ref[7:8, :], par_ref[8:9, :]
    ln_w, ln_b = par_ref[9:10, :], par_ref[10:11, :]
    bd = _head_block_mask(GW)
    ones_bd = jnp.where(bd, 1.0, 0.0).astype(BF16)

    pc = pc_ref[0]
    row0 = lax.broadcasted_iota(jnp.int32, (rb, 1), 0) == 0
    prev = jnp.where(row0, prev_ref[...], pltpu.roll(pc, 1, 0))
    prev_ref[...] = pc[rb - 1:rb, :]
    xs = [pc[:, n * GW:(n + 1) * GW] for n in range(3)]
    ps = [prev[:, n * GW:(n + 1) * GW] for n in range(3)]
    r, k, v = [x + mu * (p - x) for x, p, mu in zip(xs, ps, mu_main)]
    lo, plo = pc[:, 3 * GW:], prev[:, 3 * GW:]
    lo = lo + mu_lo * (plo - lo)
    w_pre = -_softplus(-(w0 + _dot(jnp.tanh(lo), w2_ref[...]))) - 0.5
    a = _sigmoid(a0 + _dot(lo, a2_ref[...]))
    g = _dot(_sigmoid(lo), g2_ref[...])
    kk = k * k_k
    kk = kk / jnp.maximum(jnp.sqrt(_dot_lhs2(kk * kk, ones_bd)), 1e-12)
    k = k * (1.0 + (a - 1.0) * k_a)
    r_s[...] = r
    w_s[...] = -jnp.exp(w_pre)
    k_s[...] = k
    v_s[...] = v
    al_s[...] = -kk
    be_s[...] = kk * a
    bon_s[...] = _dot_lhs2(r * k * r_k, ones_bd) * v
    gate_s[...] = g

    tt = lax.broadcasted_iota(jnp.int32, (c, GW), 0)
    ss = lax.broadcasted_iota(jnp.int32, (c, GW), 1) % c
    strict_cat, incl_cat = tt > ss, tt >= ss
    eye_cat = jnp.where(tt == ss, 1.0, 0.0)

    def chunk_maps(j, carry):
        rows = pl.ds(pl.multiple_of(j * c, c), c)
        ry, y0, g, c0, e_last = _rwkv_chunk_maps(r_s[rows, :], w_s[rows, :], k_s[rows, :], v_s[rows, :],
                                                 al_s[rows, :], be_s[rows, :], bd, strict_cat, incl_cat, eye_cat)
        ry_s[rows, :] = ry
        y_s[rows, :] = y0
        g_s[j] = g.astype(BF16)
        c_s[j] = c0
        el_s[j] = jnp.broadcast_to(e_last, (8, GW))
        return carry

    lax.fori_loop(0, rb // c, chunk_maps, 0, unroll=RWKV_MAP_UNROLL)

    def chunk_apply(j, carry):
        rows = pl.ds(pl.multiple_of(j * c, c), c)
        ht = ht_ref[...]
        htb = ht.astype(BF16)
        y_s[rows, :] = y_s[rows, :] + _dot_nt(ry_s[rows, :], htb)
        ht_ref[...] = ht * el_s[j][0:1, :] + jnp.dot(htb, g_s[j], preferred_element_type=F32) + c_s[j]
        return carry

    lax.fori_loop(0, rb // c, chunk_apply, 0, unroll=2)

    y = y_s[...]
    mean = _dot_lhs2(y, ones_bd) * (1.0 / HEAD_DIM)
    yc = y - mean
    var = _dot_lhs2(yc * yc, ones_bd) * (1.0 / HEAD_DIM)
    yn = yc * lax.rsqrt(var + RWKV_GN_EPS) * ln_w + ln_b
    o_ref[0] = ((yn + bon_s[...]) * gate_s[...]).astype(BF16)


def _rwkv(pc3, par, w2p, a2p, g2p, rb):
    b, s, _ = pc3.shape
    full = lambda shp: pl.BlockSpec(shp, lambda bi, i: (0,) * len(shp))
    return pl.pallas_call(
        functools.partial(_rwkv_body, rb=rb),
        grid=(b, s // rb),
        in_specs=[pl.BlockSpec((1, rb, RWKV_COLS), lambda bi, i: (bi, i, 0)),
                  full((16, GW)), full((RWKV_LORA, GW)), full((RWKV_LORA, GW)), full((RWKV_LORA, GW))],
        out_specs=pl.BlockSpec((1, rb, GW), lambda bi, i: (bi, i, 0)),
        out_shape=jax.ShapeDtypeStruct((b, s, GW), BF16),
        scratch_shapes=[pltpu.VMEM((GW, GW), F32), pltpu.VMEM((1, RWKV_COLS), F32)]
        + [pltpu.VMEM((rb, GW), F32)] * 10
        + [pltpu.VMEM((rb // CHUNK, GW, GW), BF16), pltpu.VMEM((rb // CHUNK, GW, GW), F32),
           pltpu.VMEM((rb // CHUNK, 8, GW), F32)],
        compiler_params=_cparams("parallel", "arbitrary"),
        name="rwkv",
    )(pc3, par, w2p, a2p, g2p)


def _lru_body(x_ref, g_ref, par_ref, wa_ref, wx_ref, o_ref, ext_ref, h_ref, *, rb):
    @pl.when(pl.program_id(1) == 0)
    def _():
        ext_ref[0:8, :] = jnp.zeros((8, GW), F32)
        h_ref[...] = jnp.zeros_like(h_ref)

    conv_b, ba, bx, lam = par_ref[4:5, :], par_ref[5:6, :], par_ref[6:7, :], par_ref[7:8, :]
    x = x_ref[0]
    ext_ref[8:8 + rb, :] = x
    conv = conv_b + par_ref[0:1, :] * ext_ref[pl.ds(8 - 3, rb), :]
    for j in range(1, LRU_CONV):
        conv = conv + par_ref[j:j + 1, :] * ext_ref[pl.ds(8 - 3 + j, rb), :]
    ext_ref[0:8, :] = x[rb - 8:rb, :]
    gate_r = _sigmoid(_dot(conv, wa_ref[...]) + ba)
    gate_i = _sigmoid(_dot(conv, wx_ref[...]) + bx)
    log_a = -LRU_C * gate_r * _softplus(-lam)
    a = jnp.exp(log_a)
    inp = jnp.sqrt(-_expm1(2.0 * log_a)) * (gate_i * conv)
    row = lax.broadcasted_iota(jnp.int32, (rb, 1), 0)
    d = 1
    while d < rb:
        m = row >= d
        inp = jnp.where(m, a * pltpu.roll(inp, d, 0) + inp, inp)
        a = jnp.where(m, a * pltpu.roll(a, d, 0), a)
        d *= 2
    h = inp + a * h_ref[...]
    h_ref[...] = h[rb - 1:rb, :]
    g = g_ref[0]
    gelu = 0.5 * g * (1.0 + jnp.tanh(math.sqrt(2.0 / math.pi) * (g + 0.044715 * (g * g * g))))
    o_ref[0] = (h * gelu).astype(BF16)


def _lru(pd3, par, wa_bd, wx_bd, rb):
    b, s, _ = pd3.shape
    full = lambda shp: pl.BlockSpec(shp, lambda bi, i: (0,) * len(shp))
    return pl.pallas_call(
        functools.partial(_lru_body, rb=rb),
        grid=(b, s // rb),
        in_specs=[pl.BlockSpec((1, rb, GW), lambda bi, i: (bi, i, 0)),
                  pl.BlockSpec((1, rb, GW), lambda bi, i: (bi, i, 1)),
                  full((8, GW)), full((GW, GW)), full((GW, GW))],
        out_specs=pl.BlockSpec((1, rb, GW), lambda bi, i: (bi, i, 0)),
        out_shape=jax.ShapeDtypeStruct((b, s, GW), BF16),
        scratch_shapes=[pltpu.VMEM((rb + 8, GW), F32), pltpu.VMEM((1, GW), F32)],
        compiler_params=_cparams("parallel", "arbitrary"),
        name="lru",
    )(pd3, pd3, par, wa_bd, wx_bd)


def _outproj_body(ya_ref, yb_ref, yc_ref, yd_ref, x_ref, w_ref, g_ref, o_ref):
    acc = jnp.dot(ya_ref[...], w_ref[0:GW, :], preferred_element_type=F32)
    acc = acc + jnp.dot(yb_ref[...], w_ref[GW:2 * GW, :], preferred_element_type=F32)
    acc = acc + jnp.dot(yc_ref[...], w_ref[2 * GW:3 * GW, :], preferred_element_type=F32)
    acc = acc + jnp.dot(yd_ref[...], w_ref[3 * GW:4 * GW, :], preferred_element_type=F32)
    o_ref[...] = x_ref[...] + _rms(acc, g_ref[...])


def _outproj(ya, yb, yc, yd, x2, w, gain, tm):
    t, d = x2.shape
    mix = pl.BlockSpec((tm, GW), lambda i: (i, 0))
    return pl.pallas_call(
        _outproj_body,
        grid=(t // tm,),
        in_specs=[mix, mix, mix, mix,
                  pl.BlockSpec((tm, d), lambda i: (i, 0)),
                  pl.BlockSpec((4 * GW, d), lambda i: (0, 0)),
                  pl.BlockSpec((1, d), lambda i: (0, 0))],
        out_specs=pl.BlockSpec((tm, d), lambda i: (i, 0)),
        out_shape=jax.ShapeDtypeStruct((t, d), F32),
        compiler_params=_cparams("parallel"),
        name="outproj",
    )(ya, yb, yc, yd, x2, w, gain)


def _mlp_body(x_ref, g1_ref, w1_ref, w2_ref, g2_ref, o_ref, h_ref, acc_ref):
    kk = pl.program_id(1)

    @pl.when(kk == 0)
    def _():
        h_ref[...] = _rms(x_ref[...], g1_ref[...]).astype(BF16)
        acc_ref[...] = jnp.zeros_like(acc_ref)

    a = jnp.dot(h_ref[...], w1_ref[...], preferred_element_type=F32)
    a = jnp.square(jnp.maximum(a, 0.0)).astype(BF16)
    acc_ref[...] += jnp.dot(a, w2_ref[...], preferred_element_type=F32)

    @pl.when(kk == pl.num_programs(1) - 1)
    def _():
        o_ref[...] = x_ref[...] + _rms(acc_ref[...], g2_ref[...])


def _mlp(x2, g1, w1, w2, g2, tm, tk):
    t, d = x2.shape
    hid = w1.shape[1]
    return pl.pallas_call(
        _mlp_body,
        grid=(t // tm, hid // tk),
        in_specs=[pl.BlockSpec((tm, d), lambda i, k: (i, 0)),
                  pl.BlockSpec((1, d), lambda i, k: (0, 0)),
                  pl.BlockSpec((d, tk), lambda i, k: (0, k)),
                  pl.BlockSpec((tk, d), lambda i, k: (k, 0)),
                  pl.BlockSpec((1, d), lambda i, k: (0, 0))],
        out_specs=pl.BlockSpec((tm, d), lambda i, k: (i, 0)),
        out_shape=jax.ShapeDtypeStruct((t, d), F32),
        scratch_shapes=[pltpu.VMEM((tm, d), BF16), pltpu.VMEM((tm, d), F32)],
        compiler_params=_cparams("parallel", "arbitrary"),
        name="mlp",
    )(x2, g1, w1, w2, g2)


def _rows(vectors, n_rows):
    tab = jnp.stack([v.astype(F32) for v in vectors])
    return jnp.pad(tab, ((0, n_rows - tab.shape[0]), (0, 0)))


def _block_diag(w):
    out = jnp.zeros((GW, GW), w.dtype)
    for n in range(w.shape[0]):
        out = out.at[n * HEAD_DIM:(n + 1) * HEAD_DIM, n * HEAD_DIM:(n + 1) * HEAD_DIM].set(w[n])
    return out


def _pick_block(n, want):
    while n % want:
        want //= 2
    return want


def kernel(x, norm_mix_pre, norm_mix_post, norm_mlp_pre, norm_mlp_post, w_in, w_out, attn_rel_bias, hgrn_lb_logits, hgrn_norm, rwkv_mu, rwkv_w0, rwkv_w2, rwkv_a0, rwkv_a2, rwkv_g2, rwkv_k_k, rwkv_k_a, rwkv_r_k, rwkv_ln_w, rwkv_ln_b, lru_conv_w, lru_conv_b, lru_wa, lru_ba, lru_wx, lru_bx, lru_lambda, mlp_w1, mlp_w2):
    b, s, d = x.shape
    depth = w_in.shape[0]
    t = b * s
    tm = _pick_block(t, 512)
    rb = _pick_block(s, 512)
    tk = _pick_block(mlp_w1.shape[-1], 1024)

    lb_sm = jax.nn.softmax(hgrn_lb_logits.astype(F32), axis=0)
    lb_all = jnp.maximum(jnp.cumsum(lb_sm, axis=0) - lb_sm[0:1], 0.0)

    x2 = x.reshape(t, d)
    for l in range(depth):
        pa, pb, pc, pd = _inproj(x2, norm_mix_pre[l].reshape(1, d), w_in[l].astype(BF16), tm)

        ya = _attention(pa.reshape(b, s, ATTN_COLS), _attn_bias_table(attn_rel_bias[l]), rb)

        lb = lb_all[l]
        hgrn_par = _rows([jnp.log(lb), jnp.log1p(-lb), 1.0 - lb, hgrn_norm[l]], 8)
        yb = _hgrn(pb.reshape(b, s, HGRN_COLS), hgrn_par, rb)

        mu = rwkv_mu[l].astype(F32)
        mu_lo = jnp.pad(mu[3 * GW:], (0, GW - RWKV_LORA))
        rwkv_par = _rows([mu[0:GW], mu[GW:2 * GW], mu[2 * GW:3 * GW], mu_lo, rwkv_w0[l], rwkv_a0[l],
                          rwkv_k_k[l], rwkv_k_a[l], rwkv_r_k[l].reshape(GW), rwkv_ln_w[l], rwkv_ln_b[l]], 16)
        zeros = lambda n: jnp.zeros((n, GW), F32)
        w2p = jnp.concatenate([rwkv_w2[l].astype(F32), zeros(96)], axis=0).astype(BF16)
        a2p = jnp.concatenate([zeros(32), rwkv_a2[l].astype(F32), zeros(64)], axis=0).astype(BF16)
        g2p = jnp.concatenate([zeros(64), rwkv_g2[l].astype(F32)], axis=0).astype(BF16)
        yc = _rwkv(pc.reshape(b, s, RWKV_COLS), rwkv_par, w2p, a2p, g2p, rb)

        cw = lru_conv_w[l].astype(F32)
        lru_par = _rows([cw[0], cw[1], cw[2], cw[3], lru_conv_b[l], lru_ba[l], lru_bx[l], lru_lambda[l]], 8)
        yd = _lru(pd.reshape(b, s, LRU_COLS), lru_par, _block_diag(lru_wa[l]).astype(BF16),
                  _block_diag(lru_wx[l]).astype(BF16), rb)

        x2 = _outproj(ya.reshape(t, GW), yb.reshape(t, GW), yc.reshape(t, GW), yd.reshape(t, GW),
                      x2, w_out[l].astype(BF16), norm_mix_post[l].reshape(1, d), tm)
        x2 = _mlp(x2, norm_mlp_pre[l].reshape(1, d), mlp_w1[l].astype(BF16), mlp_w2[l].astype(BF16),
                  norm_mlp_post[l].reshape(1, d), tm, tk)
    return x2.reshape(b, s, d)
```

```python
import functools
import math

import numpy as np

import jax
import jax.numpy as jnp
from jax import lax
from jax.experimental import pallas as pl
from jax.experimental.pallas import tpu as pltpu

F32 = jnp.float32
BF16 = jnp.bfloat16

GW = 256
HEAD_DIM = 64
N_HEADS = 4
CHUNK = 64
ATTN_LEFT_CHUNKS = 8
BAND = (ATTN_LEFT_CHUNKS + 1) * CHUNK
REL_CLIP = 256
ATTN_SCALE = HEAD_DIM ** -0.5
NEG_INF = -1e30
RMS_EPS = 1e-6
RWKV_GN_EPS = HEAD_DIM * 1e-5
RWKV_LORA = 128
LRU_C = 8.0
LRU_CONV = 4
HGRN_SUB = 16

ATTN_COLS = 3 * GW
HGRN_COLS = 4 * GW
RWKV_COLS = 3 * GW + RWKV_LORA
LRU_COLS = 2 * GW

VMEM_LIMIT = 56 * 1024 * 1024
RWKV_PASSES = (1, (3, 3, 3, 1, 1), 1)
RWKV_MAP_BATCH = 8


def _cparams(*sem):
    return pltpu.CompilerParams(dimension_semantics=sem, vmem_limit_bytes=VMEM_LIMIT)


def _dot(a, b):
    return jnp.dot(a.astype(BF16), b.astype(BF16), preferred_element_type=F32)


def _dot_nt(a, b):
    return lax.dot_general(a.astype(BF16), b.astype(BF16), (((1,), (1,)), ((), ())),
                           preferred_element_type=F32)


def _dot_tn(a, b):
    return lax.dot_general(a.astype(BF16), b.astype(BF16), (((0,), (0,)), ((), ())),
                           preferred_element_type=F32)


def _split(a):
    hi = a.astype(BF16)
    lo = (a - hi.astype(F32)).astype(BF16)
    return hi, lo


def _mm(a, b, kind="nn", passes=1):
    f = {"nn": _dot, "nt": _dot_nt, "tn": _dot_tn}[kind]
    if passes == 1:
        return f(a, b)
    ah, al = _split(a)
    bh, bl = _split(b)
    if kind == "tn":
        return f(ah, bh) + (f(ah, bl) + f(al, bh))
    m = a.shape[0]
    top = f(jnp.concatenate([ah, al], axis=0), bh)
    return top[0:m] + (f(ah, bl) + top[m:])


def _dot_lhs2(a, b):
    ah, al = _split(a)
    return _dot(ah, b) + _dot(al, b)


def _rms(x, gain):
    return x * lax.rsqrt(jnp.mean(x * x, axis=-1, keepdims=True) + RMS_EPS) * gain


def _sigmoid(x):
    return 1.0 / (1.0 + jnp.exp(-x))


def _softplus(x):
    return jnp.maximum(x, 0.0) + jnp.log1p(jnp.exp(-jnp.abs(x)))


def _log1p_exp_neg_abs(x):
    return jnp.log(1.0 + jnp.exp(-jnp.abs(x)))


def _expm1(z):
    u = jnp.exp(z)
    um1 = u - 1.0
    near = um1 * z / jnp.where(u == 1.0, 1.0, jnp.log(u))
    return jnp.where(jnp.abs(z) > 0.5, um1, jnp.where(u == 1.0, z, near))


def _head_block_mask(n):
    r = lax.broadcasted_iota(jnp.int32, (n, n), 0) // HEAD_DIM
    c = lax.broadcasted_iota(jnp.int32, (n, n), 1) // HEAD_DIM
    return r == c


def _cumsum_rows(x):
    n = x.shape[0]
    row = lax.broadcasted_iota(jnp.int32, (n, 1), 0)
    d = 1
    while d < n:
        x = x + jnp.where(row >= d, pltpu.roll(x, d, 0), 0.0)
        d *= 2
    return x


def _stack_heads(x):
    lane_head = lax.broadcasted_iota(jnp.int32, (1, GW), 1) // HEAD_DIM
    return jnp.concatenate([jnp.where(lane_head == h, x, 0.0) for h in range(N_HEADS)], axis=0)


def _unstack_heads(xbd, c):
    return xbd[0:c] + xbd[c:2 * c] + xbd[2 * c:3 * c] + xbd[3 * c:4 * c]


def _inproj_body(x_ref, g_ref, w_ref, oa_ref, ob_ref, oc_ref, od_ref):
    h = _rms(x_ref[...], g_ref[...]).astype(BF16)
    o0, o1, o2 = ATTN_COLS, ATTN_COLS + HGRN_COLS, ATTN_COLS + HGRN_COLS + RWKV_COLS
    oa_ref[...] = jnp.dot(h, w_ref[:, 0:o0], preferred_element_type=F32).astype(BF16)
    ob_ref[...] = jnp.dot(h, w_ref[:, o0:o1], preferred_element_type=F32)
    oc_ref[...] = jnp.dot(h, w_ref[:, o1:o2], preferred_element_type=F32)
    od_ref[...] = jnp.dot(h, w_ref[:, o2:], preferred_element_type=F32)


def _inproj(x2, gain, w, tm):
    t, d = x2.shape
    d_in = w.shape[1]
    return pl.pallas_call(
        _inproj_body,
        grid=(t // tm,),
        in_specs=[pl.BlockSpec((tm, d), lambda i: (i, 0)),
                  pl.BlockSpec((1, d), lambda i: (0, 0)),
                  pl.BlockSpec((d, d_in), lambda i: (0, 0))],
        out_specs=[pl.BlockSpec((tm, ATTN_COLS), lambda i: (i, 0)),
                   pl.BlockSpec((tm, HGRN_COLS), lambda i: (i, 0)),
                   pl.BlockSpec((tm, RWKV_COLS), lambda i: (i, 0)),
                   pl.BlockSpec((tm, LRU_COLS), lambda i: (i, 0))],
        out_shape=[jax.ShapeDtypeStruct((t, ATTN_COLS), BF16),
                   jax.ShapeDtypeStruct((t, HGRN_COLS), F32),
                   jax.ShapeDtypeStruct((t, RWKV_COLS), F32),
                   jax.ShapeDtypeStruct((t, LRU_COLS), F32)],
        compiler_params=_cparams("parallel"),
        name="inproj",
    )(x2, gain, w)


def _attn_body(q_ref, k_ref, v_ref, bias_ref, o_ref, *, chunks):
    i = pl.program_id(1)
    lane_head = lax.broadcasted_iota(jnp.int32, (1, GW), 1) // HEAD_DIM
    kpos = lax.broadcasted_iota(jnp.int32, (1, BAND), 1)

    def chunk(j, carry):
        c = i * chunks + j
        rows = pl.ds(pl.multiple_of(j * CHUNK, CHUNK), CHUNK)
        q = q_ref[0, rows, :]
        r0 = pl.multiple_of(c * CHUNK, CHUNK)
        kw = k_ref[0, pl.ds(r0, BAND), :]
        vw = v_ref[0, pl.ds(r0, BAND), :]
        qbd = jnp.concatenate([jnp.where(lane_head == h, q, jnp.zeros_like(q)) for h in range(N_HEADS)],
                              axis=0)
        s = _dot_nt(qbd, kw) * ATTN_SCALE + bias_ref[...]
        s = jnp.where(kpos >= (ATTN_LEFT_CHUNKS - c) * CHUNK, s, NEG_INF)
        p = jnp.exp(s - jnp.max(s, axis=-1, keepdims=True))
        p = p / jnp.sum(p, axis=-1, keepdims=True)
        obd = _dot(p, vw)
        o = jnp.zeros((CHUNK, GW), F32)
        for h in range(N_HEADS):
            o = o + jnp.where(lane_head == h, obd[h * CHUNK:(h + 1) * CHUNK, :], 0.0)
        o_ref[0, rows, :] = o.astype(BF16)
        return carry

    lax.fori_loop(0, chunks, chunk, 0, unroll=2)


def _attention(pa3, bias, rb):
    b, s, _ = pa3.shape
    pad = ATTN_LEFT_CHUNKS * CHUNK
    kp = jnp.pad(pa3[:, :, GW:2 * GW], ((0, 0), (pad, 0), (0, 0)))
    vp = jnp.pad(pa3[:, :, 2 * GW:3 * GW], ((0, 0), (pad, 0), (0, 0)))
    return pl.pallas_call(
        functools.partial(_attn_body, chunks=rb // CHUNK),
        grid=(b, s // rb),
        in_specs=[pl.BlockSpec((1, rb, GW), lambda bi, i: (bi, i, 0)),
                  pl.BlockSpec((1, s + pad, GW), lambda bi, i: (bi, 0, 0)),
                  pl.BlockSpec((1, s + pad, GW), lambda bi, i: (bi, 0, 0)),
                  pl.BlockSpec((N_HEADS * CHUNK, BAND), lambda bi, i: (0, 0))],
        out_specs=pl.BlockSpec((1, rb, GW), lambda bi, i: (bi, i, 0)),
        out_shape=jax.ShapeDtypeStruct((b, s, GW), BF16),
        compiler_params=_cparams("parallel", "arbitrary"),
        name="attn",
    )(pa3, kp, vp, bias)


def _attn_bias_table(rel_bias):
    rel = np.arange(BAND + CHUNK - 1) - (CHUNK - 1) - ATTN_LEFT_CHUNKS * CHUNK
    ext = rel_bias.astype(F32)[:, np.clip(rel, -REL_CLIP, REL_CLIP) + REL_CLIP]
    tab = jnp.stack([ext[:, CHUNK - 1 - q:CHUNK - 1 - q + BAND] for q in range(CHUNK)], axis=1)
    return tab.reshape(N_HEADS * CHUNK, BAND)


def _hgrn_body(q_ref, f_ref, i_ref, g_ref, par_ref, o_ref, st_ref, oacc_ref, q_s, key_s, cum16_s, cum64_s,
               *, rb):
    @pl.when(pl.program_id(1) == 0)
    def _():
        st_ref[...] = jnp.zeros_like(st_ref)

    m = HGRN_SUB
    log_lb = par_ref[0:1, :]
    log_1m_lb = par_ref[1:2, :]
    one_m_lb = par_ref[2:3, :]
    gain = par_ref[3:4, :]
    bd = _head_block_mask(GW)
    ones_bd = jnp.where(bd, 1.0, 0.0).astype(BF16)
    trow = lax.broadcasted_iota(jnp.int32, (m, 1), 0)

    fr = f_ref[0]
    log_sig = jnp.minimum(fr, 0.0) - _log1p_exp_neg_abs(fr)
    bt = log_1m_lb + log_sig
    log_f = jnp.maximum(log_lb, bt) + _log1p_exp_neg_abs(log_lb - bt)
    row = lax.broadcasted_iota(jnp.int32, (rb, 1), 0) % CHUNK
    cum, d = log_f, 1
    while d < CHUNK:
        cum = cum + jnp.where(row >= d, pltpu.roll(cum, d, 0), 0.0)
        d *= 2
    cum64_s[...] = cum
    before = jnp.where(row == 0, 0.0, pltpu.roll(cum, 1, 0)).reshape(rb // m, m, GW)[:, 0:1, :]
    cum16_s[...] = cum - jnp.broadcast_to(before, (rb // m, m, GW)).reshape(rb, GW)
    key_s[...] = one_m_lb * _sigmoid(-fr)
    qr = q_ref[0]
    q_s[...] = qr * _sigmoid(qr)

    nsub = CHUNK // m

    def chunk(j, carry):
        rows = pl.ds(pl.multiple_of(j * CHUNK, CHUNK), CHUNK)
        q, key, v = q_s[rows, :], key_s[rows, :], i_ref[0, rows, :]
        c16, c64 = cum16_s[rows, :], cum64_s[rows, :]
        sub = lambda x, i: x[i * m:(i + 1) * m, :]
        st = st_ref[...]
        o = _dot_nt(q * jnp.exp(c64), st)
        a_exp = []
        for i in range(nsub):
            qi, ki, ci = sub(q, i), sub(key, i), sub(c16, i)
            ps = []
            for s in range(m):
                dec = jnp.exp(jnp.minimum(ci - ci[s:s + 1, :], 0.0))
                ps.append(jnp.where(trow >= s, qi * dec * ki[s:s + 1, :], 0.0))
            a_exp.append(_dot(jnp.concatenate(ps, axis=0), ones_bd))
        o_sub = []
        for i in range(nsub):
            vi, oi = sub(v, i), sub(o, i)
            for s in range(m):
                oi = oi + a_exp[i][s * m:(s + 1) * m, :] * vi[s:s + 1, :]
            o_sub.append(oi)
        q_hat = q * jnp.exp(c16)
        k_hat = [sub(key, i) * jnp.exp(sub(c16, i)[m - 1:m, :] - sub(c16, i)) for i in range(nsub)]
        b_start = [sub(c64, i)[0:1, :] - sub(c16, i)[0:1, :] for i in range(nsub)]
        b_end = [sub(c64, i)[m - 1:m, :] for i in range(nsub)]
        v_bd = [_stack_heads(sub(v, i)) for i in range(nsub)]
        for i in range(1, nsub):
            k_bd = jnp.concatenate([_stack_heads(k_hat[jj] * jnp.exp(b_start[i] - b_end[jj])) for jj in range(i)],
                                   axis=0)
            a_cat = _dot_nt(sub(q_hat, i), k_bd)
            o_sub[i] = o_sub[i] + _dot(a_cat, jnp.concatenate(v_bd[0:i], axis=0))
        oacc_ref[rows, :] = jnp.concatenate(o_sub, axis=0)
        last = c64[CHUNK - 1:CHUNK, :]
        st_ref[...] = st * jnp.exp(last) + jnp.where(bd, _dot_tn(v, key * jnp.exp(last - c64)), 0.0)
        return carry

    lax.fori_loop(0, rb // CHUNK, chunk, 0)
    o = oacc_ref[...]
    ms = _dot_lhs2(o * o, ones_bd) * (1.0 / HEAD_DIM)
    g = g_ref[0]
    o_ref[0] = (o * lax.rsqrt(ms + RMS_EPS) * gain * (g * _sigmoid(g))).astype(BF16)


def _hgrn(pb3, par, rb):
    b, s, _ = pb3.shape
    col = lambda n: pl.BlockSpec((1, rb, GW), lambda bi, i, n=n: (bi, i, n))
    return pl.pallas_call(
        functools.partial(_hgrn_body, rb=rb),
        grid=(b, s // rb),
        in_specs=[col(0), col(1), col(2), col(3),
                  pl.BlockSpec((8, GW), lambda bi, i: (0, 0))],
        out_specs=pl.BlockSpec((1, rb, GW), lambda bi, i: (bi, i, 0)),
        out_shape=jax.ShapeDtypeStruct((b, s, GW), BF16),
        scratch_shapes=[pltpu.VMEM((GW, GW), F32)] + [pltpu.VMEM((rb, GW), F32)] * 5,
        compiler_params=_cparams("parallel", "arbitrary"),
        name="hgrn",
    )(pb3, pb3, pb3, pb3, par)


def _expand_heads(x_cat, bd):
    return jnp.where(bd, jnp.concatenate([x_cat] * N_HEADS, axis=0), 0.0)


def _rwkv_chunk_maps(chunks, bd, strict_cat, incl_cat, eye_cat):
    c = CHUNK
    p_pair, p_inv, p_app = RWKV_PASSES
    pre = []
    for r, logw, k, v, alpha, beta in chunks:
        cum = _cumsum_rows(logw)
        e_last = jnp.exp(cum[c - 1:c, :])
        a_t = alpha * jnp.exp(cum - logw)
        r_t = r * jnp.exp(cum)
        e_inv = jnp.exp(-cum)
        b_t = beta * e_inv
        k_t = k * e_inv
        ar = jnp.concatenate([a_t, r_t], axis=0)
        bbd, kbd, vbd = _stack_heads(b_t), _stack_heads(k_t), _stack_heads(v)
        pair = _mm(ar, jnp.concatenate([bbd, kbd], axis=0), "nt", p_pair)
        l_ab = jnp.where(strict_cat, pair[0:c, 0:GW], 0.0)
        lm_k = jnp.where(jnp.concatenate([strict_cat, incl_cat], axis=0), pair[:, GW:], 0.0)
        m_rb = jnp.where(incl_cat, pair[c:, 0:GW], 0.0)
        wy0 = _mm(lm_k, vbd, "nn", p_app)
        pre.append((e_last, a_t, r_t, b_t, k_t, v, l_ab, m_rb, wy0))
    xs = [p[6] for p in pre]
    invs = [eye_cat + x for x in xs]
    xbds = [_expand_heads(x, bd) for x in xs]
    for p_step in p_inv:
        xs = [_mm(x, xbd, "nn", p_step) for x, xbd in zip(xs, xbds)]
        xbds = [_expand_heads(x, bd) for x in xs]
        invs = [inv + _mm(inv, xbd, "nn", p_step) for inv, xbd in zip(invs, xbds)]
    outs = []
    for (e_last, a_t, r_t, b_t, k_t, v, l_ab, m_rb, wy0), inv in zip(pre, invs):
        sol = _mm(inv, jnp.concatenate([_stack_heads(a_t), _stack_heads(wy0[0:c])], axis=1), "nn", p_app)
        a_s, u0 = sol[:, 0:GW], sol[:, GW:]
        out = _mm(m_rb, jnp.concatenate([_stack_heads(a_s), _stack_heads(u0)], axis=1), "nn", p_app)
        ry = r_t + out[:, 0:GW]
        y0 = wy0[c:] + out[:, GW:]
        bh, kh = b_t * e_last, k_t * e_last
        g = jnp.where(bd, _mm(a_s, bh, "tn", p_app), 0.0)
        c0 = jnp.where(bd, _mm(jnp.concatenate([u0, v], axis=0), jnp.concatenate([bh, kh], axis=0), "tn", p_app),
                       0.0)
        outs.append((ry, y0, g, c0, e_last))
    return outs


def _rwkv_body(pc_ref, par_ref, w2_ref, a2_ref, g2_ref, o_ref, ht_ref, prev_ref,
               r_s, w_s, k_s, v_s, al_s, be_s, ry_s, y_s, bon_s, gate_s, g_s, c_s, el_s, *, rb):
    @pl.when(pl.program_id(1) == 0)
    def _():
        ht_ref[...] = jnp.zeros_like(ht_ref)
        prev_ref[...] = jnp.zeros_like(prev_ref)

    c = CHUNK
    mu_main = [par_ref[n:n + 1, :] for n in range(3)]
    mu_lo = par_ref[3:4, 0:RWKV_LORA]
    w0, a0 = par_ref[4:5, :], par_ref[5:6, :]
    k_k, k_a, r_k = par_ref[6:7, :], par_ref[7:8, :], par_ref[8:9, :]
    ln_w, ln_b = par_ref[9:10, :], par_ref[10:11, :]
    bd = _head_block_mask(GW)
    ones_bd = jnp.where(bd, 1.0, 0.0).astype(BF16)

    pc = pc_ref[0]
    row0 = lax.broadcasted_iota(jnp.int32, (rb, 1), 0) == 0
    prev = jnp.where(row0, prev_ref[...], pltpu.roll(pc, 1, 0))
    prev_ref[...] = pc[rb - 1:rb, :]
    xs = [pc[:, n * GW:(n + 1) * GW] for n in range(3)]
    ps = [prev[:, n * GW:(n + 1) * GW] for n in range(3)]
    r, k, v = [x + mu * (p - x) for x, p, mu in zip(xs, ps, mu_main)]
    lo, plo = pc[:, 3 * GW:], prev[:, 3 * GW:]
    lo = lo + mu_lo * (plo - lo)
    w_in = -(w0 + _dot(jnp.tanh(lo), w2_ref[...]))
    w_pre = -(jnp.maximum(w_in, 0.0) + _log1p_exp_neg_abs(w_in)) - 0.5
    a = _sigmoid(a0 + _dot(lo, a2_ref[...]))
    g = _dot(_sigmoid(lo), g2_ref[...])
    kk = k * k_k
    kk = kk / jnp.maximum(jnp.sqrt(_dot_lhs2(kk * kk, ones_bd)), 1e-12)
    k = k * (1.0 + (a - 1.0) * k_a)
    r_s[...] = r
    w_s[...] = -jnp.exp(w_pre)
    k_s[...] = k
    v_s[...] = v
    al_s[...] = -kk
    be_s[...] = kk * a
    bon_s[...] = _dot_lhs2(r * k * r_k, ones_bd) * v
    gate_s[...] = g

    tt = lax.broadcasted_iota(jnp.int32, (c, GW), 0)
    ss = lax.broadcasted_iota(jnp.int32, (c, GW), 1) % c
    strict_cat, incl_cat = tt > ss, tt >= ss
    eye_cat = jnp.where(tt == ss, 1.0, 0.0)

    nb = RWKV_MAP_BATCH

    def chunk_maps(jj, carry):
        js = [jj * nb + n for n in range(nb)]
        rows = [pl.ds(pl.multiple_of(j * c, c), c) for j in js]
        maps = _rwkv_chunk_maps([(r_s[rw, :], w_s[rw, :], k_s[rw, :], v_s[rw, :], al_s[rw, :], be_s[rw, :])
                                 for rw in rows], bd, strict_cat, incl_cat, eye_cat)
        for j, rw, (ry, y0, g, c0, e_last) in zip(js, rows, maps):
            ry_s[rw, :] = ry
            y_s[rw, :] = y0
            g_s[j] = g.astype(BF16)
            c_s[j] = c0
            el_s[j] = jnp.broadcast_to(e_last, (8, GW))
        return carry

    lax.fori_loop(0, rb // (c * nb), chunk_maps, 0)

    def chunk_apply(j, carry):
        rows = pl.ds(pl.multiple_of(j * c, c), c)
        ht = ht_ref[...]
        htb = ht.astype(BF16)
        y_s[rows, :] = y_s[rows, :] + _dot_nt(ry_s[rows, :], htb)
        ht_ref[...] = ht * el_s[j][0:1, :] + jnp.dot(htb, g_s[j], preferred_element_type=F32) + c_s[j]
        return carry

    lax.fori_loop(0, rb // c, chunk_apply, 0, unroll=2)

    y = y_s[...]
    mean = _dot_lhs2(y, ones_bd) * (1.0 / HEAD_DIM)
    yc = y - mean
    var = _dot_lhs2(yc * yc, ones_bd) * (1.0 / HEAD_DIM)
    yn = yc * lax.rsqrt(var + RWKV_GN_EPS) * ln_w + ln_b
    o_ref[0] = ((yn + bon_s[...]) * gate_s[...]).astype(BF16)


def _rwkv(pc3, par, w2p, a2p, g2p, rb):
    b, s, _ = pc3.shape
    full = lambda shp: pl.BlockSpec(shp, lambda bi, i: (0,) * len(shp))
    return pl.pallas_call(
        functools.partial(_rwkv_body, rb=rb),
        grid=(b, s // rb),
        in_specs=[pl.BlockSpec((1, rb, RWKV_COLS), lambda bi, i: (bi, i, 0)),
                  full((16, GW)), full((RWKV_LORA, GW)), full((RWKV_LORA, GW)), full((RWKV_LORA, GW))],
        out_specs=pl.BlockSpec((1, rb, GW), lambda bi, i: (bi, i, 0)),
        out_shape=jax.ShapeDtypeStruct((b, s, GW), BF16),
        scratch_shapes=[pltpu.VMEM((GW, GW), F32), pltpu.VMEM((1, RWKV_COLS), F32)]
        + [pltpu.VMEM((rb, GW), F32)] * 10
        + [pltpu.VMEM((rb // CHUNK, GW, GW), BF16), pltpu.VMEM((rb // CHUNK, GW, GW), F32),
           pltpu.VMEM((rb // CHUNK, 8, GW), F32)],
        compiler_params=_cparams("parallel", "arbitrary"),
        name="rwkv",
    )(pc3, par, w2p, a2p, g2p)


def _lru_body(x_ref, g_ref, par_ref, wa_ref, wx_ref, o_ref, ext_ref, h_ref, *, rb):
    @pl.when(pl.program_id(1) == 0)
    def _():
        ext_ref[0:8, :] = jnp.zeros((8, GW), F32)
        h_ref[...] = jnp.zeros_like(h_ref)

    conv_b, ba, bx, lam = par_ref[4:5, :], par_ref[5:6, :], par_ref[6:7, :], par_ref[7:8, :]
    x = x_ref[0]
    ext_ref[8:8 + rb, :] = x
    conv = conv_b + par_ref[0:1, :] * ext_ref[pl.ds(8 - 3, rb), :]
    for j in range(1, LRU_CONV):
        conv = conv + par_ref[j:j + 1, :] * ext_ref[pl.ds(8 - 3 + j, rb), :]
    ext_ref[0:8, :] = x[rb - 8:rb, :]
    gate_r = _sigmoid(_dot(conv, wa_ref[...]) + ba)
    gate_i = _sigmoid(_dot(conv, wx_ref[...]) + bx)
    log_a = -LRU_C * gate_r * _softplus(-lam)
    a = jnp.exp(log_a)
    inp = jnp.sqrt(-_expm1(2.0 * log_a)) * (gate_i * conv)
    row = lax.broadcasted_iota(jnp.int32, (rb, 1), 0)
    d = 1
    while d < rb:
        m = row >= d
        inp = jnp.where(m, a * pltpu.roll(inp, d, 0) + inp, inp)
        a = jnp.where(m, a * pltpu.roll(a, d, 0), a)
        d *= 2
    h = inp + a * h_ref[...]
    h_ref[...] = h[rb - 1:rb, :]
    g = g_ref[0]
    gelu = 0.5 * g * (1.0 + jnp.tanh(math.sqrt(2.0 / math.pi) * (g + 0.044715 * (g * g * g))))
    o_ref[0] = (h * gelu).astype(BF16)


def _lru(pd3, par, wa_bd, wx_bd, rb):
    b, s, _ = pd3.shape
    full = lambda shp: pl.BlockSpec(shp, lambda bi, i: (0,) * len(shp))
    return pl.pallas_call(
        functools.partial(_lru_body, rb=rb),
        grid=(b, s // rb),
        in_specs=[pl.BlockSpec((1, rb, GW), lambda bi, i: (bi, i, 0)),
                  pl.BlockSpec((1, rb, GW), lambda bi, i: (bi, i, 1)),
                  full((8, GW)), full((GW, GW)), full((GW, GW))],
        out_specs=pl.BlockSpec((1, rb, GW), lambda bi, i: (bi, i, 0)),
        out_shape=jax.ShapeDtypeStruct((b, s, GW), BF16),
        scratch_shapes=[pltpu.VMEM((rb + 8, GW), F32), pltpu.VMEM((1, GW), F32)],
        compiler_params=_cparams("parallel", "arbitrary"),
        name="lru",
    )(pd3, pd3, par, wa_bd, wx_bd)


def _outproj_body(ya_ref, yb_ref, yc_ref, yd_ref, x_ref, w_ref, g_ref, o_ref):
    acc = jnp.dot(ya_ref[...], w_ref[0:GW, :], preferred_element_type=F32)
    acc = acc + jnp.dot(yb_ref[...], w_ref[GW:2 * GW, :], preferred_element_type=F32)
    acc = acc + jnp.dot(yc_ref[...], w_ref[2 * GW:3 * GW, :], preferred_element_type=F32)
    acc = acc + jnp.dot(yd_ref[...], w_ref[3 * GW:4 * GW, :], preferred_element_type=F32)
    o_ref[...] = x_ref[...] + _rms(acc, g_ref[...])


def _outproj(ya, yb, yc, yd, x2, w, gain, tm):
    t, d = x2.shape
    mix = pl.BlockSpec((tm, GW), lambda i: (i, 0))
    return pl.pallas_call(
        _outproj_body,
        grid=(t // tm,),
        in_specs=[mix, mix, mix, mix,
                  pl.BlockSpec((tm, d), lambda i: (i, 0)),
                  pl.BlockSpec((4 * GW, d), lambda i: (0, 0)),
                  pl.BlockSpec((1, d), lambda i: (0, 0))],
        out_specs=pl.BlockSpec((tm, d), lambda i: (i, 0)),
        out_shape=jax.ShapeDtypeStruct((t, d), F32),
        compiler_params=_cparams("parallel"),
        name="outproj",
    )(ya, yb, yc, yd, x2, w, gain)


def _mlp_body(x_ref, g1_ref, w1_ref, w2_ref, g2_ref, o_ref, h_ref, acc_ref):
    kk = pl.program_id(1)

    @pl.when(kk == 0)
    def _():
        h_ref[...] = _rms(x_ref[...], g1_ref[...]).astype(BF16)
        acc_ref[...] = jnp.zeros_like(acc_ref)

    a = jnp.dot(h_ref[...], w1_ref[...], preferred_element_type=F32)
    a = jnp.square(jnp.maximum(a, 0.0)).astype(BF16)
    acc_ref[...] += jnp.dot(a, w2_ref[...], preferred_element_type=F32)

    @pl.when(kk == pl.num_programs(1) - 1)
    def _():
        o_ref[...] = x_ref[...] + _rms(acc_ref[...], g2_ref[...])


def _mlp(x2, g1, w1, w2, g2, tm, tk):
    t, d = x2.shape
    hid = w1.shape[1]
    return pl.pallas_call(
        _mlp_body,
        grid=(t // tm, hid // tk),
        in_specs=[pl.BlockSpec((tm, d), lambda i, k: (i, 0)),
                  pl.BlockSpec((1, d), lambda i, k: (0, 0)),
                  pl.BlockSpec((d, tk), lambda i, k: (0, k)),
                  pl.BlockSpec((tk, d), lambda i, k: (k, 0)),
                  pl.BlockSpec((1, d), lambda i, k: (0, 0))],
        out_specs=pl.BlockSpec((tm, d), lambda i, k: (i, 0)),
        out_shape=jax.ShapeDtypeStruct((t, d), F32),
        scratch_shapes=[pltpu.VMEM((tm, d), BF16), pltpu.VMEM((tm, d), F32)],
        compiler_params=_cparams("parallel", "arbitrary"),
        name="mlp",
    )(x2, g1, w1, w2, g2)


def _rows(vectors, n_rows):
    tab = jnp.stack([v.astype(F32) for v in vectors])
    return jnp.pad(tab, ((0, n_rows - tab.shape[0]), (0, 0)))


def _block_diag(w):
    out = jnp.zeros((GW, GW), w.dtype)
    for n in range(w.shape[0]):
        out = out.at[n * HEAD_DIM:(n + 1) * HEAD_DIM, n * HEAD_DIM:(n + 1) * HEAD_DIM].set(w[n])
    return out


def _pick_block(n, want):
    while n % want:
        want //= 2
    return want


def kernel(x, norm_mix_pre, norm_mix_post, norm_mlp_pre, norm_mlp_post, w_in, w_out, attn_rel_bias, hgrn_lb_logits, hgrn_norm, rwkv_mu, rwkv_w0, rwkv_w2, rwkv_a0, rwkv_a2, rwkv_g2, rwkv_k_k, rwkv_k_a, rwkv_r_k, rwkv_ln_w, rwkv_ln_b, lru_conv_w, lru_conv_b, lru_wa, lru_ba, lru_wx, lru_bx, lru_lambda, mlp_w1, mlp_w2):
    b, s, d = x.shape
    depth = w_in.shape[0]
    t = b * s
    tm = _pick_block(t, 512)
    rb = _pick_block(s, 512)
    tk = _pick_block(mlp_w1.shape[-1], 1024)

    lb_sm = jax.nn.softmax(hgrn_lb_logits.astype(F32), axis=0)
    lb_all = jnp.maximum(jnp.cumsum(lb_sm, axis=0) - lb_sm[0:1], 0.0)

    x2 = x.reshape(t, d)
    for l in range(depth):
        pa, pb, pc, pd = _inproj(x2, norm_mix_pre[l].reshape(1, d), w_in[l].astype(BF16), tm)

        ya = _attention(pa.reshape(b, s, ATTN_COLS), _attn_bias_table(attn_rel_bias[l]), rb)

        lb = lb_all[l]
        hgrn_par = _rows([jnp.log(lb), jnp.log1p(-lb), 1.0 - lb, hgrn_norm[l]], 8)
        yb = _hgrn(pb.reshape(b, s, HGRN_COLS), hgrn_par, rb)

        mu = rwkv_mu[l].astype(F32)
        mu_lo = jnp.pad(mu[3 * GW:], (0, GW - RWKV_LORA))
        rwkv_par = _rows([mu[0:GW], mu[GW:2 * GW], mu[2 * GW:3 * GW], mu_lo, rwkv_w0[l], rwkv_a0[l],
                          rwkv_k_k[l], rwkv_k_a[l], rwkv_r_k[l].reshape(GW), rwkv_ln_w[l], rwkv_ln_b[l]], 16)
        zeros = lambda n: jnp.zeros((n, GW), F32)
        w2p = jnp.concatenate([rwkv_w2[l].astype(F32), zeros(96)], axis=0).astype(BF16)
        a2p = jnp.concatenate([zeros(32), rwkv_a2[l].astype(F32), zeros(64)], axis=0).astype(BF16)
        g2p = jnp.concatenate([zeros(64), rwkv_g2[l].astype(F32)], axis=0).astype(BF16)
        yc = _rwkv(pc.reshape(b, s, RWKV_COLS), rwkv_par, w2p, a2p, g2p, rb)

        cw = lru_conv_w[l].astype(F32)
        lru_par = _rows([cw[0], cw[1], cw[2], cw[3], lru_conv_b[l], lru_ba[l], lru_bx[l], lru_lambda[l]], 8)
        yd = _lru(pd.reshape(b, s, LRU_COLS), lru_par, _block_diag(lru_wa[l]).astype(BF16),
                  _block_diag(lru_wx[l]).astype(BF16), rb)

        x2 = _outproj(ya.reshape(t, GW), yb.reshape(t, GW), yc.reshape(t, GW), yd.reshape(t, GW),
                      x2, w_out[l].astype(BF16), norm_mix_post[l].reshape(1, d), tm)
        x2 = _mlp(x2, norm_mlp_pre[l].reshape(1, d), mlp_w1[l].astype(BF16), mlp_w2[l].astype(BF16),
                  norm_mlp_post[l].reshape(1, d), tm, tk)
    return x2.reshape(b, s, d)
```

```python
import functools
import math

import numpy as np

import jax
import jax.numpy as jnp
from jax import lax
from jax.experimental import pallas as pl
from jax.experimental.pallas import tpu as pltpu

F32 = jnp.float32
BF16 = jnp.bfloat16

GW = 256
HEAD_DIM = 64
N_HEADS = 4
CHUNK = 64
ATTN_LEFT_CHUNKS = 8
BAND = (ATTN_LEFT_CHUNKS + 1) * CHUNK
REL_CLIP = 256
ATTN_SCALE = HEAD_DIM ** -0.5
NEG_INF = -1e30
RMS_EPS = 1e-6
RWKV_GN_EPS = HEAD_DIM * 1e-5
RWKV_LORA = 128
LRU_C = 8.0
LRU_CONV = 4
HGRN_SUB = 16

ATTN_COLS = 3 * GW
HGRN_COLS = 4 * GW
RWKV_COLS = 3 * GW + RWKV_LORA
LRU_COLS = 2 * GW

VMEM_LIMIT = 56 * 1024 * 1024
RWKV_PASSES = (1, (3, 3, 3, 1, 1), 1)
RWKV_MAP_BATCH = 8
ATTN_BATCH = 4


def _cparams(*sem):
    return pltpu.CompilerParams(dimension_semantics=sem, vmem_limit_bytes=VMEM_LIMIT)


def _dot(a, b):
    return jnp.dot(a.astype(BF16), b.astype(BF16), preferred_element_type=F32)


def _dot_nt(a, b):
    return lax.dot_general(a.astype(BF16), b.astype(BF16), (((1,), (1,)), ((), ())),
                           preferred_element_type=F32)


def _dot_tn(a, b):
    return lax.dot_general(a.astype(BF16), b.astype(BF16), (((0,), (0,)), ((), ())),
                           preferred_element_type=F32)


def _split(a):
    hi = a.astype(BF16)
    lo = (a - hi.astype(F32)).astype(BF16)
    return hi, lo


def _mm(a, b, kind="nn", passes=1):
    f = {"nn": _dot, "nt": _dot_nt, "tn": _dot_tn}[kind]
    if passes == 1:
        return f(a, b)
    ah, al = _split(a)
    bh, bl = _split(b)
    if kind == "tn":
        return f(ah, bh) + (f(ah, bl) + f(al, bh))
    m = a.shape[0]
    top = f(jnp.concatenate([ah, al], axis=0), bh)
    return top[0:m] + (f(ah, bl) + top[m:])


def _dot_lhs2(a, b):
    ah, al = _split(a)
    return _dot(ah, b) + _dot(al, b)


def _rms(x, gain):
    return x * lax.rsqrt(jnp.mean(x * x, axis=-1, keepdims=True) + RMS_EPS) * gain


def _sigmoid(x):
    return 1.0 / (1.0 + jnp.exp(-x))


def _softplus(x):
    return jnp.maximum(x, 0.0) + jnp.log1p(jnp.exp(-jnp.abs(x)))


def _log1p_exp_neg_abs(x):
    return jnp.log(1.0 + jnp.exp(-jnp.abs(x)))


def _expm1(z):
    u = jnp.exp(z)
    um1 = u - 1.0
    near = um1 * z / jnp.where(u == 1.0, 1.0, jnp.log(u))
    return jnp.where(jnp.abs(z) > 0.5, um1, jnp.where(u == 1.0, z, near))


def _head_block_mask(n):
    r = lax.broadcasted_iota(jnp.int32, (n, n), 0) // HEAD_DIM
    c = lax.broadcasted_iota(jnp.int32, (n, n), 1) // HEAD_DIM
    return r == c


def _cumsum_rows(x):
    n = x.shape[0]
    row = lax.broadcasted_iota(jnp.int32, (n, 1), 0)
    d = 1
    while d < n:
        x = x + jnp.where(row >= d, pltpu.roll(x, d, 0), 0.0)
        d *= 2
    return x


def _stack_heads(x):
    lane_head = lax.broadcasted_iota(jnp.int32, (1, GW), 1) // HEAD_DIM
    return jnp.concatenate([jnp.where(lane_head == h, x, 0.0) for h in range(N_HEADS)], axis=0)


def _unstack_heads(xbd, c):
    return xbd[0:c] + xbd[c:2 * c] + xbd[2 * c:3 * c] + xbd[3 * c:4 * c]


def _inproj_body(x_ref, g_ref, w_ref, oa_ref, ob_ref, oc_ref, od_ref):
    h = _rms(x_ref[...], g_ref[...]).astype(BF16)
    o0, o1, o2 = ATTN_COLS, ATTN_COLS + HGRN_COLS, ATTN_COLS + HGRN_COLS + RWKV_COLS
    oa_ref[...] = jnp.dot(h, w_ref[:, 0:o0], preferred_element_type=F32).astype(BF16)
    ob_ref[...] = jnp.dot(h, w_ref[:, o0:o1], preferred_element_type=F32)
    oc_ref[...] = jnp.dot(h, w_ref[:, o1:o2], preferred_element_type=F32)
    od_ref[...] = jnp.dot(h, w_ref[:, o2:], preferred_element_type=F32)


def _inproj(x2, gain, w, tm):
    t, d = x2.shape
    d_in = w.shape[1]
    return pl.pallas_call(
        _inproj_body,
        grid=(t // tm,),
        in_specs=[pl.BlockSpec((tm, d), lambda i: (i, 0)),
                  pl.BlockSpec((1, d), lambda i: (0, 0)),
                  pl.BlockSpec((d, d_in), lambda i: (0, 0))],
        out_specs=[pl.BlockSpec((tm, ATTN_COLS), lambda i: (i, 0)),
                   pl.BlockSpec((tm, HGRN_COLS), lambda i: (i, 0)),
                   pl.BlockSpec((tm, RWKV_COLS), lambda i: (i, 0)),
                   pl.BlockSpec((tm, LRU_COLS), lambda i: (i, 0))],
        out_shape=[jax.ShapeDtypeStruct((t, ATTN_COLS), BF16),
                   jax.ShapeDtypeStruct((t, HGRN_COLS), F32),
                   jax.ShapeDtypeStruct((t, RWKV_COLS), F32),
                   jax.ShapeDtypeStruct((t, LRU_COLS), F32)],
        compiler_params=_cparams("parallel"),
        name="inproj",
    )(x2, gain, w)


def _attn_body(q_ref, k_ref, v_ref, bias_ref, o_ref, *, chunks):
    i = pl.program_id(1)
    lane_head = lax.broadcasted_iota(jnp.int32, (1, GW), 1) // HEAD_DIM
    kpos = lax.broadcasted_iota(jnp.int32, (1, BAND), 1)

    nb = ATTN_BATCH

    def chunk_group(jj, carry):
        cs = [i * chunks + jj * nb + n for n in range(nb)]
        rows = [pl.ds(pl.multiple_of((jj * nb + n) * CHUNK, CHUNK), CHUNK) for n in range(nb)]
        wins = [pl.ds(pl.multiple_of(c * CHUNK, CHUNK), BAND) for c in cs]
        ss = []
        for c, rw, win in zip(cs, rows, wins):
            q = q_ref[0, rw, :] * ATTN_SCALE
            qbd = jnp.concatenate([jnp.where(lane_head == h, q, jnp.zeros_like(q)) for h in range(N_HEADS)],
                                  axis=0)
            s = _dot_nt(qbd, k_ref[0, win, :]) + bias_ref[...]
            ss.append(jnp.where(kpos >= (ATTN_LEFT_CHUNKS - c) * CHUNK, s, NEG_INF))
        ps = [jnp.exp(s - jnp.max(s, axis=-1, keepdims=True)) for s in ss]
        for rw, win, p in zip(rows, wins, ps):
            inv_l = 1.0 / jnp.sum(p, axis=-1, keepdims=True)
            obd = _dot(p, v_ref[0, win, :]) * inv_l
            o = jnp.zeros((CHUNK, GW), F32)
            for h in range(N_HEADS):
                o = o + jnp.where(lane_head == h, obd[h * CHUNK:(h + 1) * CHUNK, :], 0.0)
            o_ref[0, rw, :] = o.astype(BF16)
        return carry

    lax.fori_loop(0, chunks // nb, chunk_group, 0)


def _attention(pa3, bias, rb):
    b, s, _ = pa3.shape
    pad = ATTN_LEFT_CHUNKS * CHUNK
    kp = jnp.pad(pa3[:, :, GW:2 * GW], ((0, 0), (pad, 0), (0, 0)))
    vp = jnp.pad(pa3[:, :, 2 * GW:3 * GW], ((0, 0), (pad, 0), (0, 0)))
    return pl.pallas_call(
        functools.partial(_attn_body, chunks=rb // CHUNK),
        grid=(b, s // rb),
        in_specs=[pl.BlockSpec((1, rb, GW), lambda bi, i: (bi, i, 0)),
                  pl.BlockSpec((1, s + pad, GW), lambda bi, i: (bi, 0, 0)),
                  pl.BlockSpec((1, s + pad, GW), lambda bi, i: (bi, 0, 0)),
                  pl.BlockSpec((N_HEADS * CHUNK, BAND), lambda bi, i: (0, 0))],
        out_specs=pl.BlockSpec((1, rb, GW), lambda bi, i: (bi, i, 0)),
        out_shape=jax.ShapeDtypeStruct((b, s, GW), BF16),
        compiler_params=_cparams("parallel", "arbitrary"),
        name="attn",
    )(pa3, kp, vp, bias)


def _attn_bias_table(rel_bias):
    rel = np.arange(BAND + CHUNK - 1) - (CHUNK - 1) - ATTN_LEFT_CHUNKS * CHUNK
    ext = rel_bias.astype(F32)[:, np.clip(rel, -REL_CLIP, REL_CLIP) + REL_CLIP]
    tab = jnp.stack([ext[:, CHUNK - 1 - q:CHUNK - 1 - q + BAND] for q in range(CHUNK)], axis=1)
    return tab.reshape(N_HEADS * CHUNK, BAND)


def _hgrn_body(q_ref, f_ref, i_ref, g_ref, par_ref, o_ref, st_ref, oacc_ref, q_s, key_s, cum16_s, cum64_s,
               *, rb):
    @pl.when(pl.program_id(1) == 0)
    def _():
        st_ref[...] = jnp.zeros_like(st_ref)

    m = HGRN_SUB
    log_lb = par_ref[0:1, :]
    log_1m_lb = par_ref[1:2, :]
    one_m_lb = par_ref[2:3, :]
    gain = par_ref[3:4, :]
    bd = _head_block_mask(GW)
    ones_bd = jnp.where(bd, 1.0, 0.0).astype(BF16)
    trow = lax.broadcasted_iota(jnp.int32, (m, 1), 0)

    fr = f_ref[0]
    log_sig = jnp.minimum(fr, 0.0) - _log1p_exp_neg_abs(fr)
    bt = log_1m_lb + log_sig
    log_f = jnp.maximum(log_lb, bt) + _log1p_exp_neg_abs(log_lb - bt)
    row = lax.broadcasted_iota(jnp.int32, (rb, 1), 0) % CHUNK
    cum, d = log_f, 1
    while d < CHUNK:
        cum = cum + jnp.where(row >= d, pltpu.roll(cum, d, 0), 0.0)
        d *= 2
    cum64_s[...] = cum
    before = jnp.where(row == 0, 0.0, pltpu.roll(cum, 1, 0)).reshape(rb // m, m, GW)[:, 0:1, :]
    cum16_s[...] = cum - jnp.broadcast_to(before, (rb // m, m, GW)).reshape(rb, GW)
    key_s[...] = one_m_lb * _sigmoid(-fr)
    qr = q_ref[0]
    q_s[...] = qr * _sigmoid(qr)

    nsub = CHUNK // m

    def chunk(j, carry):
        rows = pl.ds(pl.multiple_of(j * CHUNK, CHUNK), CHUNK)
        q, key, v = q_s[rows, :], key_s[rows, :], i_ref[0, rows, :]
        c16, c64 = cum16_s[rows, :], cum64_s[rows, :]
        sub = lambda x, i: x[i * m:(i + 1) * m, :]
        st = st_ref[...]
        o = _dot_nt(q * jnp.exp(c64), st)
        a_exp = []
        for i in range(nsub):
            qi, ki, ci = sub(q, i), sub(key, i), sub(c16, i)
            ps = []
            for s in range(m):
                dec = jnp.exp(jnp.minimum(ci - ci[s:s + 1, :], 0.0))
                ps.append(jnp.where(trow >= s, qi * dec * ki[s:s + 1, :], 0.0))
            a_exp.append(_dot(jnp.concatenate(ps, axis=0), ones_bd))
        o_sub = []
        for i in range(nsub):
            vi, oi = sub(v, i), sub(o, i)
            for s in range(m):
                oi = oi + a_exp[i][s * m:(s + 1) * m, :] * vi[s:s + 1, :]
            o_sub.append(oi)
        q_hat = q * jnp.exp(c16)
        k_hat = [sub(key, i) * jnp.exp(sub(c16, i)[m - 1:m, :] - sub(c16, i)) for i in range(nsub)]
        b_start = [sub(c64, i)[0:1, :] - sub(c16, i)[0:1, :] for i in range(nsub)]
        b_end = [sub(c64, i)[m - 1:m, :] for i in range(nsub)]
        v_bd = [_stack_heads(sub(v, i)) for i in range(nsub)]
        for i in range(1, nsub):
            k_bd = jnp.concatenate([_stack_heads(k_hat[jj] * jnp.exp(b_start[i] - b_end[jj])) for jj in range(i)],
                                   axis=0)
            a_cat = _dot_nt(sub(q_hat, i), k_bd)
            o_sub[i] = o_sub[i] + _dot(a_cat, jnp.concatenate(v_bd[0:i], axis=0))
        oacc_ref[rows, :] = jnp.concatenate(o_sub, axis=0)
        last = c64[CHUNK - 1:CHUNK, :]
        st_ref[...] = st * jnp.exp(last) + jnp.where(bd, _dot_tn(v, key * jnp.exp(last - c64)), 0.0)
        return carry

    lax.fori_loop(0, rb // CHUNK, chunk, 0)
    o = oacc_ref[...]
    ms = _dot_lhs2(o * o, ones_bd) * (1.0 / HEAD_DIM)
    g = g_ref[0]
    o_ref[0] = (o * lax.rsqrt(ms + RMS_EPS) * gain * (g * _sigmoid(g))).astype(BF16)


def _hgrn(pb3, par, rb):
    b, s, _ = pb3.shape
    col = lambda n: pl.BlockSpec((1, rb, GW), lambda bi, i, n=n: (bi, i, n))
    return pl.pallas_call(
        functools.partial(_hgrn_body, rb=rb),
        grid=(b, s // rb),
        in_specs=[col(0), col(1), col(2), col(3),
                  pl.BlockSpec((8, GW), lambda bi, i: (0, 0))],
        out_specs=pl.BlockSpec((1, rb, GW), lambda bi, i: (bi, i, 0)),
        out_shape=jax.ShapeDtypeStruct((b, s, GW), BF16),
        scratch_shapes=[pltpu.VMEM((GW, GW), F32)] + [pltpu.VMEM((rb, GW), F32)] * 5,
        compiler_params=_cparams("parallel", "arbitrary"),
        name="hgrn",
    )(pb3, pb3, pb3, pb3, par)


def _expand_heads(x_cat, bd):
    return jnp.where(bd, jnp.concatenate([x_cat] * N_HEADS, axis=0), 0.0)


def _rwkv_chunk_maps(chunks, bd, strict_cat, incl_cat, eye_cat):
    c = CHUNK
    p_pair, p_inv, p_app = RWKV_PASSES
    pre = []
    for r, logw, k, v, alpha, beta in chunks:
        cum = _cumsum_rows(logw)
        e_last = jnp.exp(cum[c - 1:c, :])
        a_t = alpha * jnp.exp(cum - logw)
        r_t = r * jnp.exp(cum)
        e_inv = jnp.exp(-cum)
        b_t = beta * e_inv
        k_t = k * e_inv
        ar = jnp.concatenate([a_t, r_t], axis=0)
        bbd, kbd, vbd = _stack_heads(b_t), _stack_heads(k_t), _stack_heads(v)
        pair = _mm(ar, jnp.concatenate([bbd, kbd], axis=0), "nt", p_pair)
        l_ab = jnp.where(strict_cat, pair[0:c, 0:GW], 0.0)
        lm_k = jnp.where(jnp.concatenate([strict_cat, incl_cat], axis=0), pair[:, GW:], 0.0)
        m_rb = jnp.where(incl_cat, pair[c:, 0:GW], 0.0)
        wy0 = _mm(lm_k, vbd, "nn", p_app)
        pre.append((e_last, a_t, r_t, b_t, k_t, v, l_ab, m_rb, wy0))
    xs = [p[6] for p in pre]
    invs = [eye_cat + x for x in xs]
    xbds = [_expand_heads(x, bd) for x in xs]
    for p_step in p_inv:
        xs = [_mm(x, xbd, "nn", p_step) for x, xbd in zip(xs, xbds)]
        xbds = [_expand_heads(x, bd) for x in xs]
        invs = [inv + _mm(inv, xbd, "nn", p_step) for inv, xbd in zip(invs, xbds)]
    outs = []
    for (e_last, a_t, r_t, b_t, k_t, v, l_ab, m_rb, wy0), inv in zip(pre, invs):
        sol = _mm(inv, jnp.concatenate([_stack_heads(a_t), _stack_heads(wy0[0:c])], axis=1), "nn", p_app)
        a_s, u0 = sol[:, 0:GW], sol[:, GW:]
        out = _mm(m_rb, jnp.concatenate([_stack_heads(a_s), _stack_heads(u0)], axis=1), "nn", p_app)
        ry = r_t + out[:, 0:GW]
        y0 = wy0[c:] + out[:, GW:]
        bh, kh = b_t * e_last, k_t * e_last
        g = jnp.where(bd, _mm(a_s, bh, "tn", p_app), 0.0)
        c0 = jnp.where(bd, _mm(jnp.concatenate([u0, v], axis=0), jnp.concatenate([bh, kh], axis=0), "tn", p_app),
                       0.0)
        outs.append((ry, y0, g, c0, e_last))
    return outs


def _rwkv_body(pc_ref, par_ref, w2_ref, a2_ref, g2_ref, o_ref, ht_ref, prev_ref,
               r_s, w_s, k_s, v_s, al_s, be_s, ry_s, y_s, bon_s, gate_s, g_s, c_s, el_s, *, rb):
    @pl.when(pl.program_id(1) == 0)
    def _():
        ht_ref[...] = jnp.zeros_like(ht_ref)
        prev_ref[...] = jnp.zeros_like(prev_ref)

    c = CHUNK
    mu_main = [par_ref[n:n + 1, :] for n in range(3)]
    mu_lo = par_ref[3:4, 0:RWKV_LORA]
    w0, a0 = par_ref[4:5, :], par_ref[5:6, :]
    k_k, k_a, r_k = par_ref[6:7, :], par_ref[7:8, :], par_ref[8:9, :]
    ln_w, ln_b = par_ref[9:10, :], par_ref[10:11, :]
    bd = _head_block_mask(GW)
    ones_bd = jnp.where(bd, 1.0, 0.0).astype(BF16)

    pc = pc_ref[0]
    row0 = lax.broadcasted_iota(jnp.int32, (rb, 1), 0) == 0
    prev = jnp.where(row0, prev_ref[...], pltpu.roll(pc, 1, 0))
    prev_ref[...] = pc[rb - 1:rb, :]
    xs = [pc[:, n * GW:(n + 1) * GW] for n in range(3)]
    ps = [prev[:, n * GW:(n + 1) * GW] for n in range(3)]
    r, k, v = [x + mu * (p - x) for x, p, mu in zip(xs, ps, mu_main)]
    lo, plo = pc[:, 3 * GW:], prev[:, 3 * GW:]
    lo = lo + mu_lo * (plo - lo)
    w_in = -(w0 + _dot(jnp.tanh(lo), w2_ref[...]))
    w_pre = -(jnp.maximum(w_in, 0.0) + _log1p_exp_neg_abs(w_in)) - 0.5
    a = _sigmoid(a0 + _dot(lo, a2_ref[...]))
    g = _dot(_sigmoid(lo), g2_ref[...])
    kk = k * k_k
    kk = kk / jnp.maximum(jnp.sqrt(_dot_lhs2(kk * kk, ones_bd)), 1e-12)
    k = k * (1.0 + (a - 1.0) * k_a)
    r_s[...] = r
    w_s[...] = -jnp.exp(w_pre)
    k_s[...] = k
    v_s[...] = v
    al_s[...] = -kk
    be_s[...] = kk * a
    bon_s[...] = _dot_lhs2(r * k * r_k, ones_bd) * v
    gate_s[...] = g

    tt = lax.broadcasted_iota(jnp.int32, (c, GW), 0)
    ss = lax.broadcasted_iota(jnp.int32, (c, GW), 1) % c
    strict_cat, incl_cat = tt > ss, tt >= ss
    eye_cat = jnp.where(tt == ss, 1.0, 0.0)

    nb = RWKV_MAP_BATCH

    def chunk_maps(jj, carry):
        js = [jj * nb + n for n in range(nb)]
        rows = [pl.ds(pl.multiple_of(j * c, c), c) for j in js]
        maps = _rwkv_chunk_maps([(r_s[rw, :], w_s[rw, :], k_s[rw, :], v_s[rw, :], al_s[rw, :], be_s[rw, :])
                                 for rw in rows], bd, strict_cat, incl_cat, eye_cat)
        for j, rw, (ry, y0, g, c0, e_last) in zip(js, rows, maps):
            ry_s[rw, :] = ry
            y_s[rw, :] = y0
            g_s[j] = g.astype(BF16)
            c_s[j] = c0
            el_s[j] = jnp.broadcast_to(e_last, (8, GW))
        return carry

    lax.fori_loop(0, rb // (c * nb), chunk_maps, 0)

    def chunk_apply(j, carry):
        rows = pl.ds(pl.multiple_of(j * c, c), c)
        ht = ht_ref[...]
        htb = ht.astype(BF16)
        y_s[rows, :] = y_s[rows, :] + _dot_nt(ry_s[rows, :], htb)
        ht_ref[...] = ht * el_s[j][0:1, :] + jnp.dot(htb, g_s[j], preferred_element_type=F32) + c_s[j]
        return carry

    lax.fori_loop(0, rb // c, chunk_apply, 0, unroll=2)

    y = y_s[...]
    mean = _dot_lhs2(y, ones_bd) * (1.0 / HEAD_DIM)
    yc = y - mean
    var = _dot_lhs2(yc * yc, ones_bd) * (1.0 / HEAD_DIM)
    yn = yc * lax.rsqrt(var + RWKV_GN_EPS) * ln_w + ln_b
    o_ref[0] = ((yn + bon_s[...]) * gate_s[...]).astype(BF16)


def _rwkv(pc3, par, w2p, a2p, g2p, rb):
    b, s, _ = pc3.shape
    full = lambda shp: pl.BlockSpec(shp, lambda bi, i: (0,) * len(shp))
    return pl.pallas_call(
        functools.partial(_rwkv_body, rb=rb),
        grid=(b, s // rb),
        in_specs=[pl.BlockSpec((1, rb, RWKV_COLS), lambda bi, i: (bi, i, 0)),
                  full((16, GW)), full((RWKV_LORA, GW)), full((RWKV_LORA, GW)), full((RWKV_LORA, GW))],
        out_specs=pl.BlockSpec((1, rb, GW), lambda bi, i: (bi, i, 0)),
        out_shape=jax.ShapeDtypeStruct((b, s, GW), BF16),
        scratch_shapes=[pltpu.VMEM((GW, GW), F32), pltpu.VMEM((1, RWKV_COLS), F32)]
        + [pltpu.VMEM((rb, GW), F32)] * 10
        + [pltpu.VMEM((rb // CHUNK, GW, GW), BF16), pltpu.VMEM((rb // CHUNK, GW, GW), F32),
           pltpu.VMEM((rb // CHUNK, 8, GW), F32)],
        compiler_params=_cparams("parallel", "arbitrary"),
        name="rwkv",
    )(pc3, par, w2p, a2p, g2p)


def _lru_body(x_ref, g_ref, par_ref, wa_ref, wx_ref, o_ref, ext_ref, h_ref, *, rb):
    @pl.when(pl.program_id(1) == 0)
    def _():
        ext_ref[0:8, :] = jnp.zeros((8, GW), F32)
        h_ref[...] = jnp.zeros_like(h_ref)

    conv_b, ba, bx, lam = par_ref[4:5, :], par_ref[5:6, :], par_ref[6:7, :], par_ref[7:8, :]
    x = x_ref[0]
    ext_ref[8:8 + rb, :] = x
    conv = conv_b + par_ref[0:1, :] * ext_ref[pl.ds(8 - 3, rb), :]
    for j in range(1, LRU_CONV):
        conv = conv + par_ref[j:j + 1, :] * ext_ref[pl.ds(8 - 3 + j, rb), :]
    ext_ref[0:8, :] = x[rb - 8:rb, :]
    gate_r = _sigmoid(_dot(conv, wa_ref[...]) + ba)
    gate_i = _sigmoid(_dot(conv, wx_ref[...]) + bx)
    log_a = -LRU_C * gate_r * _softplus(-lam)
    a = jnp.exp(log_a)
    inp = jnp.sqrt(-_expm1(2.0 * log_a)) * (gate_i * conv)
    row = lax.broadcasted_iota(jnp.int32, (rb, 1), 0)
    d = 1
    while d < rb:
        m = row >= d
        inp = jnp.where(m, a * pltpu.roll(inp, d, 0) + inp, inp)
        a = jnp.where(m, a * pltpu.roll(a, d, 0), a)
        d *= 2
    h = inp + a * h_ref[...]
    h_ref[...] = h[rb - 1:rb, :]
    g = g_ref[0]
    gelu = 0.5 * g * (1.0 + jnp.tanh(math.sqrt(2.0 / math.pi) * (g + 0.044715 * (g * g * g))))
    o_ref[0] = (h * gelu).astype(BF16)


def _lru(pd3, par, wa_bd, wx_bd, rb):
    b, s, _ = pd3.shape
    full = lambda shp: pl.BlockSpec(shp, lambda bi, i: (0,) * len(shp))
    return pl.pallas_call(
        functools.partial(_lru_body, rb=rb),
        grid=(b, s // rb),
        in_specs=[pl.BlockSpec((1, rb, GW), lambda bi, i: (bi, i, 0)),
                  pl.BlockSpec((1, rb, GW), lambda bi, i: (bi, i, 1)),
                  full((8, GW)), full((GW, GW)), full((GW, GW))],
        out_specs=pl.BlockSpec((1, rb, GW), lambda bi, i: (bi, i, 0)),
        out_shape=jax.ShapeDtypeStruct((b, s, GW), BF16),
        scratch_shapes=[pltpu.VMEM((rb + 8, GW), F32), pltpu.VMEM((1, GW), F32)],
        compiler_params=_cparams("parallel", "arbitrary"),
        name="lru",
    )(pd3, pd3, par, wa_bd, wx_bd)


def _outproj_body(ya_ref, yb_ref, yc_ref, yd_ref, x_ref, w_ref, g_ref, o_ref):
    acc = jnp.dot(ya_ref[...], w_ref[0:GW, :], preferred_element_type=F32)
    acc = acc + jnp.dot(yb_ref[...], w_ref[GW:2 * GW, :], preferred_element_type=F32)
    acc = acc + jnp.dot(yc_ref[...], w_ref[2 * GW:3 * GW, :], preferred_element_type=F32)
    acc = acc + jnp.dot(yd_ref[...], w_ref[3 * GW:4 * GW, :], preferred_element_type=F32)
    o_ref[...] = x_ref[...] + _rms(acc, g_ref[...])


def _outproj(ya, yb, yc, yd, x2, w, gain, tm):
    t, d = x2.shape
    mix = pl.BlockSpec((tm, GW), lambda i: (i, 0))
    return pl.pallas_call(
        _outproj_body,
        grid=(t // tm,),
        in_specs=[mix, mix, mix, mix,
                  pl.BlockSpec((tm, d), lambda i: (i, 0)),
                  pl.BlockSpec((4 * GW, d), lambda i: (0, 0)),
                  pl.BlockSpec((1, d), lambda i: (0, 0))],
        out_specs=pl.BlockSpec((tm, d), lambda i: (i, 0)),
        out_shape=jax.ShapeDtypeStruct((t, d), F32),
        compiler_params=_cparams("parallel"),
        name="outproj",
    )(ya, yb, yc, yd, x2, w, gain)


def _mlp_body(x_ref, g1_ref, w1_ref, w2_ref, g2_ref, o_ref, h_ref, acc_ref):
    kk = pl.program_id(1)

    @pl.when(kk == 0)
    def _():
        h_ref[...] = _rms(x_ref[...], g1_ref[...]).astype(BF16)
        acc_ref[...] = jnp.zeros_like(acc_ref)

    a = jnp.dot(h_ref[...], w1_ref[...], preferred_element_type=F32)
    a = jnp.square(jnp.maximum(a, 0.0)).astype(BF16)
    acc_ref[...] += jnp.dot(a, w2_ref[...], preferred_element_type=F32)

    @pl.when(kk == pl.num_programs(1) - 1)
    def _():
        o_ref[...] = x_ref[...] + _rms(acc_ref[...], g2_ref[...])


def _mlp(x2, g1, w1, w2, g2, tm, tk):
    t, d = x2.shape
    hid = w1.shape[1]
    return pl.pallas_call(
        _mlp_body,
        grid=(t // tm, hid // tk),
        in_specs=[pl.BlockSpec((tm, d), lambda i, k: (i, 0)),
                  pl.BlockSpec((1, d), lambda i, k: (0, 0)),
                  pl.BlockSpec((d, tk), lambda i, k: (0, k)),
                  pl.BlockSpec((tk, d), lambda i, k: (k, 0)),
                  pl.BlockSpec((1, d), lambda i, k: (0, 0))],
        out_specs=pl.BlockSpec((tm, d), lambda i, k: (i, 0)),
        out_shape=jax.ShapeDtypeStruct((t, d), F32),
        scratch_shapes=[pltpu.VMEM((tm, d), BF16), pltpu.VMEM((tm, d), F32)],
        compiler_params=_cparams("parallel", "arbitrary"),
        name="mlp",
    )(x2, g1, w1, w2, g2)


def _rows(vectors, n_rows):
    tab = jnp.stack([v.astype(F32) for v in vectors])
    return jnp.pad(tab, ((0, n_rows - tab.shape[0]), (0, 0)))


def _block_diag(w):
    out = jnp.zeros((GW, GW), w.dtype)
    for n in range(w.shape[0]):
        out = out.at[n * HEAD_DIM:(n + 1) * HEAD_DIM, n * HEAD_DIM:(n + 1) * HEAD_DIM].set(w[n])
    return out


def _pick_block(n, want):
    while n % want:
        want //= 2
    return want


def kernel(x, norm_mix_pre, norm_mix_post, norm_mlp_pre, norm_mlp_post, w_in, w_out, attn_rel_bias, hgrn_lb_logits, hgrn_norm, rwkv_mu, rwkv_w0, rwkv_w2, rwkv_a0, rwkv_a2, rwkv_g2, rwkv_k_k, rwkv_k_a, rwkv_r_k, rwkv_ln_w, rwkv_ln_b, lru_conv_w, lru_conv_b, lru_wa, lru_ba, lru_wx, lru_bx, lru_lambda, mlp_w1, mlp_w2):
    b, s, d = x.shape
    depth = w_in.shape[0]
    t = b * s
    tm = _pick_block(t, 512)
    tm_mlp = _pick_block(t, 1024)
    rb = _pick_block(s, 512)
    tk = _pick_block(mlp_w1.shape[-1], 1024)

    lb_sm = jax.nn.softmax(hgrn_lb_logits.astype(F32), axis=0)
    lb_all = jnp.maximum(jnp.cumsum(lb_sm, axis=0) - lb_sm[0:1], 0.0)

    x2 = x.reshape(t, d)
    for l in range(depth):
        pa, pb, pc, pd = _inproj(x2, norm_mix_pre[l].reshape(1, d), w_in[l].astype(BF16), tm)

        ya = _attention(pa.reshape(b, s, ATTN_COLS), _attn_bias_table(attn_rel_bias[l]), rb)

        lb = lb_all[l]
        hgrn_par = _rows([jnp.log(lb), jnp.log1p(-lb), 1.0 - lb, hgrn_norm[l]], 8)
        yb = _hgrn(pb.reshape(b, s, HGRN_COLS), hgrn_par, rb)

        mu = rwkv_mu[l].astype(F32)
        mu_lo = jnp.pad(mu[3 * GW:], (0, GW - RWKV_LORA))
        rwkv_par = _rows([mu[0:GW], mu[GW:2 * GW], mu[2 * GW:3 * GW], mu_lo, rwkv_w0[l], rwkv_a0[l],
                          rwkv_k_k[l], rwkv_k_a[l], rwkv_r_k[l].reshape(GW), rwkv_ln_w[l], rwkv_ln_b[l]], 16)
        zeros = lambda n: jnp.zeros((n, GW), F32)
        w2p = jnp.concatenate([rwkv_w2[l].astype(F32), zeros(96)], axis=0).astype(BF16)
        a2p = jnp.concatenate([zeros(32), rwkv_a2[l].astype(F32), zeros(64)], axis=0).astype(BF16)
        g2p = jnp.concatenate([zeros(64), rwkv_g2[l].astype(F32)], axis=0).astype(BF16)
        yc = _rwkv(pc.reshape(b, s, RWKV_COLS), rwkv_par, w2p, a2p, g2p, rb)

        cw = lru_conv_w[l].astype(F32)
        lru_par = _rows([cw[0], cw[1], cw[2], cw[3], lru_conv_b[l], lru_ba[l], lru_bx[l], lru_lambda[l]], 8)
        yd = _lru(pd.reshape(b, s, LRU_COLS), lru_par, _block_diag(lru_wa[l]).astype(BF16),
                  _block_diag(lru_wx[l]).astype(BF16), rb)

        x2 = _outproj(ya.reshape(t, GW), yb.reshape(t, GW), yc.reshape(t, GW), yd.reshape(t, GW),
                      x2, w_out[l].astype(BF16), norm_mix_post[l].reshape(1, d), tm)
        x2 = _mlp(x2, norm_mlp_pre[l].reshape(1, d), mlp_w1[l].astype(BF16), mlp_w2[l].astype(BF16),
                  norm_mlp_post[l].reshape(1, d), tm_mlp, tk)
    return x2.reshape(b, s, d)
```

```python
import functools
import math

import numpy as np

import jax
import jax.numpy as jnp
from jax import lax
from jax.experimental import pallas as pl
from jax.experimental.pallas import tpu as pltpu

F32 = jnp.float32
BF16 = jnp.bfloat16

GW = 256
HEAD_DIM = 64
N_HEADS = 4
CHUNK = 64
ATTN_LEFT_CHUNKS = 8
BAND = (ATTN_LEFT_CHUNKS + 1) * CHUNK
ATTN_PAD = ATTN_LEFT_CHUNKS * CHUNK
REL_CLIP = 256
ATTN_SCALE = HEAD_DIM ** -0.5
NEG_INF = -1e30
RMS_EPS = 1e-6
RWKV_GN_EPS = HEAD_DIM * 1e-5
RWKV_LORA = 128
LRU_C = 8.0
LRU_CONV = 4
HGRN_SUB = 16

ATTN_COLS = 3 * GW
HGRN_COLS = 4 * GW
RWKV_COLS = 3 * GW + RWKV_LORA
LRU_COLS = 2 * GW

VMEM_LIMIT = 56 * 1024 * 1024
RWKV_PASSES = (1, (3, 3, 3, 1, 1), 1)
RWKV_MAP_BATCH = 8
ATTN_BATCH = 4


def _cparams(*sem):
    return pltpu.CompilerParams(dimension_semantics=sem, vmem_limit_bytes=VMEM_LIMIT)


def _dot(a, b):
    return jnp.dot(a.astype(BF16), b.astype(BF16), preferred_element_type=F32)


def _dot_nt(a, b):
    return lax.dot_general(a.astype(BF16), b.astype(BF16), (((1,), (1,)), ((), ())),
                           preferred_element_type=F32)


def _dot_tn(a, b):
    return lax.dot_general(a.astype(BF16), b.astype(BF16), (((0,), (0,)), ((), ())),
                           preferred_element_type=F32)


def _split(a):
    hi = a.astype(BF16)
    lo = (a - hi.astype(F32)).astype(BF16)
    return hi, lo


def _mm(a, b, kind="nn", passes=1):
    f = {"nn": _dot, "nt": _dot_nt, "tn": _dot_tn}[kind]
    if passes == 1:
        return f(a, b)
    ah, al = _split(a)
    bh, bl = _split(b)
    if kind == "tn":
        return f(ah, bh) + (f(ah, bl) + f(al, bh))
    m = a.shape[0]
    top = f(jnp.concatenate([ah, al], axis=0), bh)
    return top[0:m] + (f(ah, bl) + top[m:])


def _dot_lhs2(a, b):
    ah, al = _split(a)
    return _dot(ah, b) + _dot(al, b)


def _rms(x, gain):
    return x * lax.rsqrt(jnp.mean(x * x, axis=-1, keepdims=True) + RMS_EPS) * gain


def _sigmoid(x):
    return 1.0 / (1.0 + jnp.exp(-x))


def _softplus(x):
    return jnp.maximum(x, 0.0) + jnp.log1p(jnp.exp(-jnp.abs(x)))


def _log1p_exp_neg_abs(x):
    return jnp.log(1.0 + jnp.exp(-jnp.abs(x)))


def _expm1(z):
    u = jnp.exp(z)
    um1 = u - 1.0
    near = um1 * z / jnp.where(u == 1.0, 1.0, jnp.log(u))
    return jnp.where(jnp.abs(z) > 0.5, um1, jnp.where(u == 1.0, z, near))


def _head_block_mask(n):
    r = lax.broadcasted_iota(jnp.int32, (n, n), 0) // HEAD_DIM
    c = lax.broadcasted_iota(jnp.int32, (n, n), 1) // HEAD_DIM
    return r == c


def _cumsum_rows(x):
    n = x.shape[0]
    row = lax.broadcasted_iota(jnp.int32, (n, 1), 0)
    d = 1
    while d < n:
        x = x + jnp.where(row >= d, pltpu.roll(x, d, 0), 0.0)
        d *= 2
    return x


def _stack_heads(x):
    lane_head = lax.broadcasted_iota(jnp.int32, (1, GW), 1) // HEAD_DIM
    return jnp.concatenate([jnp.where(lane_head == h, x, 0.0) for h in range(N_HEADS)], axis=0)


def _unstack_heads(xbd, c):
    return xbd[0:c] + xbd[c:2 * c] + xbd[2 * c:3 * c] + xbd[3 * c:4 * c]


def _inproj_body(x_ref, g_ref, w_ref, oa_ref, ob_ref, oc_ref, od_ref):
    h = _rms(x_ref[...], g_ref[...]).astype(BF16)
    o0, o1, o2 = ATTN_COLS, ATTN_COLS + HGRN_COLS, ATTN_COLS + HGRN_COLS + RWKV_COLS
    oa_ref[...] = jnp.dot(h, w_ref[:, 0:o0], preferred_element_type=F32).astype(BF16)
    ob_ref[...] = jnp.dot(h, w_ref[:, o0:o1], preferred_element_type=F32)
    oc_ref[...] = jnp.dot(h, w_ref[:, o1:o2], preferred_element_type=F32)
    od_ref[...] = jnp.dot(h, w_ref[:, o2:], preferred_element_type=F32)


def _inproj(x2, gain, w, tm):
    t, d = x2.shape
    d_in = w.shape[1]
    return pl.pallas_call(
        _inproj_body,
        grid=(t // tm,),
        in_specs=[pl.BlockSpec((tm, d), lambda i: (i, 0)),
                  pl.BlockSpec((1, d), lambda i: (0, 0)),
                  pl.BlockSpec((d, d_in), lambda i: (0, 0))],
        out_specs=[pl.BlockSpec((tm, ATTN_COLS), lambda i: (i, 0)),
                   pl.BlockSpec((tm, HGRN_COLS), lambda i: (i, 0)),
                   pl.BlockSpec((tm, RWKV_COLS), lambda i: (i, 0)),
                   pl.BlockSpec((tm, LRU_COLS), lambda i: (i, 0))],
        out_shape=[jax.ShapeDtypeStruct((t, ATTN_COLS), BF16),
                   jax.ShapeDtypeStruct((t, HGRN_COLS), F32),
                   jax.ShapeDtypeStruct((t, RWKV_COLS), F32),
                   jax.ShapeDtypeStruct((t, LRU_COLS), F32)],
        compiler_params=_cparams("parallel"),
        name="inproj",
    )(x2, gain, w)


def _attn_body(q_ref, k_ref, v_ref, bias_ref, o_ref, kp_ref, vp_ref, *, chunks):
    i = pl.program_id(1)

    @pl.when(i == 0)
    def _():
        for src, dst in ((k_ref, kp_ref), (v_ref, vp_ref)):
            dst[0:ATTN_PAD, :] = jnp.zeros((ATTN_PAD, GW), BF16)
            dst[ATTN_PAD:, :] = src[0]

    lane_head = lax.broadcasted_iota(jnp.int32, (1, GW), 1) // HEAD_DIM
    kpos = lax.broadcasted_iota(jnp.int32, (1, BAND), 1)

    nb = ATTN_BATCH

    def chunk_group(jj, carry):
        cs = [i * chunks + jj * nb + n for n in range(nb)]
        rows = [pl.ds(pl.multiple_of((jj * nb + n) * CHUNK, CHUNK), CHUNK) for n in range(nb)]
        wins = [pl.ds(pl.multiple_of(c * CHUNK, CHUNK), BAND) for c in cs]
        ss = []
        for c, rw, win in zip(cs, rows, wins):
            q = q_ref[0, rw, :] * ATTN_SCALE
            qbd = jnp.concatenate([jnp.where(lane_head == h, q, jnp.zeros_like(q)) for h in range(N_HEADS)],
                                  axis=0)
            s = _dot_nt(qbd, kp_ref[win, :]) + bias_ref[...]
            ss.append(jnp.where(kpos >= (ATTN_LEFT_CHUNKS - c) * CHUNK, s, NEG_INF))
        ps = [jnp.exp(s - jnp.max(s, axis=-1, keepdims=True)) for s in ss]
        for rw, win, p in zip(rows, wins, ps):
            inv_l = 1.0 / jnp.sum(p, axis=-1, keepdims=True)
            obd = _dot(p, vp_ref[win, :]) * inv_l
            o = jnp.zeros((CHUNK, GW), F32)
            for h in range(N_HEADS):
                o = o + jnp.where(lane_head == h, obd[h * CHUNK:(h + 1) * CHUNK, :], 0.0)
            o_ref[0, rw, :] = o.astype(BF16)
        return carry

    lax.fori_loop(0, chunks // nb, chunk_group, 0)


def _attention(pa3, bias, rb):
    b, s, _ = pa3.shape
    return pl.pallas_call(
        functools.partial(_attn_body, chunks=rb // CHUNK),
        grid=(b, s // rb),
        in_specs=[pl.BlockSpec((1, rb, GW), lambda bi, i: (bi, i, 0)),
                  pl.BlockSpec((1, s, GW), lambda bi, i: (bi, 0, 1)),
                  pl.BlockSpec((1, s, GW), lambda bi, i: (bi, 0, 2)),
                  pl.BlockSpec((N_HEADS * CHUNK, BAND), lambda bi, i: (0, 0))],
        out_specs=pl.BlockSpec((1, rb, GW), lambda bi, i: (bi, i, 0)),
        out_shape=jax.ShapeDtypeStruct((b, s, GW), BF16),
        scratch_shapes=[pltpu.VMEM((s + ATTN_PAD, GW), BF16)] * 2,
        compiler_params=_cparams("parallel", "arbitrary"),
        name="attn",
    )(pa3, pa3, pa3, bias)


def _attn_bias_table(rel_bias):
    rel = np.arange(BAND + CHUNK - 1) - (CHUNK - 1) - ATTN_LEFT_CHUNKS * CHUNK
    ext = rel_bias.astype(F32)[:, np.clip(rel, -REL_CLIP, REL_CLIP) + REL_CLIP]
    tab = jnp.stack([ext[:, CHUNK - 1 - q:CHUNK - 1 - q + BAND] for q in range(CHUNK)], axis=1)
    return tab.reshape(N_HEADS * CHUNK, BAND)


def _hgrn_body(q_ref, f_ref, i_ref, g_ref, par_ref, o_ref, st_ref, oacc_ref, q_s, key_s, cum16_s, cum64_s,
               *, rb):
    @pl.when(pl.program_id(1) == 0)
    def _():
        st_ref[...] = jnp.zeros_like(st_ref)

    m = HGRN_SUB
    log_lb = par_ref[0:1, :]
    log_1m_lb = par_ref[1:2, :]
    one_m_lb = par_ref[2:3, :]
    gain = par_ref[3:4, :]
    bd = _head_block_mask(GW)
    ones_bd = jnp.where(bd, 1.0, 0.0).astype(BF16)
    trow = lax.broadcasted_iota(jnp.int32, (m, 1), 0)

    fr = f_ref[0]
    log_sig = jnp.minimum(fr, 0.0) - _log1p_exp_neg_abs(fr)
    bt = log_1m_lb + log_sig
    log_f = jnp.maximum(log_lb, bt) + _log1p_exp_neg_abs(log_lb - bt)
    row = lax.broadcasted_iota(jnp.int32, (rb, 1), 0) % CHUNK
    cum, d = log_f, 1
    while d < CHUNK:
        cum = cum + jnp.where(row >= d, pltpu.roll(cum, d, 0), 0.0)
        d *= 2
    cum64_s[...] = cum
    before = jnp.where(row == 0, 0.0, pltpu.roll(cum, 1, 0)).reshape(rb // m, m, GW)[:, 0:1, :]
    cum16_s[...] = cum - jnp.broadcast_to(before, (rb // m, m, GW)).reshape(rb, GW)
    key_s[...] = one_m_lb * _sigmoid(-fr)
    qr = q_ref[0]
    q_s[...] = qr * _sigmoid(qr)

    nsub = CHUNK // m

    def chunk(j, carry):
        rows = pl.ds(pl.multiple_of(j * CHUNK, CHUNK), CHUNK)
        q, key, v = q_s[rows, :], key_s[rows, :], i_ref[0, rows, :]
        c16, c64 = cum16_s[rows, :], cum64_s[rows, :]
        sub = lambda x, i: x[i * m:(i + 1) * m, :]
        st = st_ref[...]
        o = _dot_nt(q * jnp.exp(c64), st)
        a_exp = []
        for i in range(nsub):
            qi, ki, ci = sub(q, i), sub(key, i), sub(c16, i)
            ps = []
            for s in range(m):
                dec = jnp.exp(jnp.where(trow >= s, ci - ci[s:s + 1, :], NEG_INF))
                ps.append(qi * dec * ki[s:s + 1, :])
            a_exp.append(_dot(jnp.concatenate(ps, axis=0), ones_bd))
        o_sub = []
        for i in range(nsub):
            vi, oi = sub(v, i), sub(o, i)
            for s in range(m):
                oi = oi + a_exp[i][s * m:(s + 1) * m, :] * vi[s:s + 1, :]
            o_sub.append(oi)
        q_hat = q * jnp.exp(c16)
        k_hat = [sub(key, i) * jnp.exp(sub(c16, i)[m - 1:m, :] - sub(c16, i)) for i in range(nsub)]
        b_start = [sub(c64, i)[0:1, :] - sub(c16, i)[0:1, :] for i in range(nsub)]
        b_end = [sub(c64, i)[m - 1:m, :] for i in range(nsub)]
        v_bd = [_stack_heads(sub(v, i)) for i in range(nsub)]
        for i in range(1, nsub):
            k_bd = jnp.concatenate([_stack_heads(k_hat[jj] * jnp.exp(b_start[i] - b_end[jj])) for jj in range(i)],
                                   axis=0)
            a_cat = _dot_nt(sub(q_hat, i), k_bd)
            o_sub[i] = o_sub[i] + _dot(a_cat, jnp.concatenate(v_bd[0:i], axis=0))
        oacc_ref[rows, :] = jnp.concatenate(o_sub, axis=0)
        last = c64[CHUNK - 1:CHUNK, :]
        st_ref[...] = st * jnp.exp(last) + jnp.where(bd, _dot_tn(v, key * jnp.exp(last - c64)), 0.0)
        return carry

    lax.fori_loop(0, rb // CHUNK, chunk, 0)
    o = oacc_ref[...]
    ms = _dot_lhs2(o * o, ones_bd) * (1.0 / HEAD_DIM)
    g = g_ref[0]
    o_ref[0] = (o * lax.rsqrt(ms + RMS_EPS) * gain * (g * _sigmoid(g))).astype(BF16)


def _hgrn(pb3, par, rb):
    b, s, _ = pb3.shape
    col = lambda n: pl.BlockSpec((1, rb, GW), lambda bi, i, n=n: (bi, i, n))
    return pl.pallas_call(
        functools.partial(_hgrn_body, rb=rb),
        grid=(b, s // rb),
        in_specs=[col(0), col(1), col(2), col(3),
                  pl.BlockSpec((8, GW), lambda bi, i: (0, 0))],
        out_specs=pl.BlockSpec((1, rb, GW), lambda bi, i: (bi, i, 0)),
        out_shape=jax.ShapeDtypeStruct((b, s, GW), BF16),
        scratch_shapes=[pltpu.VMEM((GW, GW), F32)] + [pltpu.VMEM((rb, GW), F32)] * 5,
        compiler_params=_cparams("parallel", "arbitrary"),
        name="hgrn",
    )(pb3, pb3, pb3, pb3, par)


def _expand_heads(x_cat, bd):
    return jnp.where(bd, jnp.concatenate([x_cat] * N_HEADS, axis=0), 0.0)


def _rwkv_chunk_maps(chunks, bd, strict_cat, incl_cat, eye_cat):
    c = CHUNK
    p_pair, p_inv, p_app = RWKV_PASSES
    pre = []
    for r, logw, k, v, alpha, beta in chunks:
        cum = _cumsum_rows(logw)
        e_last = jnp.exp(cum[c - 1:c, :])
        a_t = alpha * jnp.exp(cum - logw)
        r_t = r * jnp.exp(cum)
        e_inv = jnp.exp(-cum)
        b_t = beta * e_inv
        k_t = k * e_inv
        ar = jnp.concatenate([a_t, r_t], axis=0)
        bbd, kbd, vbd = _stack_heads(b_t), _stack_heads(k_t), _stack_heads(v)
        pair = _mm(ar, jnp.concatenate([bbd, kbd], axis=0), "nt", p_pair)
        l_ab = jnp.where(strict_cat, pair[0:c, 0:GW], 0.0)
        lm_k = jnp.where(jnp.concatenate([strict_cat, incl_cat], axis=0), pair[:, GW:], 0.0)
        m_rb = jnp.where(incl_cat, pair[c:, 0:GW], 0.0)
        wy0 = _mm(lm_k, vbd, "nn", p_app)
        pre.append((e_last, a_t, r_t, b_t, k_t, v, l_ab, m_rb, wy0))
    xs = [p[6] for p in pre]
    invs = [eye_cat + x for x in xs]
    xs = [_mm(x, _expand_heads(x, bd), "nn", p_inv[0]) for x in xs]
    for n in range(len(p_inv)):
        last = n == len(p_inv) - 1
        p_step = p_inv[n] if last else max(p_inv[n], p_inv[n + 1])
        xbds = [_expand_heads(x, bd) for x in xs]
        if last:
            invs = [inv + _mm(inv, xbd, "nn", p_step) for inv, xbd in zip(invs, xbds)]
        else:
            both = [_mm(jnp.concatenate([inv, x], axis=0), xbd, "nn", p_step)
                    for inv, x, xbd in zip(invs, xs, xbds)]
            invs = [inv + bo[0:c] for inv, bo in zip(invs, both)]
            xs = [bo[c:] for bo in both]
    outs = []
    for (e_last, a_t, r_t, b_t, k_t, v, l_ab, m_rb, wy0), inv in zip(pre, invs):
        sol = _mm(inv, jnp.concatenate([_stack_heads(a_t), _stack_heads(wy0[0:c])], axis=1), "nn", p_app)
        a_s, u0 = sol[:, 0:GW], sol[:, GW:]
        out = _mm(m_rb, jnp.concatenate([_stack_heads(a_s), _stack_heads(u0)], axis=1), "nn", p_app)
        ry = r_t + out[:, 0:GW]
        y0 = wy0[c:] + out[:, GW:]
        bh, kh = b_t * e_last, k_t * e_last
        g = jnp.where(bd, _mm(a_s, bh, "tn", p_app), 0.0)
        c0 = jnp.where(bd, _mm(jnp.concatenate([u0, v], axis=0), jnp.concatenate([bh, kh], axis=0), "tn", p_app),
                       0.0)
        outs.append((ry, y0, g, c0, e_last))
    return outs


def _rwkv_body(pc_ref, par_ref, w2_ref, a2_ref, g2_ref, o_ref, ht_ref, prev_ref,
               r_s, w_s, k_s, v_s, al_s, be_s, ry_s, y_s, bon_s, gate_s, g_s, c_s, el_s, *, rb):
    @pl.when(pl.program_id(1) == 0)
    def _():
        ht_ref[...] = jnp.zeros_like(ht_ref)
        prev_ref[...] = jnp.zeros_like(prev_ref)

    c = CHUNK
    mu_main = [par_ref[n:n + 1, :] for n in range(3)]
    mu_lo = par_ref[3:4, 0:RWKV_LORA]
    w0, a0 = par_ref[4:5, :], par_ref[5:6, :]
    k_k, k_a, r_k = par_ref[6:7, :], par_ref[7:8, :], par_ref[8:9, :]
    ln_w, ln_b = par_ref[9:10, :], par_ref[10:11, :]
    bd = _head_block_mask(GW)
    ones_bd = jnp.where(bd, 1.0, 0.0).astype(BF16)

    pc = pc_ref[0]
    row0 = lax.broadcasted_iota(jnp.int32, (rb, 1), 0) == 0
    prev = jnp.where(row0, prev_ref[...], pltpu.roll(pc, 1, 0))
    prev_ref[...] = pc[rb - 1:rb, :]
    xs = [pc[:, n * GW:(n + 1) * GW] for n in range(3)]
    ps = [prev[:, n * GW:(n + 1) * GW] for n in range(3)]
    r, k, v = [x + mu * (p - x) for x, p, mu in zip(xs, ps, mu_main)]
    lo, plo = pc[:, 3 * GW:], prev[:, 3 * GW:]
    lo = lo + mu_lo * (plo - lo)
    w_in = -(w0 + _dot(jnp.tanh(lo), w2_ref[...]))
    w_pre = -(jnp.maximum(w_in, 0.0) + _log1p_exp_neg_abs(w_in)) - 0.5
    a = _sigmoid(a0 + _dot(lo, a2_ref[...]))
    g = _dot(_sigmoid(lo), g2_ref[...])
    kk = k * k_k
    kk = kk / jnp.maximum(jnp.sqrt(_dot_lhs2(kk * kk, ones_bd)), 1e-12)
    k = k * (1.0 + (a - 1.0) * k_a)
    r_s[...] = r
    w_s[...] = -jnp.exp(w_pre)
    k_s[...] = k
    v_s[...] = v
    al_s[...] = -kk
    be_s[...] = kk * a
    bon_s[...] = _dot_lhs2(r * k * r_k, ones_bd) * v
    gate_s[...] = g

    tt = lax.broadcasted_iota(jnp.int32, (c, GW), 0)
    ss = lax.broadcasted_iota(jnp.int32, (c, GW), 1) % c
    strict_cat, incl_cat = tt > ss, tt >= ss
    eye_cat = jnp.where(tt == ss, 1.0, 0.0)

    nb = RWKV_MAP_BATCH

    def chunk_maps(jj, carry):
        js = [jj * nb + n for n in range(nb)]
        rows = [pl.ds(pl.multiple_of(j * c, c), c) for j in js]
        maps = _rwkv_chunk_maps([(r_s[rw, :], w_s[rw, :], k_s[rw, :], v_s[rw, :], al_s[rw, :], be_s[rw, :])
                                 for rw in rows], bd, strict_cat, incl_cat, eye_cat)
        for j, rw, (ry, y0, g, c0, e_last) in zip(js, rows, maps):
            ry_s[rw, :] = ry
            y_s[rw, :] = y0
            g_s[j] = g.astype(BF16)
            c_s[j] = c0
            el_s[j] = jnp.broadcast_to(e_last, (8, GW))
        return carry

    lax.fori_loop(0, rb // (c * nb), chunk_maps, 0)

    def chunk_apply(j, carry):
        rows = pl.ds(pl.multiple_of(j * c, c), c)
        ht = ht_ref[...]
        htb = ht.astype(BF16)
        y_s[rows, :] = y_s[rows, :] + _dot_nt(ry_s[rows, :], htb)
        ht_ref[...] = ht * el_s[j][0:1, :] + jnp.dot(htb, g_s[j], preferred_element_type=F32) + c_s[j]
        return carry

    lax.fori_loop(0, rb // c, chunk_apply, 0, unroll=2)

    y = y_s[...]
    mean = _dot_lhs2(y, ones_bd) * (1.0 / HEAD_DIM)
    yc = y - mean
    var = _dot_lhs2(yc * yc, ones_bd) * (1.0 / HEAD_DIM)
    yn = yc * lax.rsqrt(var + RWKV_GN_EPS) * ln_w + ln_b
    o_ref[0] = ((yn + bon_s[...]) * gate_s[...]).astype(BF16)


def _rwkv(pc3, par, w2p, a2p, g2p, rb):
    b, s, _ = pc3.shape
    full = lambda shp: pl.BlockSpec(shp, lambda bi, i: (0,) * len(shp))
    return pl.pallas_call(
        functools.partial(_rwkv_body, rb=rb),
        grid=(b, s // rb),
        in_specs=[pl.BlockSpec((1, rb, RWKV_COLS), lambda bi, i: (bi, i, 0)),
                  full((16, GW)), full((RWKV_LORA, GW)), full((RWKV_LORA, GW)), full((RWKV_LORA, GW))],
        out_specs=pl.BlockSpec((1, rb, GW), lambda bi, i: (bi, i, 0)),
        out_shape=jax.ShapeDtypeStruct((b, s, GW), BF16),
        scratch_shapes=[pltpu.VMEM((GW, GW), F32), pltpu.VMEM((1, RWKV_COLS), F32)]
        + [pltpu.VMEM((rb, GW), F32)] * 10
        + [pltpu.VMEM((rb // CHUNK, GW, GW), BF16), pltpu.VMEM((rb // CHUNK, GW, GW), F32),
           pltpu.VMEM((rb // CHUNK, 8, GW), F32)],
        compiler_params=_cparams("parallel", "arbitrary"),
        name="rwkv",
    )(pc3, par, w2p, a2p, g2p)


def _lru_body(x_ref, g_ref, par_ref, wa_ref, wx_ref, o_ref, ext_ref, h_ref, *, rb):
    @pl.when(pl.program_id(1) == 0)
    def _():
        ext_ref[0:8, :] = jnp.zeros((8, GW), F32)
        h_ref[...] = jnp.zeros_like(h_ref)

    conv_b, ba, bx, lam = par_ref[4:5, :], par_ref[5:6, :], par_ref[6:7, :], par_ref[7:8, :]
    x = x_ref[0]
    ext_ref[8:8 + rb, :] = x
    conv = conv_b + par_ref[0:1, :] * ext_ref[pl.ds(8 - 3, rb), :]
    for j in range(1, LRU_CONV):
        conv = conv + par_ref[j:j + 1, :] * ext_ref[pl.ds(8 - 3 + j, rb), :]
    ext_ref[0:8, :] = x[rb - 8:rb, :]
    gate_r = _sigmoid(_dot(conv, wa_ref[...]) + ba)
    gate_i = _sigmoid(_dot(conv, wx_ref[...]) + bx)
    log_a = -LRU_C * gate_r * _softplus(-lam)
    a = jnp.exp(log_a)
    inp = jnp.sqrt(-_expm1(2.0 * log_a)) * (gate_i * conv)
    row = lax.broadcasted_iota(jnp.int32, (rb, 1), 0)
    d = 1
    while d < rb:
        m = row >= d
        inp = jnp.where(m, a * pltpu.roll(inp, d, 0) + inp, inp)
        a = jnp.where(m, a * pltpu.roll(a, d, 0), a)
        d *= 2
    h = inp + a * h_ref[...]
    h_ref[...] = h[rb - 1:rb, :]
    g = g_ref[0]
    gelu = 0.5 * g * (1.0 + jnp.tanh(math.sqrt(2.0 / math.pi) * (g + 0.044715 * (g * g * g))))
    o_ref[0] = (h * gelu).astype(BF16)


def _lru(pd3, par, wa_bd, wx_bd, rb):
    b, s, _ = pd3.shape
    full = lambda shp: pl.BlockSpec(shp, lambda bi, i: (0,) * len(shp))
    return pl.pallas_call(
        functools.partial(_lru_body, rb=rb),
        grid=(b, s // rb),
        in_specs=[pl.BlockSpec((1, rb, GW), lambda bi, i: (bi, i, 0)),
                  pl.BlockSpec((1, rb, GW), lambda bi, i: (bi, i, 1)),
                  full((8, GW)), full((GW, GW)), full((GW, GW))],
        out_specs=pl.BlockSpec((1, rb, GW), lambda bi, i: (bi, i, 0)),
        out_shape=jax.ShapeDtypeStruct((b, s, GW), BF16),
        scratch_shapes=[pltpu.VMEM((rb + 8, GW), F32), pltpu.VMEM((1, GW), F32)],
        compiler_params=_cparams("parallel", "arbitrary"),
        name="lru",
    )(pd3, pd3, par, wa_bd, wx_bd)


def _outproj_body(ya_ref, yb_ref, yc_ref, yd_ref, x_ref, w_ref, g_ref, o_ref):
    mix = jnp.concatenate([ya_ref[...], yb_ref[...], yc_ref[...], yd_ref[...]], axis=1)
    o_ref[...] = x_ref[...] + _rms(jnp.dot(mix, w_ref[...], preferred_element_type=F32), g_ref[...])


def _outproj(ya, yb, yc, yd, x2, w, gain, tm):
    t, d = x2.shape
    mix = pl.BlockSpec((tm, GW), lambda i: (i, 0))
    return pl.pallas_call(
        _outproj_body,
        grid=(t // tm,),
        in_specs=[mix, mix, mix, mix,
                  pl.BlockSpec((tm, d), lambda i: (i, 0)),
                  pl.BlockSpec((4 * GW, d), lambda i: (0, 0)),
                  pl.BlockSpec((1, d), lambda i: (0, 0))],
        out_specs=pl.BlockSpec((tm, d), lambda i: (i, 0)),
        out_shape=jax.ShapeDtypeStruct((t, d), F32),
        compiler_params=_cparams("parallel"),
        name="outproj",
    )(ya, yb, yc, yd, x2, w, gain)


def _mlp_body(x_ref, g1_ref, w1_ref, w2_ref, g2_ref, o_ref, h_ref, acc_ref):
    kk = pl.program_id(1)

    @pl.when(kk == 0)
    def _():
        h_ref[...] = _rms(x_ref[...], g1_ref[...]).astype(BF16)
        acc_ref[...] = jnp.zeros_like(acc_ref)

    a = jnp.dot(h_ref[...], w1_ref[...], preferred_element_type=F32)
    a = jnp.square(jnp.maximum(a, 0.0)).astype(BF16)
    acc_ref[...] += jnp.dot(a, w2_ref[...], preferred_element_type=F32)

    @pl.when(kk == pl.num_programs(1) - 1)
    def _():
        o_ref[...] = x_ref[...] + _rms(acc_ref[...], g2_ref[...])


def _mlp(x2, g1, w1, w2, g2, tm, tk):
    t, d = x2.shape
    hid = w1.shape[1]
    return pl.pallas_call(
        _mlp_body,
        grid=(t // tm, hid // tk),
        in_specs=[pl.BlockSpec((tm, d), lambda i, k: (i, 0)),
                  pl.BlockSpec((1, d), lambda i, k: (0, 0)),
                  pl.BlockSpec((d, tk), lambda i, k: (0, k)),
                  pl.BlockSpec((tk, d), lambda i, k: (k, 0)),
                  pl.BlockSpec((1, d), lambda i, k: (0, 0))],
        out_specs=pl.BlockSpec((tm, d), lambda i, k: (i, 0)),
        out_shape=jax.ShapeDtypeStruct((t, d), F32),
        scratch_shapes=[pltpu.VMEM((tm, d), BF16), pltpu.VMEM((tm, d), F32)],
        compiler_params=_cparams("parallel", "arbitrary"),
        name="mlp",
    )(x2, g1, w1, w2, g2)


def _rows(vectors, n_rows):
    tab = jnp.stack([v.astype(F32) for v in vectors])
    return jnp.pad(tab, ((0, n_rows - tab.shape[0]), (0, 0)))


def _block_diag(w):
    out = jnp.zeros((GW, GW), w.dtype)
    for n in range(w.shape[0]):
        out = out.at[n * HEAD_DIM:(n + 1) * HEAD_DIM, n * HEAD_DIM:(n + 1) * HEAD_DIM].set(w[n])
    return out


def _pick_block(n, want):
    while n % want:
        want //= 2
    return want


def kernel(x, norm_mix_pre, norm_mix_post, norm_mlp_pre, norm_mlp_post, w_in, w_out, attn_rel_bias, hgrn_lb_logits, hgrn_norm, rwkv_mu, rwkv_w0, rwkv_w2, rwkv_a0, rwkv_a2, rwkv_g2, rwkv_k_k, rwkv_k_a, rwkv_r_k, rwkv_ln_w, rwkv_ln_b, lru_conv_w, lru_conv_b, lru_wa, lru_ba, lru_wx, lru_bx, lru_lambda, mlp_w1, mlp_w2):
    b, s, d = x.shape
    depth = w_in.shape[0]
    t = b * s
    tm = _pick_block(t, 512)
    tm_mlp = _pick_block(t, 1024)
    rb = _pick_block(s, 512)
    tk = _pick_block(mlp_w1.shape[-1], 1024)

    lb_sm = jax.nn.softmax(hgrn_lb_logits.astype(F32), axis=0)
    lb_all = jnp.maximum(jnp.cumsum(lb_sm, axis=0) - lb_sm[0:1], 0.0)

    x2 = x.reshape(t, d)
    for l in range(depth):
        pa, pb, pc, pd = _inproj(x2, norm_mix_pre[l].reshape(1, d), w_in[l].astype(BF16), tm)

        ya = _attention(pa.reshape(b, s, ATTN_COLS), _attn_bias_table(attn_rel_bias[l]), rb)

        lb = lb_all[l]
        hgrn_par = _rows([jnp.log(lb), jnp.log1p(-lb), 1.0 - lb, hgrn_norm[l]], 8)
        yb = _hgrn(pb.reshape(b, s, HGRN_COLS), hgrn_par, rb)

        mu = rwkv_mu[l].astype(F32)
        mu_lo = jnp.pad(mu[3 * GW:], (0, GW - RWKV_LORA))
        rwkv_par = _rows([mu[0:GW], mu[GW:2 * GW], mu[2 * GW:3 * GW], mu_lo, rwkv_w0[l], rwkv_a0[l],
                          rwkv_k_k[l], rwkv_k_a[l], rwkv_r_k[l].reshape(GW), rwkv_ln_w[l], rwkv_ln_b[l]], 16)
        zeros = lambda n: jnp.zeros((n, GW), F32)
        w2p = jnp.concatenate([rwkv_w2[l].astype(F32), zeros(96)], axis=0).astype(BF16)
        a2p = jnp.concatenate([zeros(32), rwkv_a2[l].astype(F32), zeros(64)], axis=0).astype(BF16)
        g2p = jnp.concatenate([zeros(64), rwkv_g2[l].astype(F32)], axis=0).astype(BF16)
        yc = _rwkv(pc.reshape(b, s, RWKV_COLS), rwkv_par, w2p, a2p, g2p, rb)

        cw = lru_conv_w[l].astype(F32)
        lru_par = _rows([cw[0], cw[1], cw[2], cw[3], lru_conv_b[l], lru_ba[l], lru_bx[l], lru_lambda[l]], 8)
        yd = _lru(pd.reshape(b, s, LRU_COLS), lru_par, _block_diag(lru_wa[l]).astype(BF16),
                  _block_diag(lru_wx[l]).astype(BF16), rb)

        x2 = _outproj(ya.reshape(t, GW), yb.reshape(t, GW), yc.reshape(t, GW), yd.reshape(t, GW),
                      x2, w_out[l].astype(BF16), norm_mix_post[l].reshape(1, d), tm)
        x2 = _mlp(x2, norm_mlp_pre[l].reshape(1, d), mlp_w1[l].astype(BF16), mlp_w2[l].astype(BF16),
                  norm_mlp_post[l].reshape(1, d), tm_mlp, tk)
    return x2.reshape(b, s, d)
```

```python
import functools
import math

import numpy as np

import jax
import jax.numpy as jnp
from jax import lax
from jax.experimental import pallas as pl
from jax.experimental.pallas import tpu as pltpu

F32 = jnp.float32
BF16 = jnp.bfloat16

GW = 256
HEAD_DIM = 64
N_HEADS = 4
CHUNK = 64
ATTN_LEFT_CHUNKS = 8
BAND = (ATTN_LEFT_CHUNKS + 1) * CHUNK
ATTN_PAD = ATTN_LEFT_CHUNKS * CHUNK
REL_CLIP = 256
ATTN_SCALE = HEAD_DIM ** -0.5
NEG_INF = -1e30
RMS_EPS = 1e-6
RWKV_GN_EPS = HEAD_DIM * 1e-5
RWKV_LORA = 128
LRU_C = 8.0
LRU_CONV = 4
HGRN_SUB = 16

ATTN_COLS = 3 * GW
HGRN_COLS = 4 * GW
RWKV_COLS = 3 * GW + RWKV_LORA
LRU_COLS = 2 * GW

VMEM_LIMIT = 56 * 1024 * 1024
RWKV_PASSES = (1, (3, 3, 3, 1, 1), 1)
RWKV_MAP_BATCH = 8
ATTN_BATCH = 4


def _cparams(*sem):
    return pltpu.CompilerParams(dimension_semantics=sem, vmem_limit_bytes=VMEM_LIMIT)


def _dot(a, b):
    return jnp.dot(a.astype(BF16), b.astype(BF16), preferred_element_type=F32)


def _dot_nt(a, b):
    return lax.dot_general(a.astype(BF16), b.astype(BF16), (((1,), (1,)), ((), ())),
                           preferred_element_type=F32)


def _dot_tn(a, b):
    return lax.dot_general(a.astype(BF16), b.astype(BF16), (((0,), (0,)), ((), ())),
                           preferred_element_type=F32)


def _split(a):
    hi = a.astype(BF16)
    lo = (a - hi.astype(F32)).astype(BF16)
    return hi, lo


def _mm(a, b, kind="nn", passes=1):
    f = {"nn": _dot, "nt": _dot_nt, "tn": _dot_tn}[kind]
    if passes == 1:
        return f(a, b)
    ah, al = _split(a)
    bh, bl = _split(b)
    if kind == "tn":
        return f(ah, bh) + (f(ah, bl) + f(al, bh))
    m = a.shape[0]
    top = f(jnp.concatenate([ah, al], axis=0), bh)
    return top[0:m] + (f(ah, bl) + top[m:])


def _rms(x, gain):
    return x * lax.rsqrt(jnp.mean(x * x, axis=-1, keepdims=True) + RMS_EPS) * gain


def _sigmoid(x):
    return 1.0 / (1.0 + jnp.exp(-x))


def _softplus(x):
    return jnp.maximum(x, 0.0) + jnp.log1p(jnp.exp(-jnp.abs(x)))


def _log1p_exp_neg_abs(x):
    return jnp.log(1.0 + jnp.exp(-jnp.abs(x)))


def _expm1(z):
    u = jnp.exp(z)
    um1 = u - 1.0
    near = um1 * z / jnp.where(u == 1.0, 1.0, jnp.log(u))
    return jnp.where(jnp.abs(z) > 0.5, um1, jnp.where(u == 1.0, z, near))


def _head_block_mask(n):
    r = lax.broadcasted_iota(jnp.int32, (n, n), 0) // HEAD_DIM
    c = lax.broadcasted_iota(jnp.int32, (n, n), 1) // HEAD_DIM
    return r == c


def _cumsum_rows(x):
    n = x.shape[0]
    row = lax.broadcasted_iota(jnp.int32, (n, 1), 0)
    d = 1
    while d < n:
        x = x + jnp.where(row >= d, pltpu.roll(x, d, 0), 0.0)
        d *= 2
    return x


def _stack_heads(x):
    lane_head = lax.broadcasted_iota(jnp.int32, (1, GW), 1) // HEAD_DIM
    return jnp.concatenate([jnp.where(lane_head == h, x, 0.0) for h in range(N_HEADS)], axis=0)


def _unstack_heads(xbd, c):
    return xbd[0:c] + xbd[c:2 * c] + xbd[2 * c:3 * c] + xbd[3 * c:4 * c]


def _inproj_body(x_ref, g_ref, w_ref, oa_ref, ob_ref, oc_ref, od_ref):
    h = _rms(x_ref[...], g_ref[...]).astype(BF16)
    o0, o1, o2 = ATTN_COLS, ATTN_COLS + HGRN_COLS, ATTN_COLS + HGRN_COLS + RWKV_COLS
    oa_ref[...] = jnp.dot(h, w_ref[:, 0:o0], preferred_element_type=F32).astype(BF16)
    ob_ref[...] = jnp.dot(h, w_ref[:, o0:o1], preferred_element_type=F32)
    oc_ref[...] = jnp.dot(h, w_ref[:, o1:o2], preferred_element_type=F32)
    od_ref[...] = jnp.dot(h, w_ref[:, o2:], preferred_element_type=F32)


def _inproj(x2, gain, w_all, layer, tm):
    t, d = x2.shape
    d_in = w_all.shape[2]
    return pl.pallas_call(
        _inproj_body,
        grid=(t // tm,),
        in_specs=[pl.BlockSpec((tm, d), lambda i: (i, 0)),
                  pl.BlockSpec((1, d), lambda i: (0, 0)),
                  pl.BlockSpec((None, d, d_in), lambda i: (layer, 0, 0))],
        out_specs=[pl.BlockSpec((tm, ATTN_COLS), lambda i: (i, 0)),
                   pl.BlockSpec((tm, HGRN_COLS), lambda i: (i, 0)),
                   pl.BlockSpec((tm, RWKV_COLS), lambda i: (i, 0)),
                   pl.BlockSpec((tm, LRU_COLS), lambda i: (i, 0))],
        out_shape=[jax.ShapeDtypeStruct((t, ATTN_COLS), BF16),
                   jax.ShapeDtypeStruct((t, HGRN_COLS), F32),
                   jax.ShapeDtypeStruct((t, RWKV_COLS), F32),
                   jax.ShapeDtypeStruct((t, LRU_COLS), F32)],
        compiler_params=_cparams("parallel"),
        name="inproj",
    )(x2, gain, w_all)


def _attn_body(q_ref, k_ref, v_ref, bias_ref, o_ref, kp_ref, vp_ref, *, chunks):
    i = pl.program_id(1)

    @pl.when(i == 0)
    def _():
        for src, dst in ((k_ref, kp_ref), (v_ref, vp_ref)):
            dst[0:ATTN_PAD, :] = jnp.zeros((ATTN_PAD, GW), BF16)
            dst[ATTN_PAD:, :] = src[0]

    lane_head = lax.broadcasted_iota(jnp.int32, (1, GW), 1) // HEAD_DIM
    kpos = lax.broadcasted_iota(jnp.int32, (1, BAND), 1)

    nb = ATTN_BATCH

    def chunk_group(jj, carry):
        cs = [i * chunks + jj * nb + n for n in range(nb)]
        rows = [pl.ds(pl.multiple_of((jj * nb + n) * CHUNK, CHUNK), CHUNK) for n in range(nb)]
        wins = [pl.ds(pl.multiple_of(c * CHUNK, CHUNK), BAND) for c in cs]
        ss = []
        for c, rw, win in zip(cs, rows, wins):
            q = q_ref[0, rw, :] * ATTN_SCALE
            qbd = jnp.concatenate([jnp.where(lane_head == h, q, jnp.zeros_like(q)) for h in range(N_HEADS)],
                                  axis=0)
            s = _dot_nt(qbd, kp_ref[win, :]) + bias_ref[...]
            ss.append(jnp.where(kpos >= (ATTN_LEFT_CHUNKS - c) * CHUNK, s, NEG_INF))
        ps = [jnp.exp(s - jnp.max(s, axis=-1, keepdims=True)) for s in ss]
        for rw, win, p in zip(rows, wins, ps):
            inv_l = 1.0 / jnp.sum(p, axis=-1, keepdims=True)
            obd = _dot(p, vp_ref[win, :]) * inv_l
            o = jnp.zeros((CHUNK, GW), F32)
            for h in range(N_HEADS):
                o = o + jnp.where(lane_head == h, obd[h * CHUNK:(h + 1) * CHUNK, :], 0.0)
            o_ref[0, rw, :] = o.astype(BF16)
        return carry

    lax.fori_loop(0, chunks // nb, chunk_group, 0)


def _attention(pa3, bias, rb):
    b, s, _ = pa3.shape
    return pl.pallas_call(
        functools.partial(_attn_body, chunks=rb // CHUNK),
        grid=(b, s // rb),
        in_specs=[pl.BlockSpec((1, rb, GW), lambda bi, i: (bi, i, 0)),
                  pl.BlockSpec((1, s, GW), lambda bi, i: (bi, 0, 1)),
                  pl.BlockSpec((1, s, GW), lambda bi, i: (bi, 0, 2)),
                  pl.BlockSpec((N_HEADS * CHUNK, BAND), lambda bi, i: (0, 0))],
        out_specs=pl.BlockSpec((1, rb, GW), lambda bi, i: (bi, i, 0)),
        out_shape=jax.ShapeDtypeStruct((b, s, GW), BF16),
        scratch_shapes=[pltpu.VMEM((s + ATTN_PAD, GW), BF16)] * 2,
        compiler_params=_cparams("parallel", "arbitrary"),
        name="attn",
    )(pa3, pa3, pa3, bias)


def _attn_bias_table(rel_bias):
    rel = np.arange(BAND + CHUNK - 1) - (CHUNK - 1) - ATTN_LEFT_CHUNKS * CHUNK
    ext = rel_bias.astype(F32)[:, np.clip(rel, -REL_CLIP, REL_CLIP) + REL_CLIP]
    tab = jnp.stack([ext[:, CHUNK - 1 - q:CHUNK - 1 - q + BAND] for q in range(CHUNK)], axis=1)
    return tab.reshape(N_HEADS * CHUNK, BAND)


def _hgrn_body(q_ref, f_ref, i_ref, g_ref, par_ref, o_ref, st_ref, oacc_ref, q_s, key_s, cum16_s, cum64_s,
               *, rb):
    @pl.when(pl.program_id(1) == 0)
    def _():
        st_ref[...] = jnp.zeros_like(st_ref)

    m = HGRN_SUB
    lb = par_ref[0:1, :]
    log_1m_lb = par_ref[1:2, :]
    one_m_lb = par_ref[2:3, :]
    gain = par_ref[3:4, :]
    bd = _head_block_mask(GW)
    ones_bd = jnp.where(bd, 1.0, 0.0).astype(BF16)
    trow = lax.broadcasted_iota(jnp.int32, (m, 1), 0)

    fr = f_ref[0]
    e = jnp.exp(-jnp.abs(fr))
    inv_1pe = 1.0 / (1.0 + e)
    sig = jnp.where(fr >= 0.0, inv_1pe, e * inv_1pe)
    log_sig = jnp.minimum(fr, 0.0) - jnp.log(1.0 + e)
    log_f = jnp.maximum(jnp.log(lb + one_m_lb * sig), log_1m_lb + log_sig)
    row = lax.broadcasted_iota(jnp.int32, (rb, 1), 0) % CHUNK
    cum, d = log_f, 1
    while d < CHUNK:
        cum = cum + jnp.where(row >= d, pltpu.roll(cum, d, 0), 0.0)
        d *= 2
    cum64_s[...] = cum
    before = jnp.where(row == 0, 0.0, pltpu.roll(cum, 1, 0)).reshape(rb // m, m, GW)[:, 0:1, :]
    cum16_s[...] = cum - jnp.broadcast_to(before, (rb // m, m, GW)).reshape(rb, GW)
    key_s[...] = one_m_lb * jnp.where(fr >= 0.0, e * inv_1pe, inv_1pe)
    qr = q_ref[0]
    q_s[...] = qr * _sigmoid(qr)

    nsub = CHUNK // m

    def chunk(j, carry):
        rows = pl.ds(pl.multiple_of(j * CHUNK, CHUNK), CHUNK)
        q, key, v = q_s[rows, :], key_s[rows, :], i_ref[0, rows, :]
        c16, c64 = cum16_s[rows, :], cum64_s[rows, :]
        sub = lambda x, i: x[i * m:(i + 1) * m, :]
        st = st_ref[...]
        o = _dot_nt(q * jnp.exp(c64), st)
        a_exp = []
        for i in range(nsub):
            qi, ki, ci = sub(q, i), sub(key, i), sub(c16, i)
            ps = []
            for s in range(m):
                dec = jnp.exp(jnp.where(trow >= s, ci - ci[s:s + 1, :], NEG_INF))
                ps.append(qi * dec * ki[s:s + 1, :])
            a_exp.append(_dot(jnp.concatenate(ps, axis=0), ones_bd))
        o_sub = []
        for i in range(nsub):
            vi, oi = sub(v, i), sub(o, i)
            for s in range(m):
                oi = oi + a_exp[i][s * m:(s + 1) * m, :] * vi[s:s + 1, :]
            o_sub.append(oi)
        q_hat = q * jnp.exp(c16)
        k_hat = [sub(key, i) * jnp.exp(sub(c16, i)[m - 1:m, :] - sub(c16, i)) for i in range(nsub)]
        b_start = [sub(c64, i)[0:1, :] - sub(c16, i)[0:1, :] for i in range(nsub)]
        b_end = [sub(c64, i)[m - 1:m, :] for i in range(nsub)]
        v_bd = [_stack_heads(sub(v, i)) for i in range(nsub)]
        for i in range(1, nsub):
            k_bd = jnp.concatenate([_stack_heads(k_hat[jj] * jnp.exp(b_start[i] - b_end[jj])) for jj in range(i)],
                                   axis=0)
            a_cat = _dot_nt(sub(q_hat, i), k_bd)
            o_sub[i] = o_sub[i] + _dot(a_cat, jnp.concatenate(v_bd[0:i], axis=0))
        oacc_ref[rows, :] = jnp.concatenate(o_sub, axis=0)
        last = c64[CHUNK - 1:CHUNK, :]
        st_ref[...] = st * jnp.exp(last) + jnp.where(bd, _dot_tn(v, key * jnp.exp(last - c64)), 0.0)
        return carry

    lax.fori_loop(0, rb // CHUNK, chunk, 0)
    o = oacc_ref[...]
    ms = _dot(o * o, ones_bd) * (1.0 / HEAD_DIM)
    g = g_ref[0]
    o_ref[0] = (o * lax.rsqrt(ms + RMS_EPS) * gain * (g * _sigmoid(g))).astype(BF16)


def _hgrn(pb3, par, rb):
    b, s, _ = pb3.shape
    col = lambda n: pl.BlockSpec((1, rb, GW), lambda bi, i, n=n: (bi, i, n))
    return pl.pallas_call(
        functools.partial(_hgrn_body, rb=rb),
        grid=(b, s // rb),
        in_specs=[col(0), col(1), col(2), col(3),
                  pl.BlockSpec((8, GW), lambda bi, i: (0, 0))],
        out_specs=pl.BlockSpec((1, rb, GW), lambda bi, i: (bi, i, 0)),
        out_shape=jax.ShapeDtypeStruct((b, s, GW), BF16),
        scratch_shapes=[pltpu.VMEM((GW, GW), F32)] + [pltpu.VMEM((rb, GW), F32)] * 5,
        compiler_params=_cparams("parallel", "arbitrary"),
        name="hgrn",
    )(pb3, pb3, pb3, pb3, par)


def _expand_heads(x_cat, bd):
    return jnp.where(bd, jnp.concatenate([x_cat] * N_HEADS, axis=0), 0.0)


def _mm_cat(a, x_cat, bd, passes):
    if passes == 1:
        return _dot(a, _expand_heads(x_cat.astype(BF16), bd))
    ah, al = _split(a)
    xh, xl = _split(x_cat)
    m = a.shape[0]
    top = _dot(jnp.concatenate([ah, al], axis=0), _expand_heads(xh, bd))
    return top[0:m] + (_dot(ah, _expand_heads(xl, bd)) + top[m:])


def _rwkv_chunk_maps(chunks, bd, strict_cat, incl_cat, eye_cat):
    c = CHUNK
    p_pair, p_inv, p_app = RWKV_PASSES
    pre = []
    for r, logw, k, v, alpha, beta in chunks:
        cum = _cumsum_rows(logw)
        e_last = jnp.exp(cum[c - 1:c, :])
        a_t = alpha * jnp.exp(cum - logw)
        r_t = r * jnp.exp(cum)
        e_inv = jnp.exp(-cum)
        b_t = beta * e_inv
        k_t = k * e_inv
        ar = jnp.concatenate([a_t, r_t], axis=0)
        bbd, kbd, vbd = _stack_heads(b_t), _stack_heads(k_t), _stack_heads(v)
        pair = _mm(ar, jnp.concatenate([bbd, kbd], axis=0), "nt", p_pair)
        l_ab = jnp.where(strict_cat, pair[0:c, 0:GW], 0.0)
        lm_k = jnp.where(jnp.concatenate([strict_cat, incl_cat], axis=0), pair[:, GW:], 0.0)
        m_rb = jnp.where(incl_cat, pair[c:, 0:GW], 0.0)
        wy0 = _mm(lm_k, vbd, "nn", p_app)
        pre.append((e_last, a_t, r_t, b_t, k_t, v, l_ab, m_rb, wy0))
    xs = [p[6] for p in pre]
    invs = [eye_cat + x for x in xs]
    xs = [_mm_cat(x, x, bd, p_inv[0]) for x in xs]
    for n in range(len(p_inv)):
        last = n == len(p_inv) - 1
        p_step = p_inv[n] if last else max(p_inv[n], p_inv[n + 1])
        if last:
            invs = [inv + _mm_cat(inv, x, bd, p_step) for inv, x in zip(invs, xs)]
        else:
            both = [_mm_cat(jnp.concatenate([inv, x], axis=0), x, bd, p_step) for inv, x in zip(invs, xs)]
            invs = [inv + bo[0:c] for inv, bo in zip(invs, both)]
            xs = [bo[c:] for bo in both]
    outs = []
    for (e_last, a_t, r_t, b_t, k_t, v, l_ab, m_rb, wy0), inv in zip(pre, invs):
        sol = _mm(inv, jnp.concatenate([_stack_heads(a_t), _stack_heads(wy0[0:c])], axis=1), "nn", p_app)
        a_s, u0 = sol[:, 0:GW], sol[:, GW:]
        out = _mm(m_rb, jnp.concatenate([_stack_heads(a_s), _stack_heads(u0)], axis=1), "nn", p_app)
        ry = r_t + out[:, 0:GW]
        y0 = wy0[c:] + out[:, GW:]
        bh, kh = b_t * e_last, k_t * e_last
        g = jnp.where(bd, _mm(a_s, bh, "tn", p_app), 0.0)
        c0 = jnp.where(bd, _mm(jnp.concatenate([u0, v], axis=0), jnp.concatenate([bh, kh], axis=0), "tn", p_app),
                       0.0)
        outs.append((ry, y0, g, c0, e_last))
    return outs


def _rwkv_body(pc_ref, par_ref, w2_ref, a2_ref, g2_ref, o_ref, ht_ref, prev_ref,
               r_s, w_s, k_s, v_s, al_s, be_s, ry_s, y_s, bon_s, gate_s, g_s, c_s, el_s, *, rb):
    @pl.when(pl.program_id(1) == 0)
    def _():
        ht_ref[...] = jnp.zeros_like(ht_ref)
        prev_ref[...] = jnp.zeros_like(prev_ref)

    c = CHUNK
    mu_main = [par_ref[n:n + 1, :] for n in range(3)]
    mu_lo = par_ref[3:4, 0:RWKV_LORA]
    w0, a0 = par_ref[4:5, :], par_ref[5:6, :]
    k_k, k_a, r_k = par_ref[6:7, :], par_ref[7:8, :], par_ref[8:9, :]
    ln_w, ln_b = par_ref[9:10, :], par_ref[10:11, :]
    bd = _head_block_mask(GW)
    ones_bd = jnp.where(bd, 1.0, 0.0).astype(BF16)

    pc = pc_ref[0]
    row0 = lax.broadcasted_iota(jnp.int32, (rb, 1), 0) == 0
    prev = jnp.where(row0, prev_ref[...], pltpu.roll(pc, 1, 0))
    prev_ref[...] = pc[rb - 1:rb, :]
    xs = [pc[:, n * GW:(n + 1) * GW] for n in range(3)]
    ps = [prev[:, n * GW:(n + 1) * GW] for n in range(3)]
    r, k, v = [x + mu * (p - x) for x, p, mu in zip(xs, ps, mu_main)]
    lo, plo = pc[:, 3 * GW:], prev[:, 3 * GW:]
    lo = lo + mu_lo * (plo - lo)
    w_in = -(w0 + _dot(jnp.tanh(lo), w2_ref[...]))
    w_pre = -(jnp.maximum(w_in, 0.0) + _log1p_exp_neg_abs(w_in)) - 0.5
    a = _sigmoid(a0 + _dot(lo, a2_ref[...]))
    g = _dot(_sigmoid(lo), g2_ref[...])
    kk = k * k_k
    kk = kk / jnp.maximum(jnp.sqrt(_dot(kk * kk, ones_bd)), 1e-12)
    k = k * (1.0 + (a - 1.0) * k_a)
    r_s[...] = r
    w_s[...] = -jnp.exp(w_pre)
    k_s[...] = k
    v_s[...] = v
    al_s[...] = -kk
    be_s[...] = kk * a
    bon_s[...] = _dot(r * k * r_k, ones_bd) * v
    gate_s[...] = g

    tt = lax.broadcasted_iota(jnp.int32, (c, GW), 0)
    ss = lax.broadcasted_iota(jnp.int32, (c, GW), 1) % c
    strict_cat, incl_cat = tt > ss, tt >= ss
    eye_cat = jnp.where(tt == ss, 1.0, 0.0)

    nb = RWKV_MAP_BATCH

    def chunk_maps(jj, carry):
        js = [jj * nb + n for n in range(nb)]
        rows = [pl.ds(pl.multiple_of(j * c, c), c) for j in js]
        maps = _rwkv_chunk_maps([(r_s[rw, :], w_s[rw, :], k_s[rw, :], v_s[rw, :], al_s[rw, :], be_s[rw, :])
                                 for rw in rows], bd, strict_cat, incl_cat, eye_cat)
        for j, rw, (ry, y0, g, c0, e_last) in zip(js, rows, maps):
            ry_s[rw, :] = ry
            y_s[rw, :] = y0
            g_s[j] = g.astype(BF16)
            c_s[j] = c0
            el_s[j] = jnp.broadcast_to(e_last, (8, GW))
        return carry

    lax.fori_loop(0, rb // (c * nb), chunk_maps, 0)

    def chunk_apply(j, carry):
        rows = pl.ds(pl.multiple_of(j * c, c), c)
        ht = ht_ref[...]
        htb = ht.astype(BF16)
        y_s[rows, :] = y_s[rows, :] + _dot_nt(ry_s[rows, :], htb)
        ht_ref[...] = ht * el_s[j][0:1, :] + jnp.dot(htb, g_s[j], preferred_element_type=F32) + c_s[j]
        return carry

    lax.fori_loop(0, rb // c, chunk_apply, 0, unroll=2)

    y = y_s[...]
    mean = _dot(y, ones_bd) * (1.0 / HEAD_DIM)
    yc = y - mean
    var = _dot(yc * yc, ones_bd) * (1.0 / HEAD_DIM)
    yn = yc * lax.rsqrt(var + RWKV_GN_EPS) * ln_w + ln_b
    o_ref[0] = ((yn + bon_s[...]) * gate_s[...]).astype(BF16)


def _rwkv(pc3, par, w2p, a2p, g2p, rb):
    b, s, _ = pc3.shape
    full = lambda shp: pl.BlockSpec(shp, lambda bi, i: (0,) * len(shp))
    return pl.pallas_call(
        functools.partial(_rwkv_body, rb=rb),
        grid=(b, s // rb),
        in_specs=[pl.BlockSpec((1, rb, RWKV_COLS), lambda bi, i: (bi, i, 0)),
                  full((16, GW)), full((RWKV_LORA, GW)), full((RWKV_LORA, GW)), full((RWKV_LORA, GW))],
        out_specs=pl.BlockSpec((1, rb, GW), lambda bi, i: (bi, i, 0)),
        out_shape=jax.ShapeDtypeStruct((b, s, GW), BF16),
        scratch_shapes=[pltpu.VMEM((GW, GW), F32), pltpu.VMEM((1, RWKV_COLS), F32)]
        + [pltpu.VMEM((rb, GW), F32)] * 10
        + [pltpu.VMEM((rb // CHUNK, GW, GW), BF16), pltpu.VMEM((rb // CHUNK, GW, GW), F32),
           pltpu.VMEM((rb // CHUNK, 8, GW), F32)],
        compiler_params=_cparams("parallel", "arbitrary"),
        name="rwkv",
    )(pc3, par, w2p, a2p, g2p)


def _lru_body(x_ref, g_ref, par_ref, wa_ref, wx_ref, o_ref, ext_ref, h_ref, *, rb):
    @pl.when(pl.program_id(1) == 0)
    def _():
        ext_ref[0:8, :] = jnp.zeros((8, GW), F32)
        h_ref[...] = jnp.zeros_like(h_ref)

    conv_b, ba, bx, lam = par_ref[4:5, :], par_ref[5:6, :], par_ref[6:7, :], par_ref[7:8, :]
    x = x_ref[0]
    ext_ref[8:8 + rb, :] = x
    conv = conv_b + par_ref[0:1, :] * ext_ref[pl.ds(8 - 3, rb), :]
    for j in range(1, LRU_CONV):
        conv = conv + par_ref[j:j + 1, :] * ext_ref[pl.ds(8 - 3 + j, rb), :]
    ext_ref[0:8, :] = x[rb - 8:rb, :]
    gate_r = _sigmoid(_dot(conv, wa_ref[...]) + ba)
    gate_i = _sigmoid(_dot(conv, wx_ref[...]) + bx)
    log_a = -LRU_C * gate_r * _softplus(-lam)
    a = jnp.exp(log_a)
    inp = jnp.sqrt(-_expm1(2.0 * log_a)) * (gate_i * conv)
    row = lax.broadcasted_iota(jnp.int32, (rb, 1), 0)
    d = 1
    while d < rb:
        m = row >= d
        inp = jnp.where(m, a * pltpu.roll(inp, d, 0) + inp, inp)
        a = jnp.where(m, a * pltpu.roll(a, d, 0), a)
        d *= 2
    h = inp + a * h_ref[...]
    h_ref[...] = h[rb - 1:rb, :]
    g = g_ref[0]
    gelu = 0.5 * g * (1.0 + jnp.tanh(math.sqrt(2.0 / math.pi) * (g + 0.044715 * (g * g * g))))
    o_ref[0] = (h * gelu).astype(BF16)


def _lru(pd3, par, wa_bd, wx_bd, rb):
    b, s, _ = pd3.shape
    full = lambda shp: pl.BlockSpec(shp, lambda bi, i: (0,) * len(shp))
    return pl.pallas_call(
        functools.partial(_lru_body, rb=rb),
        grid=(b, s // rb),
        in_specs=[pl.BlockSpec((1, rb, GW), lambda bi, i: (bi, i, 0)),
                  pl.BlockSpec((1, rb, GW), lambda bi, i: (bi, i, 1)),
                  full((8, GW)), full((GW, GW)), full((GW, GW))],
        out_specs=pl.BlockSpec((1, rb, GW), lambda bi, i: (bi, i, 0)),
        out_shape=jax.ShapeDtypeStruct((b, s, GW), BF16),
        scratch_shapes=[pltpu.VMEM((rb + 8, GW), F32), pltpu.VMEM((1, GW), F32)],
        compiler_params=_cparams("parallel", "arbitrary"),
        name="lru",
    )(pd3, pd3, par, wa_bd, wx_bd)


def _outproj_body(ya_ref, yb_ref, yc_ref, yd_ref, x_ref, w_ref, g_ref, o_ref):
    mix = jnp.concatenate([ya_ref[...], yb_ref[...], yc_ref[...], yd_ref[...]], axis=1)
    o_ref[...] = x_ref[...] + _rms(jnp.dot(mix, w_ref[...], preferred_element_type=F32), g_ref[...])


def _outproj(ya, yb, yc, yd, x2, w_all, layer, gain, tm):
    t, d = x2.shape
    mix = pl.BlockSpec((tm, GW), lambda i: (i, 0))
    return pl.pallas_call(
        _outproj_body,
        grid=(t // tm,),
        in_specs=[mix, mix, mix, mix,
                  pl.BlockSpec((tm, d), lambda i: (i, 0)),
                  pl.BlockSpec((None, 4 * GW, d), lambda i: (layer, 0, 0)),
                  pl.BlockSpec((1, d), lambda i: (0, 0))],
        out_specs=pl.BlockSpec((tm, d), lambda i: (i, 0)),
        out_shape=jax.ShapeDtypeStruct((t, d), F32),
        compiler_params=_cparams("parallel"),
        name="outproj",
    )(ya, yb, yc, yd, x2, w_all, gain)


def _mlp_body(x_ref, g1_ref, w1_ref, w2_ref, g2_ref, o_ref, h_ref, acc_ref):
    kk = pl.program_id(1)

    @pl.when(kk == 0)
    def _():
        h_ref[...] = _rms(x_ref[...], g1_ref[...]).astype(BF16)
        acc_ref[...] = jnp.zeros_like(acc_ref)

    a = jnp.dot(h_ref[...], w1_ref[...], preferred_element_type=F32)
    a = jnp.square(jnp.maximum(a, 0.0)).astype(BF16)
    acc_ref[...] += jnp.dot(a, w2_ref[...], preferred_element_type=F32)

    @pl.when(kk == pl.num_programs(1) - 1)
    def _():
        o_ref[...] = x_ref[...] + _rms(acc_ref[...], g2_ref[...])


def _mlp(x2, g1, w1_all, w2_all, layer, g2, tm, tk):
    t, d = x2.shape
    hid = w1_all.shape[2]
    return pl.pallas_call(
        _mlp_body,
        grid=(t // tm, hid // tk),
        in_specs=[pl.BlockSpec((tm, d), lambda i, k: (i, 0)),
                  pl.BlockSpec((1, d), lambda i, k: (0, 0)),
                  pl.BlockSpec((None, d, tk), lambda i, k: (layer, 0, k)),
                  pl.BlockSpec((None, tk, d), lambda i, k: (layer, k, 0)),
                  pl.BlockSpec((1, d), lambda i, k: (0, 0))],
        out_specs=pl.BlockSpec((tm, d), lambda i, k: (i, 0)),
        out_shape=jax.ShapeDtypeStruct((t, d), F32),
        scratch_shapes=[pltpu.VMEM((tm, d), BF16), pltpu.VMEM((tm, d), F32)],
        compiler_params=_cparams("parallel", "arbitrary"),
        name="mlp",
    )(x2, g1, w1_all, w2_all, g2)


def _rows(vectors, n_rows):
    tab = jnp.stack([v.astype(F32) for v in vectors])
    return jnp.pad(tab, ((0, n_rows - tab.shape[0]), (0, 0)))


def _block_diag(w):
    out = jnp.zeros((GW, GW), w.dtype)
    for n in range(w.shape[0]):
        out = out.at[n * HEAD_DIM:(n + 1) * HEAD_DIM, n * HEAD_DIM:(n + 1) * HEAD_DIM].set(w[n])
    return out


def _pick_block(n, want):
    while n % want:
        want //= 2
    return want


def kernel(x, norm_mix_pre, norm_mix_post, norm_mlp_pre, norm_mlp_post, w_in, w_out, attn_rel_bias, hgrn_lb_logits, hgrn_norm, rwkv_mu, rwkv_w0, rwkv_w2, rwkv_a0, rwkv_a2, rwkv_g2, rwkv_k_k, rwkv_k_a, rwkv_r_k, rwkv_ln_w, rwkv_ln_b, lru_conv_w, lru_conv_b, lru_wa, lru_ba, lru_wx, lru_bx, lru_lambda, mlp_w1, mlp_w2):
    b, s, d = x.shape
    depth = w_in.shape[0]
    t = b * s
    tm = _pick_block(t, 512)
    tm_mlp = _pick_block(t, 1024)
    rb = _pick_block(s, 512)
    tk = _pick_block(mlp_w1.shape[-1], 1024)

    lb_sm = jax.nn.softmax(hgrn_lb_logits.astype(F32), axis=0)
    lb_all = jnp.maximum(jnp.cumsum(lb_sm, axis=0) - lb_sm[0:1], 0.0)

    w_in_b, w_out_b = w_in.astype(BF16), w_out.astype(BF16)
    w1_b, w2_b = mlp_w1.astype(BF16), mlp_w2.astype(BF16)

    x2 = x.reshape(t, d)
    for l in range(depth):
        pa, pb, pc, pd = _inproj(x2, norm_mix_pre[l].reshape(1, d), w_in_b, l, tm)

        ya = _attention(pa.reshape(b, s, ATTN_COLS), _attn_bias_table(attn_rel_bias[l]), rb)

        lb = lb_all[l]
        hgrn_par = _rows([lb, jnp.log1p(-lb), 1.0 - lb, hgrn_norm[l]], 8)
        yb = _hgrn(pb.reshape(b, s, HGRN_COLS), hgrn_par, rb)

        mu = rwkv_mu[l].astype(F32)
        mu_lo = jnp.pad(mu[3 * GW:], (0, GW - RWKV_LORA))
        rwkv_par = _rows([mu[0:GW], mu[GW:2 * GW], mu[2 * GW:3 * GW], mu_lo, rwkv_w0[l], rwkv_a0[l],
                          rwkv_k_k[l], rwkv_k_a[l], rwkv_r_k[l].reshape(GW), rwkv_ln_w[l], rwkv_ln_b[l]], 16)
        zeros = lambda n: jnp.zeros((n, GW), F32)
        w2p = jnp.concatenate([rwkv_w2[l].astype(F32), zeros(96)], axis=0).astype(BF16)
        a2p = jnp.concatenate([zeros(32), rwkv_a2[l].astype(F32), zeros(64)], axis=0).astype(BF16)
        g2p = jnp.concatenate([zeros(64), rwkv_g2[l].astype(F32)], axis=0).astype(BF16)
        yc = _rwkv(pc.reshape(b, s, RWKV_COLS), rwkv_par, w2p, a2p, g2p, rb)

        cw = lru_conv_w[l].astype(F32)
        lru_par = _rows([cw[0], cw[1], cw[2], cw[3], lru_conv_b[l], lru_ba[l], lru_bx[l], lru_lambda[l]], 8)
        yd = _lru(pd.reshape(b, s, LRU_COLS), lru_par, _block_diag(lru_wa[l]).astype(BF16),
                  _block_diag(lru_wx[l]).astype(BF16), rb)

        x2 = _outproj(ya.reshape(t, GW), yb.reshape(t, GW), yc.reshape(t, GW), yd.reshape(t, GW),
                      x2, w_out_b, l, norm_mix_post[l].reshape(1, d), tm)
        x2 = _mlp(x2, norm_mlp_pre[l].reshape(1, d), w1_b, w2_b, l, norm_mlp_post[l].reshape(1, d), tm_mlp, tk)
    return x2.reshape(b, s, d)
```

```python
import functools
import math

import numpy as np

import jax
import jax.numpy as jnp
from jax import lax
from jax.experimental import pallas as pl
from jax.experimental.pallas import tpu as pltpu

F32 = jnp.float32
BF16 = jnp.bfloat16

GW = 256
HEAD_DIM = 64
N_HEADS = 4
CHUNK = 64
ATTN_LEFT_CHUNKS = 8
BAND = (ATTN_LEFT_CHUNKS + 1) * CHUNK
ATTN_PAD = ATTN_LEFT_CHUNKS * CHUNK
REL_CLIP = 256
ATTN_SCALE = HEAD_DIM ** -0.5
NEG_INF = -1e30
RMS_EPS = 1e-6
RWKV_GN_EPS = HEAD_DIM * 1e-5
RWKV_LORA = 128
LRU_C = 8.0
LRU_CONV = 4
HGRN_SUB = 16

ATTN_COLS = 3 * GW
HGRN_COLS = 4 * GW
RWKV_COLS = 3 * GW + RWKV_LORA
LRU_COLS = 2 * GW

VMEM_LIMIT = 56 * 1024 * 1024
RWKV_PASSES = (1, (3, 3, 3, 1, 1), 1)
RWKV_MAP_BATCH = 8
ATTN_BATCH = 8


def _cparams(*sem):
    return pltpu.CompilerParams(dimension_semantics=sem, vmem_limit_bytes=VMEM_LIMIT)


def _dot(a, b):
    return jnp.dot(a.astype(BF16), b.astype(BF16), preferred_element_type=F32)


def _dot_nt(a, b):
    return lax.dot_general(a.astype(BF16), b.astype(BF16), (((1,), (1,)), ((), ())),
                           preferred_element_type=F32)


def _dot_tn(a, b):
    return lax.dot_general(a.astype(BF16), b.astype(BF16), (((0,), (0,)), ((), ())),
                           preferred_element_type=F32)


def _split(a):
    hi = a.astype(BF16)
    lo = (a - hi.astype(F32)).astype(BF16)
    return hi, lo


def _mm(a, b, kind="nn", passes=1):
    f = {"nn": _dot, "nt": _dot_nt, "tn": _dot_tn}[kind]
    if passes == 1:
        return f(a, b)
    ah, al = _split(a)
    bh, bl = _split(b)
    if kind == "tn":
        return f(ah, bh) + (f(ah, bl) + f(al, bh))
    m = a.shape[0]
    top = f(jnp.concatenate([ah, al], axis=0), bh)
    return top[0:m] + (f(ah, bl) + top[m:])


def _rms(x, gain):
    return x * lax.rsqrt(jnp.mean(x * x, axis=-1, keepdims=True) + RMS_EPS) * gain


def _sigmoid(x):
    return 1.0 / (1.0 + jnp.exp(-x))


def _softplus(x):
    return jnp.maximum(x, 0.0) + jnp.log1p(jnp.exp(-jnp.abs(x)))


def _log1p_exp_neg_abs(x):
    return jnp.log(1.0 + jnp.exp(-jnp.abs(x)))


def _expm1(z):
    u = jnp.exp(z)
    um1 = u - 1.0
    near = um1 * z / jnp.where(u == 1.0, 1.0, jnp.log(u))
    return jnp.where(jnp.abs(z) > 0.5, um1, jnp.where(u == 1.0, z, near))


def _head_block_mask(n):
    r = lax.broadcasted_iota(jnp.int32, (n, n), 0) // HEAD_DIM
    c = lax.broadcasted_iota(jnp.int32, (n, n), 1) // HEAD_DIM
    return r == c


def _cumsum_rows(x):
    n = x.shape[0]
    row = lax.broadcasted_iota(jnp.int32, (n, 1), 0)
    d = 1
    while d < n:
        x = x + jnp.where(row >= d, pltpu.roll(x, d, 0), 0.0)
        d *= 2
    return x


def _stack_heads(x):
    lane_head = lax.broadcasted_iota(jnp.int32, (1, GW), 1) // HEAD_DIM
    return jnp.concatenate([jnp.where(lane_head == h, x, 0.0) for h in range(N_HEADS)], axis=0)


def _unstack_heads(xbd, c):
    return xbd[0:c] + xbd[c:2 * c] + xbd[2 * c:3 * c] + xbd[3 * c:4 * c]


def _inproj_body(x_ref, g_ref, w_ref, oa_ref, ob_ref, oc_ref, od_ref):
    h = _rms(x_ref[...], g_ref[...]).astype(BF16)
    o0, o1, o2 = ATTN_COLS, ATTN_COLS + HGRN_COLS, ATTN_COLS + HGRN_COLS + RWKV_COLS
    oa_ref[...] = jnp.dot(h, w_ref[:, 0:o0], preferred_element_type=F32).astype(BF16)
    ob_ref[...] = jnp.dot(h, w_ref[:, o0:o1], preferred_element_type=F32)
    oc_ref[...] = jnp.dot(h, w_ref[:, o1:o2], preferred_element_type=F32)
    od_ref[...] = jnp.dot(h, w_ref[:, o2:], preferred_element_type=F32)


def _inproj(x2, gain, w_all, layer, tm):
    t, d = x2.shape
    d_in = w_all.shape[2]
    return pl.pallas_call(
        _inproj_body,
        grid=(t // tm,),
        in_specs=[pl.BlockSpec((tm, d), lambda i: (i, 0)),
                  pl.BlockSpec((1, d), lambda i: (0, 0)),
                  pl.BlockSpec((None, d, d_in), lambda i: (layer, 0, 0))],
        out_specs=[pl.BlockSpec((tm, ATTN_COLS), lambda i: (i, 0)),
                   pl.BlockSpec((tm, HGRN_COLS), lambda i: (i, 0)),
                   pl.BlockSpec((tm, RWKV_COLS), lambda i: (i, 0)),
                   pl.BlockSpec((tm, LRU_COLS), lambda i: (i, 0))],
        out_shape=[jax.ShapeDtypeStruct((t, ATTN_COLS), BF16),
                   jax.ShapeDtypeStruct((t, HGRN_COLS), F32),
                   jax.ShapeDtypeStruct((t, RWKV_COLS), F32),
                   jax.ShapeDtypeStruct((t, LRU_COLS), F32)],
        compiler_params=_cparams("parallel"),
        name="inproj",
    )(x2, gain, w_all)


def _attn_body(q_ref, k_ref, v_ref, bias_ref, o_ref, kp_ref, vp_ref, *, chunks):
    i = pl.program_id(1)

    @pl.when(i == 0)
    def _():
        for src, dst in ((k_ref, kp_ref), (v_ref, vp_ref)):
            dst[0:ATTN_PAD, :] = jnp.zeros((ATTN_PAD, GW), BF16)
            dst[ATTN_PAD:, :] = src[0]

    lane_head = lax.broadcasted_iota(jnp.int32, (1, GW), 1) // HEAD_DIM
    kpos = lax.broadcasted_iota(jnp.int32, (1, BAND), 1)

    nb = ATTN_BATCH

    def chunk_group(jj, carry):
        cs = [i * chunks + jj * nb + n for n in range(nb)]
        rows = [pl.ds(pl.multiple_of((jj * nb + n) * CHUNK, CHUNK), CHUNK) for n in range(nb)]
        wins = [pl.ds(pl.multiple_of(c * CHUNK, CHUNK), BAND) for c in cs]
        ss = []
        for c, rw, win in zip(cs, rows, wins):
            q = q_ref[0, rw, :] * ATTN_SCALE
            qbd = jnp.concatenate([jnp.where(lane_head == h, q, jnp.zeros_like(q)) for h in range(N_HEADS)],
                                  axis=0)
            s = _dot_nt(qbd, kp_ref[win, :]) + bias_ref[...]
            ss.append(jnp.where(kpos >= (ATTN_LEFT_CHUNKS - c) * CHUNK, s, NEG_INF))
        ps = [jnp.exp(s - jnp.max(s, axis=-1, keepdims=True)) for s in ss]
        for rw, win, p in zip(rows, wins, ps):
            inv_l = 1.0 / jnp.sum(p, axis=-1, keepdims=True)
            obd = _dot(p, vp_ref[win, :]) * inv_l
            o = jnp.zeros((CHUNK, GW), F32)
            for h in range(N_HEADS):
                o = o + jnp.where(lane_head == h, obd[h * CHUNK:(h + 1) * CHUNK, :], 0.0)
            o_ref[0, rw, :] = o.astype(BF16)
        return carry

    lax.fori_loop(0, chunks // nb, chunk_group, 0)


def _attention(pa3, bias, rb):
    b, s, _ = pa3.shape
    return pl.pallas_call(
        functools.partial(_attn_body, chunks=rb // CHUNK),
        grid=(b, s // rb),
        in_specs=[pl.BlockSpec((1, rb, GW), lambda bi, i: (bi, i, 0)),
                  pl.BlockSpec((1, s, GW), lambda bi, i: (bi, 0, 1)),
                  pl.BlockSpec((1, s, GW), lambda bi, i: (bi, 0, 2)),
                  pl.BlockSpec((N_HEADS * CHUNK, BAND), lambda bi, i: (0, 0))],
        out_specs=pl.BlockSpec((1, rb, GW), lambda bi, i: (bi, i, 0)),
        out_shape=jax.ShapeDtypeStruct((b, s, GW), BF16),
        scratch_shapes=[pltpu.VMEM((s + ATTN_PAD, GW), BF16)] * 2,
        compiler_params=_cparams("parallel", "arbitrary"),
        name="attn",
    )(pa3, pa3, pa3, bias)


def _attn_bias_table(rel_bias):
    rel = np.arange(BAND + CHUNK - 1) - (CHUNK - 1) - ATTN_LEFT_CHUNKS * CHUNK
    ext = rel_bias.astype(F32)[:, np.clip(rel, -REL_CLIP, REL_CLIP) + REL_CLIP]
    tab = jnp.stack([ext[:, CHUNK - 1 - q:CHUNK - 1 - q + BAND] for q in range(CHUNK)], axis=1)
    return tab.reshape(N_HEADS * CHUNK, BAND)


def _hgrn_body(q_ref, f_ref, i_ref, g_ref, par_ref, o_ref, st_ref, oacc_ref, q_s, key_s, cum16_s, cum64_s,
               *, rb):
    @pl.when(pl.program_id(1) == 0)
    def _():
        st_ref[...] = jnp.zeros_like(st_ref)

    m = HGRN_SUB
    lb = par_ref[0:1, :]
    log_1m_lb = par_ref[1:2, :]
    one_m_lb = par_ref[2:3, :]
    gain = par_ref[3:4, :]
    bd = _head_block_mask(GW)
    ones_bd = jnp.where(bd, 1.0, 0.0).astype(BF16)
    trow = lax.broadcasted_iota(jnp.int32, (m, 1), 0)

    fr = f_ref[0]
    e = jnp.exp(-jnp.abs(fr))
    inv_1pe = 1.0 / (1.0 + e)
    sig = jnp.where(fr >= 0.0, inv_1pe, e * inv_1pe)
    log_sig = jnp.minimum(fr, 0.0) - jnp.log(1.0 + e)
    log_f = jnp.maximum(jnp.log(lb + one_m_lb * sig), log_1m_lb + log_sig)
    row = lax.broadcasted_iota(jnp.int32, (rb, 1), 0) % CHUNK
    cum, d = log_f, 1
    while d < CHUNK:
        cum = cum + jnp.where(row >= d, pltpu.roll(cum, d, 0), 0.0)
        d *= 2
    cum64_s[...] = cum
    before = jnp.where(row == 0, 0.0, pltpu.roll(cum, 1, 0)).reshape(rb // m, m, GW)[:, 0:1, :]
    cum16_s[...] = cum - jnp.broadcast_to(before, (rb // m, m, GW)).reshape(rb, GW)
    key_s[...] = one_m_lb * jnp.where(fr >= 0.0, e * inv_1pe, inv_1pe)
    qr = q_ref[0]
    q_s[...] = qr * _sigmoid(qr)

    nsub = CHUNK // m

    def chunk(j, carry):
        rows = pl.ds(pl.multiple_of(j * CHUNK, CHUNK), CHUNK)
        q, key, v = q_s[rows, :], key_s[rows, :], i_ref[0, rows, :]
        c16, c64 = cum16_s[rows, :], cum64_s[rows, :]
        sub = lambda x, i: x[i * m:(i + 1) * m, :]
        st = st_ref[...]
        o = _dot_nt(q * jnp.exp(c64), st)
        a_exp = []
        for i in range(nsub):
            qi, ki, ci = sub(q, i), sub(key, i), sub(c16, i)
            ps = []
            for s in range(m):
                dec = jnp.exp(jnp.where(trow >= s, ci - ci[s:s + 1, :], NEG_INF))
                ps.append(qi * dec * ki[s:s + 1, :])
            a_exp.append(_dot(jnp.concatenate(ps, axis=0), ones_bd))
        o_sub = []
        for i in range(nsub):
            vi, oi = sub(v, i), sub(o, i)
            for s in range(m):
                oi = oi + a_exp[i][s * m:(s + 1) * m, :] * vi[s:s + 1, :]
            o_sub.append(oi)
        q_hat = q * jnp.exp(c16)
        k_hat = [sub(key, i) * jnp.exp(sub(c16, i)[m - 1:m, :] - sub(c16, i)) for i in range(nsub)]
        b_start = [sub(c64, i)[0:1, :] - sub(c16, i)[0:1, :] for i in range(nsub)]
        b_end = [sub(c64, i)[m - 1:m, :] for i in range(nsub)]
        v_bd = [_stack_heads(sub(v, i)) for i in range(nsub)]
        for i in range(1, nsub):
            k_bd = jnp.concatenate([_stack_heads(k_hat[jj] * jnp.exp(b_start[i] - b_end[jj])) for jj in range(i)],
                                   axis=0)
            a_cat = _dot_nt(sub(q_hat, i), k_bd)
            o_sub[i] = o_sub[i] + _dot(a_cat, jnp.concatenate(v_bd[0:i], axis=0))
        oacc_ref[rows, :] = jnp.concatenate(o_sub, axis=0)
        last = c64[CHUNK - 1:CHUNK, :]
        st_ref[...] = st * jnp.exp(last) + jnp.where(bd, _dot_tn(v, key * jnp.exp(last - c64)), 0.0)
        return carry

    lax.fori_loop(0, rb // CHUNK, chunk, 0)
    o = oacc_ref[...]
    ms = _dot(o * o, ones_bd) * (1.0 / HEAD_DIM)
    g = g_ref[0]
    o_ref[0] = (o * lax.rsqrt(ms + RMS_EPS) * gain * (g * _sigmoid(g))).astype(BF16)


def _hgrn(pb3, par, rb):
    b, s, _ = pb3.shape
    col = lambda n: pl.BlockSpec((1, rb, GW), lambda bi, i, n=n: (bi, i, n))
    return pl.pallas_call(
        functools.partial(_hgrn_body, rb=rb),
        grid=(b, s // rb),
        in_specs=[col(0), col(1), col(2), col(3),
                  pl.BlockSpec((8, GW), lambda bi, i: (0, 0))],
        out_specs=pl.BlockSpec((1, rb, GW), lambda bi, i: (bi, i, 0)),
        out_shape=jax.ShapeDtypeStruct((b, s, GW), BF16),
        scratch_shapes=[pltpu.VMEM((GW, GW), F32)] + [pltpu.VMEM((rb, GW), F32)] * 5,
        compiler_params=_cparams("parallel", "arbitrary"),
        name="hgrn",
    )(pb3, pb3, pb3, pb3, par)


def _expand_heads(x_cat, bd):
    return jnp.where(bd, jnp.concatenate([x_cat] * N_HEADS, axis=0), 0.0)


def _mm_cat(a, x_cat, bd, passes):
    if passes == 1:
        return _dot(a, _expand_heads(x_cat.astype(BF16), bd))
    ah, al = _split(a)
    xh, xl = _split(x_cat)
    m = a.shape[0]
    top = _dot(jnp.concatenate([ah, al], axis=0), _expand_heads(xh, bd))
    return top[0:m] + (_dot(ah, _expand_heads(xl, bd)) + top[m:])


def _rwkv_chunk_maps(chunks, bd, strict_cat, incl_cat, eye_cat):
    c = CHUNK
    p_pair, p_inv, p_app = RWKV_PASSES
    pre = []
    for r, logw, k, v, alpha, beta in chunks:
        cum = _cumsum_rows(logw)
        e_last = jnp.exp(cum[c - 1:c, :])
        a_t = alpha * jnp.exp(cum - logw)
        r_t = r * jnp.exp(cum)
        e_inv = jnp.exp(-cum)
        b_t = beta * e_inv
        k_t = k * e_inv
        ar = jnp.concatenate([a_t, r_t], axis=0)
        bbd, kbd, vbd = _stack_heads(b_t), _stack_heads(k_t), _stack_heads(v)
        pair = _mm(ar, jnp.concatenate([bbd, kbd], axis=0), "nt", p_pair)
        l_ab = jnp.where(strict_cat, pair[0:c, 0:GW], 0.0)
        lm_k = jnp.where(jnp.concatenate([strict_cat, incl_cat], axis=0), pair[:, GW:], 0.0)
        m_rb = jnp.where(incl_cat, pair[c:, 0:GW], 0.0)
        wy0 = _mm(lm_k, vbd, "nn", p_app)
        pre.append((e_last, a_t, r_t, b_t, k_t, v, l_ab, m_rb, wy0))
    xs = [p[6] for p in pre]
    invs = [eye_cat + x for x in xs]
    xs = [_mm_cat(x, x, bd, p_inv[0]) for x in xs]
    for n in range(len(p_inv)):
        last = n == len(p_inv) - 1
        p_step = p_inv[n] if last else max(p_inv[n], p_inv[n + 1])
        if last:
            invs = [inv + _mm_cat(inv, x, bd, p_step) for inv, x in zip(invs, xs)]
        else:
            both = [_mm_cat(jnp.concatenate([inv, x], axis=0), x, bd, p_step) for inv, x in zip(invs, xs)]
            invs = [inv + bo[0:c] for inv, bo in zip(invs, both)]
            xs = [bo[c:] for bo in both]
    outs = []
    for (e_last, a_t, r_t, b_t, k_t, v, l_ab, m_rb, wy0), inv in zip(pre, invs):
        sol = _mm(inv, jnp.concatenate([_stack_heads(a_t), _stack_heads(wy0[0:c])], axis=1), "nn", p_app)
        a_s, u0 = sol[:, 0:GW], sol[:, GW:]
        out = _mm(m_rb, jnp.concatenate([_stack_heads(a_s), _stack_heads(u0)], axis=1), "nn", p_app)
        ry = r_t + out[:, 0:GW]
        y0 = wy0[c:] + out[:, GW:]
        bh, kh = b_t * e_last, k_t * e_last
        g = jnp.where(bd, _mm(a_s, bh, "tn", p_app), 0.0)
        c0 = jnp.where(bd, _mm(jnp.concatenate([u0, v], axis=0), jnp.concatenate([bh, kh], axis=0), "tn", p_app),
                       0.0)
        outs.append((ry, y0, g, c0, e_last))
    return outs


def _rwkv_body(pc_ref, par_ref, w2_ref, a2_ref, g2_ref, o_ref, ht_ref, prev_ref,
               r_s, w_s, k_s, v_s, al_s, be_s, ry_s, y_s, bon_s, gate_s, g_s, c_s, el_s, *, rb):
    @pl.when(pl.program_id(1) == 0)
    def _():
        ht_ref[...] = jnp.zeros_like(ht_ref)
        prev_ref[...] = jnp.zeros_like(prev_ref)

    c = CHUNK
    mu_main = [par_ref[n:n + 1, :] for n in range(3)]
    mu_lo = par_ref[3:4, 0:RWKV_LORA]
    w0, a0 = par_ref[4:5, :], par_ref[5:6, :]
    k_k, k_a, r_k = par_ref[6:7, :], par_ref[7:8, :], par_ref[8:9, :]
    ln_w, ln_b = par_ref[9:10, :], par_ref[10:11, :]
    bd = _head_block_mask(GW)
    ones_bd = jnp.where(bd, 1.0, 0.0).astype(BF16)

    pc = pc_ref[0]
    row0 = lax.broadcasted_iota(jnp.int32, (rb, 1), 0) == 0
    prev = jnp.where(row0, prev_ref[...], pltpu.roll(pc, 1, 0))
    prev_ref[...] = pc[rb - 1:rb, :]
    xs = [pc[:, n * GW:(n + 1) * GW] for n in range(3)]
    ps = [prev[:, n * GW:(n + 1) * GW] for n in range(3)]
    r, k, v = [x + mu * (p - x) for x, p, mu in zip(xs, ps, mu_main)]
    lo, plo = pc[:, 3 * GW:], prev[:, 3 * GW:]
    lo = lo + mu_lo * (plo - lo)
    w_in = -(w0 + _dot(jnp.tanh(lo), w2_ref[...]))
    w_pre = -(jnp.maximum(w_in, 0.0) + _log1p_exp_neg_abs(w_in)) - 0.5
    a = _sigmoid(a0 + _dot(lo, a2_ref[...]))
    g = _dot(_sigmoid(lo), g2_ref[...])
    kk = k * k_k
    kk = kk / jnp.maximum(jnp.sqrt(_dot(kk * kk, ones_bd)), 1e-12)
    k = k * (1.0 + (a - 1.0) * k_a)
    r_s[...] = r
    w_s[...] = -jnp.exp(w_pre)
    k_s[...] = k
    v_s[...] = v
    al_s[...] = -kk
    be_s[...] = kk * a
    bon_s[...] = _dot(r * k * r_k, ones_bd) * v
    gate_s[...] = g

    tt = lax.broadcasted_iota(jnp.int32, (c, GW), 0)
    ss = lax.broadcasted_iota(jnp.int32, (c, GW), 1) % c
    strict_cat, incl_cat = tt > ss, tt >= ss
    eye_cat = jnp.where(tt == ss, 1.0, 0.0)

    nb = RWKV_MAP_BATCH

    def chunk_maps(jj, carry):
        js = [jj * nb + n for n in range(nb)]
        rows = [pl.ds(pl.multiple_of(j * c, c), c) for j in js]
        maps = _rwkv_chunk_maps([(r_s[rw, :], w_s[rw, :], k_s[rw, :], v_s[rw, :], al_s[rw, :], be_s[rw, :])
                                 for rw in rows], bd, strict_cat, incl_cat, eye_cat)
        for j, rw, (ry, y0, g, c0, e_last) in zip(js, rows, maps):
            ry_s[rw, :] = ry
            y_s[rw, :] = y0
            g_s[j] = g.astype(BF16)
            c_s[j] = c0
            el_s[j] = jnp.broadcast_to(e_last, (8, GW))
        return carry

    lax.fori_loop(0, rb // (c * nb), chunk_maps, 0)

    def chunk_apply(j, carry):
        rows = pl.ds(pl.multiple_of(j * c, c), c)
        ht = ht_ref[...]
        htb = ht.astype(BF16)
        y_s[rows, :] = y_s[rows, :] + _dot_nt(ry_s[rows, :], htb)
        ht_ref[...] = ht * el_s[j][0:1, :] + jnp.dot(htb, g_s[j], preferred_element_type=F32) + c_s[j]
        return carry

    lax.fori_loop(0, rb // c, chunk_apply, 0, unroll=2)

    y = y_s[...]
    mean = _dot(y, ones_bd) * (1.0 / HEAD_DIM)
    yc = y - mean
    var = _dot(yc * yc, ones_bd) * (1.0 / HEAD_DIM)
    yn = yc * lax.rsqrt(var + RWKV_GN_EPS) * ln_w + ln_b
    o_ref[0] = ((yn + bon_s[...]) * gate_s[...]).astype(BF16)


def _rwkv(pc3, par, w2p, a2p, g2p, rb):
    b, s, _ = pc3.shape
    full = lambda shp: pl.BlockSpec(shp, lambda bi, i: (0,) * len(shp))
    return pl.pallas_call(
        functools.partial(_rwkv_body, rb=rb),
        grid=(b, s // rb),
        in_specs=[pl.BlockSpec((1, rb, RWKV_COLS), lambda bi, i: (bi, i, 0)),
                  full((16, GW)), full((RWKV_LORA, GW)), full((RWKV_LORA, GW)), full((RWKV_LORA, GW))],
        out_specs=pl.BlockSpec((1, rb, GW), lambda bi, i: (bi, i, 0)),
        out_shape=jax.ShapeDtypeStruct((b, s, GW), BF16),
        scratch_shapes=[pltpu.VMEM((GW, GW), F32), pltpu.VMEM((1, RWKV_COLS), F32)]
        + [pltpu.VMEM((rb, GW), F32)] * 10
        + [pltpu.VMEM((rb // CHUNK, GW, GW), BF16), pltpu.VMEM((rb // CHUNK, GW, GW), F32),
           pltpu.VMEM((rb // CHUNK, 8, GW), F32)],
        compiler_params=_cparams("parallel", "arbitrary"),
        name="rwkv",
    )(pc3, par, w2p, a2p, g2p)


def _lru_body(x_ref, g_ref, par_ref, wa_ref, wx_ref, o_ref, ext_ref, h_ref, *, rb):
    @pl.when(pl.program_id(1) == 0)
    def _():
        ext_ref[0:8, :] = jnp.zeros((8, GW), F32)
        h_ref[...] = jnp.zeros_like(h_ref)

    conv_b, ba, bx, lam = par_ref[4:5, :], par_ref[5:6, :], par_ref[6:7, :], par_ref[7:8, :]
    x = x_ref[0]
    ext_ref[8:8 + rb, :] = x
    conv = conv_b + par_ref[0:1, :] * ext_ref[pl.ds(8 - 3, rb), :]
    for j in range(1, LRU_CONV):
        conv = conv + par_ref[j:j + 1, :] * ext_ref[pl.ds(8 - 3 + j, rb), :]
    ext_ref[0:8, :] = x[rb - 8:rb, :]
    gate_r = _sigmoid(_dot(conv, wa_ref[...]) + ba)
    gate_i = _sigmoid(_dot(conv, wx_ref[...]) + bx)
    log_a = -LRU_C * gate_r * _softplus(-lam)
    a = jnp.exp(log_a)
    inp = jnp.sqrt(-_expm1(2.0 * log_a)) * (gate_i * conv)
    row = lax.broadcasted_iota(jnp.int32, (rb, 1), 0) % CHUNK
    d = 1
    while d < CHUNK:
        m = row >= d
        inp = jnp.where(m, a * pltpu.roll(inp, d, 0) + inp, inp)
        a = jnp.where(m, a * pltpu.roll(a, d, 0), a)
        d *= 2
    h_prev, hs = h_ref[...], []
    for n in range(rb // CHUNK):
        seg = slice(n * CHUNK, (n + 1) * CHUNK)
        hs.append(inp[seg] + a[seg] * h_prev)
        h_prev = hs[-1][CHUNK - 1:CHUNK, :]
    h = jnp.concatenate(hs, axis=0)
    h_ref[...] = h_prev
    g = g_ref[0]
    gelu = 0.5 * g * (1.0 + jnp.tanh(math.sqrt(2.0 / math.pi) * (g + 0.044715 * (g * g * g))))
    o_ref[0] = (h * gelu).astype(BF16)


def _lru(pd3, par, wa_bd, wx_bd, rb):
    b, s, _ = pd3.shape
    full = lambda shp: pl.BlockSpec(shp, lambda bi, i: (0,) * len(shp))
    return pl.pallas_call(
        functools.partial(_lru_body, rb=rb),
        grid=(b, s // rb),
        in_specs=[pl.BlockSpec((1, rb, GW), lambda bi, i: (bi, i, 0)),
                  pl.BlockSpec((1, rb, GW), lambda bi, i: (bi, i, 1)),
                  full((8, GW)), full((GW, GW)), full((GW, GW))],
        out_specs=pl.BlockSpec((1, rb, GW), lambda bi, i: (bi, i, 0)),
        out_shape=jax.ShapeDtypeStruct((b, s, GW), BF16),
        scratch_shapes=[pltpu.VMEM((rb + 8, GW), F32), pltpu.VMEM((1, GW), F32)],
        compiler_params=_cparams("parallel", "arbitrary"),
        name="lru",
    )(pd3, pd3, par, wa_bd, wx_bd)


def _outmlp_body(ya_ref, yb_ref, yc_ref, yd_ref, x_ref, wo_ref, gains_ref, w1_ref, w2_ref, o_ref,
                 xm_ref, h_ref, acc_ref):
    kk = pl.program_id(1)

    @pl.when(kk == 0)
    def _():
        mix = jnp.concatenate([ya_ref[...], yb_ref[...], yc_ref[...], yd_ref[...]], axis=1)
        xm = x_ref[...] + _rms(jnp.dot(mix, wo_ref[...], preferred_element_type=F32), gains_ref[0:1, :])
        xm_ref[...] = xm
        h_ref[...] = _rms(xm, gains_ref[1:2, :]).astype(BF16)
        acc_ref[...] = jnp.zeros_like(acc_ref)

    a = jnp.dot(h_ref[...], w1_ref[...], preferred_element_type=F32)
    a = jnp.square(jnp.maximum(a, 0.0)).astype(BF16)
    acc_ref[...] += jnp.dot(a, w2_ref[...], preferred_element_type=F32)

    @pl.when(kk == pl.num_programs(1) - 1)
    def _():
        o_ref[...] = xm_ref[...] + _rms(acc_ref[...], gains_ref[2:3, :])


def _outmlp(ya, yb, yc, yd, x2, wo_all, gains, w1_all, w2_all, layer, tm, tk):
    t, d = x2.shape
    hid = w1_all.shape[2]
    mix = pl.BlockSpec((tm, GW), lambda i, k: (i, 0))
    return pl.pallas_call(
        _outmlp_body,
        grid=(t // tm, hid // tk),
        in_specs=[mix, mix, mix, mix,
                  pl.BlockSpec((tm, d), lambda i, k: (i, 0)),
                  pl.BlockSpec((None, 4 * GW, d), lambda i, k: (layer, 0, 0)),
                  pl.BlockSpec((8, d), lambda i, k: (0, 0)),
                  pl.BlockSpec((None, d, tk), lambda i, k: (layer, 0, k)),
                  pl.BlockSpec((None, tk, d), lambda i, k: (layer, k, 0))],
        out_specs=pl.BlockSpec((tm, d), lambda i, k: (i, 0)),
        out_shape=jax.ShapeDtypeStruct((t, d), F32),
        scratch_shapes=[pltpu.VMEM((tm, d), F32), pltpu.VMEM((tm, d), BF16), pltpu.VMEM((tm, d), F32)],
        compiler_params=_cparams("parallel", "arbitrary"),
        name="outmlp",
    )(ya, yb, yc, yd, x2, wo_all, gains, w1_all, w2_all)


def _rows(vectors, n_rows):
    tab = jnp.stack([v.astype(F32) for v in vectors])
    return jnp.pad(tab, ((0, n_rows - tab.shape[0]), (0, 0)))


def _block_diag(w):
    out = jnp.zeros((GW, GW), w.dtype)
    for n in range(w.shape[0]):
        out = out.at[n * HEAD_DIM:(n + 1) * HEAD_DIM, n * HEAD_DIM:(n + 1) * HEAD_DIM].set(w[n])
    return out


def _pick_block(n, want):
    while n % want:
        want //= 2
    return want


def kernel(x, norm_mix_pre, norm_mix_post, norm_mlp_pre, norm_mlp_post, w_in, w_out, attn_rel_bias, hgrn_lb_logits, hgrn_norm, rwkv_mu, rwkv_w0, rwkv_w2, rwkv_a0, rwkv_a2, rwkv_g2, rwkv_k_k, rwkv_k_a, rwkv_r_k, rwkv_ln_w, rwkv_ln_b, lru_conv_w, lru_conv_b, lru_wa, lru_ba, lru_wx, lru_bx, lru_lambda, mlp_w1, mlp_w2):
    b, s, d = x.shape
    depth = w_in.shape[0]
    t = b * s
    tm = _pick_block(t, 512)
    tm_mlp = _pick_block(t, 1024)
    rb = _pick_block(s, 512)
    tk = _pick_block(mlp_w1.shape[-1], 1024)

    lb_sm = jax.nn.softmax(hgrn_lb_logits.astype(F32), axis=0)
    lb_all = jnp.maximum(jnp.cumsum(lb_sm, axis=0) - lb_sm[0:1], 0.0)

    w_in_b, w_out_b = w_in.astype(BF16), w_out.astype(BF16)
    w1_b, w2_b = mlp_w1.astype(BF16), mlp_w2.astype(BF16)

    x2 = x.reshape(t, d)
    for l in range(depth):
        pa, pb, pc, pd = _inproj(x2, norm_mix_pre[l].reshape(1, d), w_in_b, l, tm)

        ya = _attention(pa.reshape(b, s, ATTN_COLS), _attn_bias_table(attn_rel_bias[l]), rb)

        lb = lb_all[l]
        hgrn_par = _rows([lb, jnp.log1p(-lb), 1.0 - lb, hgrn_norm[l]], 8)
        yb = _hgrn(pb.reshape(b, s, HGRN_COLS), hgrn_par, rb)

        mu = rwkv_mu[l].astype(F32)
        mu_lo = jnp.pad(mu[3 * GW:], (0, GW - RWKV_LORA))
        rwkv_par = _rows([mu[0:GW], mu[GW:2 * GW], mu[2 * GW:3 * GW], mu_lo, rwkv_w0[l], rwkv_a0[l],
                          rwkv_k_k[l], rwkv_k_a[l], rwkv_r_k[l].reshape(GW), rwkv_ln_w[l], rwkv_ln_b[l]], 16)
        zeros = lambda n: jnp.zeros((n, GW), F32)
        w2p = jnp.concatenate([rwkv_w2[l].astype(F32), zeros(96)], axis=0).astype(BF16)
        a2p = jnp.concatenate([zeros(32), rwkv_a2[l].astype(F32), zeros(64)], axis=0).astype(BF16)
        g2p = jnp.concatenate([zeros(64), rwkv_g2[l].astype(F32)], axis=0).astype(BF16)
        yc = _rwkv(pc.reshape(b, s, RWKV_COLS), rwkv_par, w2p, a2p, g2p, rb)

        cw = lru_conv_w[l].astype(F32)
        lru_par = _rows([cw[0], cw[1], cw[2], cw[3], lru_conv_b[l], lru_ba[l], lru_bx[l], lru_lambda[l]], 8)
        yd = _lru(pd.reshape(b, s, LRU_COLS), lru_par, _block_diag(lru_wa[l]).astype(BF16),
                  _block_diag(lru_wx[l]).astype(BF16), rb)

        gains = jnp.pad(jnp.stack([norm_mix_post[l], norm_mlp_pre[l], norm_mlp_post[l]]).astype(F32),
                        ((0, 5), (0, 0)))
        x2 = _outmlp(ya.reshape(t, GW), yb.reshape(t, GW), yc.reshape(t, GW), yd.reshape(t, GW),
                     x2, w_out_b, gains, w1_b, w2_b, l, tm_mlp, tk)
    return x2.reshape(b, s, d)
```

```python
import functools
import math

import numpy as np

import jax
import jax.numpy as jnp
from jax import lax
from jax.experimental import pallas as pl
from jax.experimental.pallas import tpu as pltpu

F32 = jnp.float32
BF16 = jnp.bfloat16

GW = 256
HEAD_DIM = 64
N_HEADS = 4
CHUNK = 64
ATTN_LEFT_CHUNKS = 8
BAND = (ATTN_LEFT_CHUNKS + 1) * CHUNK
ATTN_PAD = ATTN_LEFT_CHUNKS * CHUNK
REL_CLIP = 256
ATTN_SCALE = HEAD_DIM ** -0.5
NEG_INF = -1e30
RMS_EPS = 1e-6
RWKV_GN_EPS = HEAD_DIM * 1e-5
RWKV_LORA = 128
LRU_C = 8.0
LRU_CONV = 4
HGRN_SUB = 16

ATTN_COLS = 3 * GW
HGRN_COLS = 4 * GW
RWKV_COLS = 3 * GW + RWKV_LORA
LRU_COLS = 2 * GW

VMEM_LIMIT = 56 * 1024 * 1024
RWKV_PASSES = (1, (3, 3, 3, 1, 1), 1)
RWKV_MAP_BATCH = 8
ATTN_BATCH = 8


def _cparams(*sem):
    return pltpu.CompilerParams(dimension_semantics=sem, vmem_limit_bytes=VMEM_LIMIT)


def _dot(a, b):
    return jnp.dot(a.astype(BF16), b.astype(BF16), preferred_element_type=F32)


def _dot_nt(a, b):
    return lax.dot_general(a.astype(BF16), b.astype(BF16), (((1,), (1,)), ((), ())),
                           preferred_element_type=F32)


def _dot_tn(a, b):
    return lax.dot_general(a.astype(BF16), b.astype(BF16), (((0,), (0,)), ((), ())),
                           preferred_element_type=F32)


def _split(a):
    hi = a.astype(BF16)
    lo = (a - hi.astype(F32)).astype(BF16)
    return hi, lo


def _mm(a, b, kind="nn", passes=1):
    f = {"nn": _dot, "nt": _dot_nt, "tn": _dot_tn}[kind]
    if passes == 1:
        return f(a, b)
    ah, al = _split(a)
    bh, bl = _split(b)
    if kind == "tn":
        return f(ah, bh) + (f(ah, bl) + f(al, bh))
    m = a.shape[0]
    top = f(jnp.concatenate([ah, al], axis=0), bh)
    return top[0:m] + (f(ah, bl) + top[m:])


def _rms(x, gain):
    return x * lax.rsqrt(jnp.mean(x * x, axis=-1, keepdims=True) + RMS_EPS) * gain


def _sigmoid(x):
    return 1.0 / (1.0 + jnp.exp(-x))


def _softplus(x):
    return jnp.maximum(x, 0.0) + jnp.log1p(jnp.exp(-jnp.abs(x)))


def _log1p_exp_neg_abs(x):
    return jnp.log(1.0 + jnp.exp(-jnp.abs(x)))


def _expm1(z):
    u = jnp.exp(z)
    um1 = u - 1.0
    near = um1 * z / jnp.where(u == 1.0, 1.0, jnp.log(u))
    return jnp.where(jnp.abs(z) > 0.5, um1, jnp.where(u == 1.0, z, near))


def _head_block_mask(n):
    r = lax.broadcasted_iota(jnp.int32, (n, n), 0) // HEAD_DIM
    c = lax.broadcasted_iota(jnp.int32, (n, n), 1) // HEAD_DIM
    return r == c


def _cumsum_rows(x):
    n = x.shape[0]
    row = lax.broadcasted_iota(jnp.int32, (n, 1), 0)
    d = 1
    while d < n:
        x = x + jnp.where(row >= d, pltpu.roll(x, d, 0), 0.0)
        d *= 2
    return x


def _stack_heads(x):
    lane_head = lax.broadcasted_iota(jnp.int32, (1, GW), 1) // HEAD_DIM
    return jnp.concatenate([jnp.where(lane_head == h, x, 0.0) for h in range(N_HEADS)], axis=0)


def _unstack_heads(xbd, c):
    return xbd[0:c] + xbd[c:2 * c] + xbd[2 * c:3 * c] + xbd[3 * c:4 * c]


def _inproj_body(x_ref, g_ref, w_ref, oa_ref, ob_ref, oc_ref, od_ref):
    h = _rms(x_ref[...], g_ref[...]).astype(BF16)
    o0, o1, o2 = ATTN_COLS, ATTN_COLS + HGRN_COLS, ATTN_COLS + HGRN_COLS + RWKV_COLS
    oa_ref[...] = jnp.dot(h, w_ref[:, 0:o0], preferred_element_type=F32).astype(BF16)
    ob_ref[...] = jnp.dot(h, w_ref[:, o0:o1], preferred_element_type=F32)
    oc_ref[...] = jnp.dot(h, w_ref[:, o1:o2], preferred_element_type=F32)
    od_ref[...] = jnp.dot(h, w_ref[:, o2:], preferred_element_type=F32)


def _inproj(x2, gain, w_all, layer, tm):
    t, d = x2.shape
    d_in = w_all.shape[2]
    return pl.pallas_call(
        _inproj_body,
        grid=(t // tm,),
        in_specs=[pl.BlockSpec((tm, d), lambda i: (i, 0)),
                  pl.BlockSpec((1, d), lambda i: (0, 0)),
                  pl.BlockSpec((None, d, d_in), lambda i: (layer, 0, 0))],
        out_specs=[pl.BlockSpec((tm, ATTN_COLS), lambda i: (i, 0)),
                   pl.BlockSpec((tm, HGRN_COLS), lambda i: (i, 0)),
                   pl.BlockSpec((tm, RWKV_COLS), lambda i: (i, 0)),
                   pl.BlockSpec((tm, LRU_COLS), lambda i: (i, 0))],
        out_shape=[jax.ShapeDtypeStruct((t, ATTN_COLS), BF16),
                   jax.ShapeDtypeStruct((t, HGRN_COLS), F32),
                   jax.ShapeDtypeStruct((t, RWKV_COLS), F32),
                   jax.ShapeDtypeStruct((t, LRU_COLS), F32)],
        compiler_params=_cparams("parallel"),
        name="inproj",
    )(x2, gain, w_all)


def _attn_body(q_ref, k_ref, v_ref, bias_ref, o_ref, kp_ref, vp_ref, *, chunks):
    i = pl.program_id(1)

    @pl.when(i == 0)
    def _():
        for src, dst in ((k_ref, kp_ref), (v_ref, vp_ref)):
            dst[0:ATTN_PAD, :] = jnp.zeros((ATTN_PAD, GW), BF16)
            dst[ATTN_PAD:, :] = src[0]

    lane_head = lax.broadcasted_iota(jnp.int32, (1, GW), 1) // HEAD_DIM
    kpos = lax.broadcasted_iota(jnp.int32, (1, BAND), 1)

    nb = ATTN_BATCH

    def chunk_group(jj, carry):
        cs = [i * chunks + jj * nb + n for n in range(nb)]
        rows = [pl.ds(pl.multiple_of((jj * nb + n) * CHUNK, CHUNK), CHUNK) for n in range(nb)]
        wins = [pl.ds(pl.multiple_of(c * CHUNK, CHUNK), BAND) for c in cs]
        ss = []
        for c, rw, win in zip(cs, rows, wins):
            q = q_ref[0, rw, :] * ATTN_SCALE
            qbd = jnp.concatenate([jnp.where(lane_head == h, q, jnp.zeros_like(q)) for h in range(N_HEADS)],
                                  axis=0)
            s = _dot_nt(qbd, kp_ref[win, :]) + bias_ref[...]
            ss.append(jnp.where(kpos >= (ATTN_LEFT_CHUNKS - c) * CHUNK, s, NEG_INF))
        ps = [jnp.exp(s - jnp.max(s, axis=-1, keepdims=True)) for s in ss]
        for rw, win, p in zip(rows, wins, ps):
            inv_l = 1.0 / jnp.sum(p, axis=-1, keepdims=True)
            obd = _dot(p, vp_ref[win, :]) * inv_l
            o = jnp.zeros((CHUNK, GW), F32)
            for h in range(N_HEADS):
                o = o + jnp.where(lane_head == h, obd[h * CHUNK:(h + 1) * CHUNK, :], 0.0)
            o_ref[0, rw, :] = o.astype(BF16)
        return carry

    lax.fori_loop(0, chunks // nb, chunk_group, 0)


def _attention(pa3, bias, rb):
    b, s, _ = pa3.shape
    return pl.pallas_call(
        functools.partial(_attn_body, chunks=rb // CHUNK),
        grid=(b, s // rb),
        in_specs=[pl.BlockSpec((1, rb, GW), lambda bi, i: (bi, i, 0)),
                  pl.BlockSpec((1, s, GW), lambda bi, i: (bi, 0, 1)),
                  pl.BlockSpec((1, s, GW), lambda bi, i: (bi, 0, 2)),
                  pl.BlockSpec((N_HEADS * CHUNK, BAND), lambda bi, i: (0, 0))],
        out_specs=pl.BlockSpec((1, rb, GW), lambda bi, i: (bi, i, 0)),
        out_shape=jax.ShapeDtypeStruct((b, s, GW), BF16),
        scratch_shapes=[pltpu.VMEM((s + ATTN_PAD, GW), BF16)] * 2,
        compiler_params=_cparams("parallel", "arbitrary"),
        name="attn",
    )(pa3, pa3, pa3, bias)


def _attn_bias_table(rel_bias):
    rel = np.arange(BAND + CHUNK - 1) - (CHUNK - 1) - ATTN_LEFT_CHUNKS * CHUNK
    ext = rel_bias.astype(F32)[:, np.clip(rel, -REL_CLIP, REL_CLIP) + REL_CLIP]
    tab = jnp.stack([ext[:, CHUNK - 1 - q:CHUNK - 1 - q + BAND] for q in range(CHUNK)], axis=1)
    return tab.reshape(N_HEADS * CHUNK, BAND)


def _hgrn_body(q_ref, f_ref, i_ref, g_ref, par_ref, o_ref, st_ref, oacc_ref, q_s, key_s, cum16_s, cum64_s,
               *, rb):
    @pl.when(pl.program_id(1) == 0)
    def _():
        st_ref[...] = jnp.zeros_like(st_ref)

    m = HGRN_SUB
    lb = par_ref[0:1, :]
    log_1m_lb = par_ref[1:2, :]
    one_m_lb = par_ref[2:3, :]
    gain = par_ref[3:4, :]
    bd = _head_block_mask(GW)
    ones_bd = jnp.where(bd, 1.0, 0.0).astype(BF16)
    trow = lax.broadcasted_iota(jnp.int32, (m, 1), 0)

    fr = f_ref[0]
    e = jnp.exp(-jnp.abs(fr))
    inv_1pe = 1.0 / (1.0 + e)
    sig = jnp.where(fr >= 0.0, inv_1pe, e * inv_1pe)
    log_sig = jnp.minimum(fr, 0.0) - jnp.log(1.0 + e)
    log_f = jnp.maximum(jnp.log(lb + one_m_lb * sig), log_1m_lb + log_sig)
    row = lax.broadcasted_iota(jnp.int32, (rb, 1), 0) % CHUNK
    cum, d = log_f, 1
    while d < CHUNK:
        cum = cum + jnp.where(row >= d, pltpu.roll(cum, d, 0), 0.0)
        d *= 2
    cum64_s[...] = cum
    before = jnp.where(row == 0, 0.0, pltpu.roll(cum, 1, 0)).reshape(rb // m, m, GW)[:, 0:1, :]
    cum16_s[...] = cum - jnp.broadcast_to(before, (rb // m, m, GW)).reshape(rb, GW)
    key_s[...] = one_m_lb * jnp.where(fr >= 0.0, e * inv_1pe, inv_1pe)
    qr = q_ref[0]
    q_s[...] = qr * _sigmoid(qr)

    nsub = CHUNK // m

    def chunk(j, carry):
        rows = pl.ds(pl.multiple_of(j * CHUNK, CHUNK), CHUNK)
        q, key, v = q_s[rows, :], key_s[rows, :], i_ref[0, rows, :]
        c16, c64 = cum16_s[rows, :], cum64_s[rows, :]
        sub = lambda x, i: x[i * m:(i + 1) * m, :]
        st = st_ref[...]
        o = _dot_nt(q * jnp.exp(c64), st)
        a_exp = []
        for i in range(nsub):
            qi, ki, ci = sub(q, i), sub(key, i), sub(c16, i)
            ps = []
            for s in range(m):
                dec = jnp.exp(jnp.where(trow >= s, ci - ci[s:s + 1, :], NEG_INF))
                ps.append(qi * dec * ki[s:s + 1, :])
            a_exp.append(_dot(jnp.concatenate(ps, axis=0), ones_bd))
        o_sub = []
        for i in range(nsub):
            vi, oi = sub(v, i), sub(o, i)
            for s in range(m):
                oi = oi + a_exp[i][s * m:(s + 1) * m, :] * vi[s:s + 1, :]
            o_sub.append(oi)
        q_hat = q * jnp.exp(c16)
        k_hat = [sub(key, i) * jnp.exp(sub(c16, i)[m - 1:m, :] - sub(c16, i)) for i in range(nsub)]
        b_start = [sub(c64, i)[0:1, :] - sub(c16, i)[0:1, :] for i in range(nsub)]
        b_end = [sub(c64, i)[m - 1:m, :] for i in range(nsub)]
        v_bd = [_stack_heads(sub(v, i)) for i in range(nsub)]
        for i in range(1, nsub):
            k_bd = jnp.concatenate([_stack_heads(k_hat[jj] * jnp.exp(b_start[i] - b_end[jj])) for jj in range(i)],
                                   axis=0)
            a_cat = _dot_nt(sub(q_hat, i), k_bd)
            o_sub[i] = o_sub[i] + _dot(a_cat, jnp.concatenate(v_bd[0:i], axis=0))
        oacc_ref[rows, :] = jnp.concatenate(o_sub, axis=0)
        last = c64[CHUNK - 1:CHUNK, :]
        st_ref[...] = st * jnp.exp(last) + jnp.where(bd, _dot_tn(v, key * jnp.exp(last - c64)), 0.0)
        return carry

    lax.fori_loop(0, rb // CHUNK, chunk, 0)
    o = oacc_ref[...]
    ms = _dot(o * o, ones_bd) * (1.0 / HEAD_DIM)
    g = g_ref[0]
    o_ref[0] = (o * lax.rsqrt(ms + RMS_EPS) * gain * (g * _sigmoid(g))).astype(BF16)


def _hgrn(pb3, par, rb):
    b, s, _ = pb3.shape
    col = lambda n: pl.BlockSpec((1, rb, GW), lambda bi, i, n=n: (bi, i, n))
    return pl.pallas_call(
        functools.partial(_hgrn_body, rb=rb),
        grid=(b, s // rb),
        in_specs=[col(0), col(1), col(2), col(3),
                  pl.BlockSpec((8, GW), lambda bi, i: (0, 0))],
        out_specs=pl.BlockSpec((1, rb, GW), lambda bi, i: (bi, i, 0)),
        out_shape=jax.ShapeDtypeStruct((b, s, GW), BF16),
        scratch_shapes=[pltpu.VMEM((GW, GW), F32)] + [pltpu.VMEM((rb, GW), F32)] * 5,
        compiler_params=_cparams("parallel", "arbitrary"),
        name="hgrn",
    )(pb3, pb3, pb3, pb3, par)


def _expand_heads(x_cat, bd):
    return jnp.where(bd, jnp.concatenate([x_cat] * N_HEADS, axis=0), 0.0)


def _mm_cat(a, x_cat, bd, passes):
    if passes == 1:
        return _dot(a, _expand_heads(x_cat.astype(BF16), bd))
    ah, al = _split(a)
    xh, xl = _split(x_cat)
    m = a.shape[0]
    top = _dot(jnp.concatenate([ah, al], axis=0), _expand_heads(xh, bd))
    return top[0:m] + (_dot(ah, _expand_heads(xl, bd)) + top[m:])


def _rwkv_chunk_maps(chunks, bd, strict_cat, incl_cat, eye_cat):
    c = CHUNK
    p_pair, p_inv, p_app = RWKV_PASSES
    pre = []
    for r, logw, k, v, alpha, beta in chunks:
        cum = _cumsum_rows(logw)
        e_last = jnp.exp(cum[c - 1:c, :])
        a_t = alpha * jnp.exp(cum - logw)
        r_t = r * jnp.exp(cum)
        e_inv = jnp.exp(-cum)
        b_t = beta * e_inv
        k_t = k * e_inv
        ar = jnp.concatenate([a_t, r_t], axis=0)
        bbd, kbd, vbd = _stack_heads(b_t), _stack_heads(k_t), _stack_heads(v)
        pair = _mm(ar, jnp.concatenate([bbd, kbd], axis=0), "nt", p_pair)
        l_ab = jnp.where(strict_cat, pair[0:c, 0:GW], 0.0)
        lm_k = jnp.where(jnp.concatenate([strict_cat, incl_cat], axis=0), pair[:, GW:], 0.0)
        m_rb = jnp.where(incl_cat, pair[c:, 0:GW], 0.0)
        wy0 = _mm(lm_k, vbd, "nn", p_app)
        pre.append((e_last, a_t, r_t, b_t, k_t, v, l_ab, m_rb, wy0))
    xs = [p[6] for p in pre]
    invs = [eye_cat + x for x in xs]
    xs = [_mm_cat(x, x, bd, p_inv[0]) for x in xs]
    for n in range(len(p_inv)):
        last = n == len(p_inv) - 1
        p_step = p_inv[n] if last else max(p_inv[n], p_inv[n + 1])
        if last:
            invs = [inv + _mm_cat(inv, x, bd, p_step) for inv, x in zip(invs, xs)]
        else:
            both = [_mm_cat(jnp.concatenate([inv, x], axis=0), x, bd, p_step) for inv, x in zip(invs, xs)]
            invs = [inv + bo[0:c] for inv, bo in zip(invs, both)]
            xs = [bo[c:] for bo in both]
    outs = []
    for (e_last, a_t, r_t, b_t, k_t, v, l_ab, m_rb, wy0), inv in zip(pre, invs):
        sol = _mm(inv, jnp.concatenate([_stack_heads(a_t), _stack_heads(wy0[0:c])], axis=1), "nn", p_app)
        a_s, u0 = sol[:, 0:GW], sol[:, GW:]
        out = _mm(m_rb, jnp.concatenate([_stack_heads(a_s), _stack_heads(u0)], axis=1), "nn", p_app)
        ry = r_t + out[:, 0:GW]
        y0 = wy0[c:] + out[:, GW:]
        bh, kh = b_t * e_last, k_t * e_last
        g = jnp.where(bd, _mm(a_s, bh, "tn", p_app), 0.0)
        c0 = jnp.where(bd, _mm(jnp.concatenate([u0, v], axis=0), jnp.concatenate([bh, kh], axis=0), "tn", p_app),
                       0.0)
        outs.append((ry, y0, g, c0, e_last))
    return outs


def _rwkv_body(pc_ref, par_ref, w2_ref, a2_ref, g2_ref, o_ref, ht_ref, prev_ref,
               r_s, w_s, k_s, v_s, al_s, be_s, ry_s, y_s, bon_s, gate_s, g_s, c_s, el_s, *, rb, nseq):
    @pl.when(pl.program_id(1) == 0)
    def _():
        ht_ref[...] = jnp.zeros_like(ht_ref)
        prev_ref[...] = jnp.zeros_like(prev_ref)

    c = CHUNK
    mu_main = [par_ref[n:n + 1, :] for n in range(3)]
    mu_lo = par_ref[3:4, 0:RWKV_LORA]
    w0, a0 = par_ref[4:5, :], par_ref[5:6, :]
    k_k, k_a, r_k = par_ref[6:7, :], par_ref[7:8, :], par_ref[8:9, :]
    ln_w, ln_b = par_ref[9:10, :], par_ref[10:11, :]
    bd = _head_block_mask(GW)
    ones_bd = jnp.where(bd, 1.0, 0.0).astype(BF16)

    row0 = lax.broadcasted_iota(jnp.int32, (rb, 1), 0) == 0
    for q in range(nseq):
        blk = pl.ds(q * rb, rb)
        pc = pc_ref[q]
        prev = jnp.where(row0, prev_ref[q:q + 1, :], pltpu.roll(pc, 1, 0))
        prev_ref[q:q + 1, :] = pc[rb - 1:rb, :]
        xs = [pc[:, n * GW:(n + 1) * GW] for n in range(3)]
        ps = [prev[:, n * GW:(n + 1) * GW] for n in range(3)]
        r, k, v = [x + mu * (p - x) for x, p, mu in zip(xs, ps, mu_main)]
        lo, plo = pc[:, 3 * GW:], prev[:, 3 * GW:]
        lo = lo + mu_lo * (plo - lo)
        w_in = -(w0 + _dot(jnp.tanh(lo), w2_ref[...]))
        w_pre = -(jnp.maximum(w_in, 0.0) + _log1p_exp_neg_abs(w_in)) - 0.5
        a = _sigmoid(a0 + _dot(lo, a2_ref[...]))
        g = _dot(_sigmoid(lo), g2_ref[...])
        kk = k * k_k
        kk = kk / jnp.maximum(jnp.sqrt(_dot(kk * kk, ones_bd)), 1e-12)
        k = k * (1.0 + (a - 1.0) * k_a)
        r_s[blk, :] = r
        w_s[blk, :] = -jnp.exp(w_pre)
        k_s[blk, :] = k
        v_s[blk, :] = v
        al_s[blk, :] = -kk
        be_s[blk, :] = kk * a
        bon_s[blk, :] = _dot(r * k * r_k, ones_bd) * v
        gate_s[blk, :] = g

    tt = lax.broadcasted_iota(jnp.int32, (c, GW), 0)
    ss = lax.broadcasted_iota(jnp.int32, (c, GW), 1) % c
    strict_cat, incl_cat = tt > ss, tt >= ss
    eye_cat = jnp.where(tt == ss, 1.0, 0.0)

    nb = RWKV_MAP_BATCH

    def chunk_maps(jj, carry):
        js = [jj * nb + n for n in range(nb)]
        rows = [pl.ds(pl.multiple_of(j * c, c), c) for j in js]
        maps = _rwkv_chunk_maps([(r_s[rw, :], w_s[rw, :], k_s[rw, :], v_s[rw, :], al_s[rw, :], be_s[rw, :])
                                 for rw in rows], bd, strict_cat, incl_cat, eye_cat)
        for j, rw, (ry, y0, g, c0, e_last) in zip(js, rows, maps):
            ry_s[rw, :] = ry
            y_s[rw, :] = y0
            g_s[j] = g.astype(BF16)
            c_s[j] = c0
            el_s[j] = jnp.broadcast_to(e_last, (8, GW))
        return carry

    lax.fori_loop(0, nseq * rb // (c * nb), chunk_maps, 0)

    nchunk = rb // c

    def chunk_apply(j, carry):
        for q in range(nseq):
            jq = q * nchunk + j
            rows = pl.ds(pl.multiple_of(jq * c, c), c)
            ht = ht_ref[q]
            htb = ht.astype(BF16)
            y_s[rows, :] = y_s[rows, :] + _dot_nt(ry_s[rows, :], htb)
            ht_ref[q] = ht * el_s[jq][0:1, :] + jnp.dot(htb, g_s[jq], preferred_element_type=F32) + c_s[jq]
        return carry

    lax.fori_loop(0, nchunk, chunk_apply, 0, unroll=2)

    y = y_s[...]
    mean = _dot(y, ones_bd) * (1.0 / HEAD_DIM)
    yc = y - mean
    var = _dot(yc * yc, ones_bd) * (1.0 / HEAD_DIM)
    yn = yc * lax.rsqrt(var + RWKV_GN_EPS) * ln_w + ln_b
    o_ref[...] = ((yn + bon_s[...]) * gate_s[...]).astype(BF16).reshape(nseq, rb, GW)


def _rwkv(pc3, par, w2p, a2p, g2p, rb):
    b, s, _ = pc3.shape
    nseq = 2 if b % 2 == 0 else 1
    nchunk = nseq * rb // CHUNK
    full = lambda shp: pl.BlockSpec(shp, lambda bi, i: (0,) * len(shp))
    return pl.pallas_call(
        functools.partial(_rwkv_body, rb=rb, nseq=nseq),
        grid=(b // nseq, s // rb),
        in_specs=[pl.BlockSpec((nseq, rb, RWKV_COLS), lambda bi, i: (bi, i, 0)),
                  full((16, GW)), full((RWKV_LORA, GW)), full((RWKV_LORA, GW)), full((RWKV_LORA, GW))],
        out_specs=pl.BlockSpec((nseq, rb, GW), lambda bi, i: (bi, i, 0)),
        out_shape=jax.ShapeDtypeStruct((b, s, GW), BF16),
        scratch_shapes=[pltpu.VMEM((nseq, GW, GW), F32), pltpu.VMEM((nseq, RWKV_COLS), F32)]
        + [pltpu.VMEM((nseq * rb, GW), F32)] * 10
        + [pltpu.VMEM((nchunk, GW, GW), BF16), pltpu.VMEM((nchunk, GW, GW), F32),
           pltpu.VMEM((nchunk, 8, GW), F32)],
        compiler_params=_cparams("parallel", "arbitrary"),
        name="rwkv",
    )(pc3, par, w2p, a2p, g2p)


def _lru_body(x_ref, g_ref, par_ref, wa_ref, wx_ref, o_ref, ext_ref, h_ref, *, rb):
    @pl.when(pl.program_id(1) == 0)
    def _():
        ext_ref[0:8, :] = jnp.zeros((8, GW), F32)
        h_ref[...] = jnp.zeros_like(h_ref)

    conv_b, ba, bx, lam = par_ref[4:5, :], par_ref[5:6, :], par_ref[6:7, :], par_ref[7:8, :]
    x = x_ref[0]
    ext_ref[8:8 + rb, :] = x
    conv = conv_b + par_ref[0:1, :] * ext_ref[pl.ds(8 - 3, rb), :]
    for j in range(1, LRU_CONV):
        conv = conv + par_ref[j:j + 1, :] * ext_ref[pl.ds(8 - 3 + j, rb), :]
    ext_ref[0:8, :] = x[rb - 8:rb, :]
    gate_r = _sigmoid(_dot(conv, wa_ref[...]) + ba)
    gate_i = _sigmoid(_dot(conv, wx_ref[...]) + bx)
    log_a = -LRU_C * gate_r * _softplus(-lam)
    a = jnp.exp(log_a)
    inp = jnp.sqrt(-_expm1(2.0 * log_a)) * (gate_i * conv)
    row = lax.broadcasted_iota(jnp.int32, (rb, 1), 0) % CHUNK
    d = 1
    while d < CHUNK:
        m = row >= d
        inp = jnp.where(m, a * pltpu.roll(inp, d, 0) + inp, inp)
        a = jnp.where(m, a * pltpu.roll(a, d, 0), a)
        d *= 2
    h_prev, hs = h_ref[...], []
    for n in range(rb // CHUNK):
        seg = slice(n * CHUNK, (n + 1) * CHUNK)
        hs.append(inp[seg] + a[seg] * h_prev)
        h_prev = hs[-1][CHUNK - 1:CHUNK, :]
    h = jnp.concatenate(hs, axis=0)
    h_ref[...] = h_prev
    g = g_ref[0]
    gelu = 0.5 * g * (1.0 + jnp.tanh(math.sqrt(2.0 / math.pi) * (g + 0.044715 * (g * g * g))))
    o_ref[0] = (h * gelu).astype(BF16)


def _lru(pd3, par, wa_bd, wx_bd, rb):
    b, s, _ = pd3.shape
    full = lambda shp: pl.BlockSpec(shp, lambda bi, i: (0,) * len(shp))
    return pl.pallas_call(
        functools.partial(_lru_body, rb=rb),
        grid=(b, s // rb),
        in_specs=[pl.BlockSpec((1, rb, GW), lambda bi, i: (bi, i, 0)),
                  pl.BlockSpec((1, rb, GW), lambda bi, i: (bi, i, 1)),
                  full((8, GW)), full((GW, GW)), full((GW, GW))],
        out_specs=pl.BlockSpec((1, rb, GW), lambda bi, i: (bi, i, 0)),
        out_shape=jax.ShapeDtypeStruct((b, s, GW), BF16),
        scratch_shapes=[pltpu.VMEM((rb + 8, GW), F32), pltpu.VMEM((1, GW), F32)],
        compiler_params=_cparams("parallel", "arbitrary"),
        name="lru",
    )(pd3, pd3, par, wa_bd, wx_bd)


def _outmlp_body(ya_ref, yb_ref, yc_ref, yd_ref, x_ref, wo_ref, gains_ref, w1_ref, w2_ref, o_ref,
                 xm_ref, h_ref, acc_ref):
    kk = pl.program_id(1)

    @pl.when(kk == 0)
    def _():
        mix = jnp.concatenate([ya_ref[...], yb_ref[...], yc_ref[...], yd_ref[...]], axis=1)
        xm = x_ref[...] + _rms(jnp.dot(mix, wo_ref[...], preferred_element_type=F32), gains_ref[0:1, :])
        xm_ref[...] = xm
        h_ref[...] = _rms(xm, gains_ref[1:2, :]).astype(BF16)
        acc_ref[...] = jnp.zeros_like(acc_ref)

    a = jnp.dot(h_ref[...], w1_ref[...], preferred_element_type=F32)
    a = jnp.square(jnp.maximum(a, 0.0)).astype(BF16)
    acc_ref[...] += jnp.dot(a, w2_ref[...], preferred_element_type=F32)

    @pl.when(kk == pl.num_programs(1) - 1)
    def _():
        o_ref[...] = xm_ref[...] + _rms(acc_ref[...], gains_ref[2:3, :])


def _outmlp(ya, yb, yc, yd, x2, wo_all, gains, w1_all, w2_all, layer, tm, tk):
    t, d = x2.shape
    hid = w1_all.shape[2]
    mix = pl.BlockSpec((tm, GW), lambda i, k: (i, 0))
    return pl.pallas_call(
        _outmlp_body,
        grid=(t // tm, hid // tk),
        in_specs=[mix, mix, mix, mix,
                  pl.BlockSpec((tm, d), lambda i, k: (i, 0)),
                  pl.BlockSpec((None, 4 * GW, d), lambda i, k: (layer, 0, 0)),
                  pl.BlockSpec((8, d), lambda i, k: (0, 0)),
                  pl.BlockSpec((None, d, tk), lambda i, k: (layer, 0, k)),
                  pl.BlockSpec((None, tk, d), lambda i, k: (layer, k, 0))],
        out_specs=pl.BlockSpec((tm, d), lambda i, k: (i, 0)),
        out_shape=jax.ShapeDtypeStruct((t, d), F32),
        scratch_shapes=[pltpu.VMEM((tm, d), F32), pltpu.VMEM((tm, d), BF16), pltpu.VMEM((tm, d), F32)],
        compiler_params=_cparams("parallel", "arbitrary"),
        name="outmlp",
    )(ya, yb, yc, yd, x2, wo_all, gains, w1_all, w2_all)


def _rows(vectors, n_rows):
    tab = jnp.stack([v.astype(F32) for v in vectors])
    return jnp.pad(tab, ((0, n_rows - tab.shape[0]), (0, 0)))


def _block_diag(w):
    out = jnp.zeros((GW, GW), w.dtype)
    for n in range(w.shape[0]):
        out = out.at[n * HEAD_DIM:(n + 1) * HEAD_DIM, n * HEAD_DIM:(n + 1) * HEAD_DIM].set(w[n])
    return out


def _pick_block(n, want):
    while n % want:
        want //= 2
    return want


def kernel(x, norm_mix_pre, norm_mix_post, norm_mlp_pre, norm_mlp_post, w_in, w_out, attn_rel_bias, hgrn_lb_logits, hgrn_norm, rwkv_mu, rwkv_w0, rwkv_w2, rwkv_a0, rwkv_a2, rwkv_g2, rwkv_k_k, rwkv_k_a, rwkv_r_k, rwkv_ln_w, rwkv_ln_b, lru_conv_w, lru_conv_b, lru_wa, lru_ba, lru_wx, lru_bx, lru_lambda, mlp_w1, mlp_w2):
    b, s, d = x.shape
    depth = w_in.shape[0]
    t = b * s
    tm = _pick_block(t, 512)
    tm_mlp = _pick_block(t, 1024)
    rb = _pick_block(s, 512)
    tk = _pick_block(mlp_w1.shape[-1], 1024)

    lb_sm = jax.nn.softmax(hgrn_lb_logits.astype(F32), axis=0)
    lb_all = jnp.maximum(jnp.cumsum(lb_sm, axis=0) - lb_sm[0:1], 0.0)

    w_in_b, w_out_b = w_in.astype(BF16), w_out.astype(BF16)
    w1_b, w2_b = mlp_w1.astype(BF16), mlp_w2.astype(BF16)

    x2 = x.reshape(t, d)
    for l in range(depth):
        pa, pb, pc, pd = _inproj(x2, norm_mix_pre[l].reshape(1, d), w_in_b, l, tm)

        ya = _attention(pa.reshape(b, s, ATTN_COLS), _attn_bias_table(attn_rel_bias[l]), rb)

        lb = lb_all[l]
        hgrn_par = _rows([lb, jnp.log1p(-lb), 1.0 - lb, hgrn_norm[l]], 8)
        yb = _hgrn(pb.reshape(b, s, HGRN_COLS), hgrn_par, rb)

        mu = rwkv_mu[l].astype(F32)
        mu_lo = jnp.pad(mu[3 * GW:], (0, GW - RWKV_LORA))
        rwkv_par = _rows([mu[0:GW], mu[GW:2 * GW], mu[2 * GW:3 * GW], mu_lo, rwkv_w0[l], rwkv_a0[l],
                          rwkv_k_k[l], rwkv_k_a[l], rwkv_r_k[l].reshape(GW), rwkv_ln_w[l], rwkv_ln_b[l]], 16)
        zeros = lambda n: jnp.zeros((n, GW), F32)
        w2p = jnp.concatenate([rwkv_w2[l].astype(F32), zeros(96)], axis=0).astype(BF16)
        a2p = jnp.concatenate([zeros(32), rwkv_a2[l].astype(F32), zeros(64)], axis=0).astype(BF16)
        g2p = jnp.concatenate([zeros(64), rwkv_g2[l].astype(F32)], axis=0).astype(BF16)
        yc = _rwkv(pc.reshape(b, s, RWKV_COLS), rwkv_par, w2p, a2p, g2p, rb)

        cw = lru_conv_w[l].astype(F32)
        lru_par = _rows([cw[0], cw[1], cw[2], cw[3], lru_conv_b[l], lru_ba[l], lru_bx[l], lru_lambda[l]], 8)
        yd = _lru(pd.reshape(b, s, LRU_COLS), lru_par, _block_diag(lru_wa[l]).astype(BF16),
                  _block_diag(lru_wx[l]).astype(BF16), rb)

        gains = jnp.pad(jnp.stack([norm_mix_post[l], norm_mlp_pre[l], norm_mlp_post[l]]).astype(F32),
                        ((0, 5), (0, 0)))
        x2 = _outmlp(ya.reshape(t, GW), yb.reshape(t, GW), yc.reshape(t, GW), yd.reshape(t, GW),
                     x2, w_out_b, gains, w1_b, w2_b, l, tm_mlp, tk)
    return x2.reshape(b, s, d)
```

```python
import functools
import math

import numpy as np

import jax
import jax.numpy as jnp
from jax import lax
from jax.experimental import pallas as pl
from jax.experimental.pallas import tpu as pltpu

F32 = jnp.float32
BF16 = jnp.bfloat16

GW = 256
HEAD_DIM = 64
N_HEADS = 4
CHUNK = 64
ATTN_LEFT_CHUNKS = 8
BAND = (ATTN_LEFT_CHUNKS + 1) * CHUNK
ATTN_PAD = ATTN_LEFT_CHUNKS * CHUNK
REL_CLIP = 256
ATTN_SCALE = HEAD_DIM ** -0.5
NEG_INF = -1e30
RMS_EPS = 1e-6
RWKV_GN_EPS = HEAD_DIM * 1e-5
RWKV_LORA = 128
LRU_C = 8.0
LRU_CONV = 4
HGRN_SUB = 16

ATTN_COLS = 3 * GW
HGRN_COLS = 4 * GW
RWKV_COLS = 3 * GW + RWKV_LORA
LRU_COLS = 2 * GW

VMEM_LIMIT = 56 * 1024 * 1024
RWKV_PASSES = (1, (3, 3, 3, 1, 1), 1)
RWKV_MAP_BATCH = 8
ATTN_BATCH = 8


def _cparams(*sem):
    return pltpu.CompilerParams(dimension_semantics=sem, vmem_limit_bytes=VMEM_LIMIT)


def _dot(a, b):
    return jnp.dot(a.astype(BF16), b.astype(BF16), preferred_element_type=F32)


def _dot_nt(a, b):
    return lax.dot_general(a.astype(BF16), b.astype(BF16), (((1,), (1,)), ((), ())),
                           preferred_element_type=F32)


def _dot_tn(a, b):
    return lax.dot_general(a.astype(BF16), b.astype(BF16), (((0,), (0,)), ((), ())),
                           preferred_element_type=F32)


def _split(a):
    hi = a.astype(BF16)
    lo = (a - hi.astype(F32)).astype(BF16)
    return hi, lo


def _mm(a, b, kind="nn", passes=1):
    f = {"nn": _dot, "nt": _dot_nt, "tn": _dot_tn}[kind]
    if passes == 1:
        return f(a, b)
    ah, al = _split(a)
    bh, bl = _split(b)
    if kind == "tn":
        return f(ah, bh) + (f(ah, bl) + f(al, bh))
    m = a.shape[0]
    top = f(jnp.concatenate([ah, al], axis=0), bh)
    return top[0:m] + (f(ah, bl) + top[m:])


def _rms(x, gain):
    return x * lax.rsqrt(jnp.mean(x * x, axis=-1, keepdims=True) + RMS_EPS) * gain


def _sigmoid(x):
    return 1.0 / (1.0 + jnp.exp(-x))


def _softplus(x):
    return jnp.maximum(x, 0.0) + jnp.log1p(jnp.exp(-jnp.abs(x)))


def _log1p_exp_neg_abs(x):
    return jnp.log(1.0 + jnp.exp(-jnp.abs(x)))


def _expm1(z):
    u = jnp.exp(z)
    um1 = u - 1.0
    near = um1 * z / jnp.where(u == 1.0, 1.0, jnp.log(u))
    return jnp.where(jnp.abs(z) > 0.5, um1, jnp.where(u == 1.0, z, near))


def _head_block_mask(n):
    r = lax.broadcasted_iota(jnp.int32, (n, n), 0) // HEAD_DIM
    c = lax.broadcasted_iota(jnp.int32, (n, n), 1) // HEAD_DIM
    return r == c


def _cumsum_rows(x):
    n = x.shape[0]
    row = lax.broadcasted_iota(jnp.int32, (n, 1), 0)
    d = 1
    while d < n:
        x = x + jnp.where(row >= d, pltpu.roll(x, d, 0), 0.0)
        d *= 2
    return x


def _stack_heads(x):
    lane_head = lax.broadcasted_iota(jnp.int32, (1, GW), 1) // HEAD_DIM
    return jnp.concatenate([jnp.where(lane_head == h, x, 0.0) for h in range(N_HEADS)], axis=0)


def _unstack_heads(xbd, c):
    return xbd[0:c] + xbd[c:2 * c] + xbd[2 * c:3 * c] + xbd[3 * c:4 * c]


def _inproj_body(x_ref, g_ref, w_ref, oa_ref, ob_ref, oc_ref, od_ref):
    h = _rms(x_ref[...], g_ref[...]).astype(BF16)
    o0, o1, o2 = ATTN_COLS, ATTN_COLS + HGRN_COLS, ATTN_COLS + HGRN_COLS + RWKV_COLS
    oa_ref[...] = jnp.dot(h, w_ref[:, 0:o0], preferred_element_type=F32).astype(BF16)
    ob_ref[...] = jnp.dot(h, w_ref[:, o0:o1], preferred_element_type=F32)
    oc_ref[...] = jnp.dot(h, w_ref[:, o1:o2], preferred_element_type=F32)
    od_ref[...] = jnp.dot(h, w_ref[:, o2:], preferred_element_type=F32)


def _inproj(x2, gain, w_all, layer, tm):
    t, d = x2.shape
    d_in = w_all.shape[2]
    return pl.pallas_call(
        _inproj_body,
        grid=(t // tm,),
        in_specs=[pl.BlockSpec((tm, d), lambda i: (i, 0)),
                  pl.BlockSpec((1, d), lambda i: (0, 0)),
                  pl.BlockSpec((None, d, d_in), lambda i: (layer, 0, 0))],
        out_specs=[pl.BlockSpec((tm, ATTN_COLS), lambda i: (i, 0)),
                   pl.BlockSpec((tm, HGRN_COLS), lambda i: (i, 0)),
                   pl.BlockSpec((tm, RWKV_COLS), lambda i: (i, 0)),
                   pl.BlockSpec((tm, LRU_COLS), lambda i: (i, 0))],
        out_shape=[jax.ShapeDtypeStruct((t, ATTN_COLS), BF16),
                   jax.ShapeDtypeStruct((t, HGRN_COLS), F32),
                   jax.ShapeDtypeStruct((t, RWKV_COLS), F32),
                   jax.ShapeDtypeStruct((t, LRU_COLS), F32)],
        compiler_params=_cparams("parallel"),
        name="inproj",
    )(x2, gain, w_all)


def _attn_body(q_ref, k_ref, v_ref, bias_ref, o_ref, kp_ref, vp_ref, *, chunks):
    i = pl.program_id(1)

    @pl.when(i == 0)
    def _():
        for src, dst in ((k_ref, kp_ref), (v_ref, vp_ref)):
            dst[0:ATTN_PAD, :] = jnp.zeros((ATTN_PAD, GW), BF16)
            dst[ATTN_PAD:, :] = src[0]

    lane_head = lax.broadcasted_iota(jnp.int32, (1, GW), 1) // HEAD_DIM
    kpos = lax.broadcasted_iota(jnp.int32, (1, BAND), 1)

    nb = ATTN_BATCH

    def chunk_group(jj, carry):
        cs = [i * chunks + jj * nb + n for n in range(nb)]
        rows = [pl.ds(pl.multiple_of((jj * nb + n) * CHUNK, CHUNK), CHUNK) for n in range(nb)]
        wins = [pl.ds(pl.multiple_of(c * CHUNK, CHUNK), BAND) for c in cs]
        ss = []
        for c, rw, win in zip(cs, rows, wins):
            q = q_ref[0, rw, :] * ATTN_SCALE
            qbd = jnp.concatenate([jnp.where(lane_head == h, q, jnp.zeros_like(q)) for h in range(N_HEADS)],
                                  axis=0)
            s = _dot_nt(qbd, kp_ref[win, :]) + bias_ref[...]
            ss.append(jnp.where(kpos >= (ATTN_LEFT_CHUNKS - c) * CHUNK, s, NEG_INF))
        ps = [jnp.exp(s - jnp.max(s, axis=-1, keepdims=True)) for s in ss]
        for rw, win, p in zip(rows, wins, ps):
            inv_l = 1.0 / jnp.sum(p, axis=-1, keepdims=True)
            obd = _dot(p, vp_ref[win, :]) * inv_l
            o = jnp.zeros((CHUNK, GW), F32)
            for h in range(N_HEADS):
                o = o + jnp.where(lane_head == h, obd[h * CHUNK:(h + 1) * CHUNK, :], 0.0)
            o_ref[0, rw, :] = o.astype(BF16)
        return carry

    lax.fori_loop(0, chunks // nb, chunk_group, 0)


def _attention(pa3, bias, rb):
    b, s, _ = pa3.shape
    return pl.pallas_call(
        functools.partial(_attn_body, chunks=rb // CHUNK),
        grid=(b, s // rb),
        in_specs=[pl.BlockSpec((1, rb, GW), lambda bi, i: (bi, i, 0)),
                  pl.BlockSpec((1, s, GW), lambda bi, i: (bi, 0, 1)),
                  pl.BlockSpec((1, s, GW), lambda bi, i: (bi, 0, 2)),
                  pl.BlockSpec((N_HEADS * CHUNK, BAND), lambda bi, i: (0, 0))],
        out_specs=pl.BlockSpec((1, rb, GW), lambda bi, i: (bi, i, 0)),
        out_shape=jax.ShapeDtypeStruct((b, s, GW), BF16),
        scratch_shapes=[pltpu.VMEM((s + ATTN_PAD, GW), BF16)] * 2,
        compiler_params=_cparams("parallel", "arbitrary"),
        name="attn",
    )(pa3, pa3, pa3, bias)


def _attn_bias_table(rel_bias):
    rel = np.arange(BAND + CHUNK - 1) - (CHUNK - 1) - ATTN_LEFT_CHUNKS * CHUNK
    ext = rel_bias.astype(F32)[:, np.clip(rel, -REL_CLIP, REL_CLIP) + REL_CLIP]
    tab = jnp.stack([ext[:, CHUNK - 1 - q:CHUNK - 1 - q + BAND] for q in range(CHUNK)], axis=1)
    return tab.reshape(N_HEADS * CHUNK, BAND)


def _hgrn_body(q_ref, f_ref, i_ref, g_ref, par_ref, o_ref, st_ref, oacc_ref, q_s, key_s, cum16_s, cum64_s,
               *, rb):
    @pl.when(pl.program_id(1) == 0)
    def _():
        st_ref[...] = jnp.zeros_like(st_ref)

    m = HGRN_SUB
    lb = par_ref[0:1, :]
    log_1m_lb = par_ref[1:2, :]
    one_m_lb = par_ref[2:3, :]
    gain = par_ref[3:4, :]
    bd = _head_block_mask(GW)
    ones_bd = jnp.where(bd, 1.0, 0.0).astype(BF16)
    trow = lax.broadcasted_iota(jnp.int32, (m, 1), 0)

    fr = f_ref[0]
    e = jnp.exp(-jnp.abs(fr))
    inv_1pe = 1.0 / (1.0 + e)
    sig = jnp.where(fr >= 0.0, inv_1pe, e * inv_1pe)
    log_sig = jnp.minimum(fr, 0.0) - jnp.log(1.0 + e)
    log_f = jnp.maximum(jnp.log(lb + one_m_lb * sig), log_1m_lb + log_sig)
    row = lax.broadcasted_iota(jnp.int32, (rb, 1), 0) % CHUNK
    cum, d = log_f, 1
    while d < CHUNK:
        cum = cum + jnp.where(row >= d, pltpu.roll(cum, d, 0), 0.0)
        d *= 2
    cum64_s[...] = cum
    before = jnp.where(row == 0, 0.0, pltpu.roll(cum, 1, 0)).reshape(rb // m, m, GW)[:, 0:1, :]
    cum16_s[...] = cum - jnp.broadcast_to(before, (rb // m, m, GW)).reshape(rb, GW)
    key_s[...] = one_m_lb * jnp.where(fr >= 0.0, e * inv_1pe, inv_1pe)
    qr = q_ref[0]
    q_s[...] = qr * _sigmoid(qr)

    nsub = CHUNK // m

    def chunk(j, carry):
        rows = pl.ds(pl.multiple_of(j * CHUNK, CHUNK), CHUNK)
        q, key, v = q_s[rows, :], key_s[rows, :], i_ref[0, rows, :]
        c16, c64 = cum16_s[rows, :], cum64_s[rows, :]
        sub = lambda x, i: x[i * m:(i + 1) * m, :]
        st = st_ref[...]
        o = _dot_nt(q * jnp.exp(c64), st)
        a_exp = []
        for i in range(nsub):
            qi, ki, ci = sub(q, i), sub(key, i), sub(c16, i)
            ps = []
            for s in range(m):
                dec = jnp.exp(jnp.where(trow >= s, ci - ci[s:s + 1, :], NEG_INF))
                ps.append(qi * dec * ki[s:s + 1, :])
            a_exp.append(_dot(jnp.concatenate(ps, axis=0), ones_bd))
        o_sub = []
        for i in range(nsub):
            vi, oi = sub(v, i), sub(o, i)
            for s in range(m):
                oi = oi + a_exp[i][s * m:(s + 1) * m, :] * vi[s:s + 1, :]
            o_sub.append(oi)
        q_hat = q * jnp.exp(c16)
        k_hat = [sub(key, i) * jnp.exp(sub(c16, i)[m - 1:m, :] - sub(c16, i)) for i in range(nsub)]
        b_start = [sub(c64, i)[0:1, :] - sub(c16, i)[0:1, :] for i in range(nsub)]
        b_end = [sub(c64, i)[m - 1:m, :] for i in range(nsub)]
        v_bd = [_stack_heads(sub(v, i)) for i in range(nsub)]
        for i in range(1, nsub):
            k_bd = jnp.concatenate([_stack_heads(k_hat[jj] * jnp.exp(b_start[i] - b_end[jj])) for jj in range(i)],
                                   axis=0)
            a_cat = _dot_nt(sub(q_hat, i), k_bd)
            o_sub[i] = o_sub[i] + _dot(a_cat, jnp.concatenate(v_bd[0:i], axis=0))
        oacc_ref[rows, :] = jnp.concatenate(o_sub, axis=0)
        last = c64[CHUNK - 1:CHUNK, :]
        st_ref[...] = st * jnp.exp(last) + jnp.where(bd, _dot_tn(v, key * jnp.exp(last - c64)), 0.0)
        return carry

    lax.fori_loop(0, rb // CHUNK, chunk, 0)
    o = oacc_ref[...]
    ms = _dot(o * o, ones_bd) * (1.0 / HEAD_DIM)
    g = g_ref[0]
    o_ref[0] = (o * lax.rsqrt(ms + RMS_EPS) * gain * (g * _sigmoid(g))).astype(BF16)


def _hgrn(pb3, par, rb):
    b, s, _ = pb3.shape
    col = lambda n: pl.BlockSpec((1, rb, GW), lambda bi, i, n=n: (bi, i, n))
    return pl.pallas_call(
        functools.partial(_hgrn_body, rb=rb),
        grid=(b, s // rb),
        in_specs=[col(0), col(1), col(2), col(3),
                  pl.BlockSpec((8, GW), lambda bi, i: (0, 0))],
        out_specs=pl.BlockSpec((1, rb, GW), lambda bi, i: (bi, i, 0)),
        out_shape=jax.ShapeDtypeStruct((b, s, GW), BF16),
        scratch_shapes=[pltpu.VMEM((GW, GW), F32)] + [pltpu.VMEM((rb, GW), F32)] * 5,
        compiler_params=_cparams("parallel", "arbitrary"),
        name="hgrn",
    )(pb3, pb3, pb3, pb3, par)


def _expand_heads(x_cat, bd):
    return jnp.where(bd, jnp.concatenate([x_cat] * N_HEADS, axis=0), 0.0)


def _mm_cat(a, x_cat, bd, passes):
    if passes == 1:
        return _dot(a, _expand_heads(x_cat.astype(BF16), bd))
    ah, al = _split(a)
    xh, xl = _split(x_cat)
    m = a.shape[0]
    top = _dot(jnp.concatenate([ah, al], axis=0), _expand_heads(xh, bd))
    return top[0:m] + (_dot(ah, _expand_heads(xl, bd)) + top[m:])


def _rwkv_chunk_maps(chunks, bd, strict_cat, incl_cat, eye_cat):
    c = CHUNK
    p_pair, p_inv, p_app = RWKV_PASSES
    pre = []
    for r, logw, k, v, alpha, beta in chunks:
        cum = _cumsum_rows(logw)
        e_last = jnp.exp(cum[c - 1:c, :])
        a_t = alpha * jnp.exp(cum - logw)
        r_t = r * jnp.exp(cum)
        e_inv = jnp.exp(-cum)
        b_t = beta * e_inv
        k_t = k * e_inv
        ar = jnp.concatenate([a_t, r_t], axis=0)
        bbd, kbd, vbd = _stack_heads(b_t), _stack_heads(k_t), _stack_heads(v)
        pair = _mm(ar, jnp.concatenate([bbd, kbd], axis=0), "nt", p_pair)
        l_ab = jnp.where(strict_cat, pair[0:c, 0:GW], 0.0)
        lm_k = jnp.where(jnp.concatenate([strict_cat, incl_cat], axis=0), pair[:, GW:], 0.0)
        m_rb = jnp.where(incl_cat, pair[c:, 0:GW], 0.0)
        wy0 = _mm(lm_k, vbd, "nn", p_app)
        pre.append((e_last, a_t, r_t, b_t, k_t, v, l_ab, m_rb, wy0))
    xs = [p[6] for p in pre]
    invs = [eye_cat + x for x in xs]
    xs = [_mm_cat(x, x, bd, p_inv[0]) for x in xs]
    for n in range(len(p_inv)):
        last = n == len(p_inv) - 1
        p_step = p_inv[n] if last else max(p_inv[n], p_inv[n + 1])
        if last:
            invs = [inv + _mm_cat(inv, x, bd, p_step) for inv, x in zip(invs, xs)]
        else:
            both = [_mm_cat(jnp.concatenate([inv, x], axis=0), x, bd, p_step) for inv, x in zip(invs, xs)]
            invs = [inv + bo[0:c] for inv, bo in zip(invs, both)]
            xs = [bo[c:] for bo in both]
    outs = []
    for (e_last, a_t, r_t, b_t, k_t, v, l_ab, m_rb, wy0), inv in zip(pre, invs):
        sol = _mm(inv, jnp.concatenate([_stack_heads(a_t), _stack_heads(wy0[0:c])], axis=1), "nn", p_app)
        a_s, u0 = sol[:, 0:GW], sol[:, GW:]
        out = _mm(m_rb, jnp.concatenate([_stack_heads(a_s), _stack_heads(u0)], axis=1), "nn", p_app)
        ry = r_t + out[:, 0:GW]
        y0 = wy0[c:] + out[:, GW:]
        bh, kh = b_t * e_last, k_t * e_last
        g = jnp.where(bd, _mm(a_s, bh, "tn", p_app), 0.0)
        c0 = jnp.where(bd, _mm(jnp.concatenate([u0, v], axis=0), jnp.concatenate([bh, kh], axis=0), "tn", p_app),
                       0.0)
        outs.append((ry, y0, g, c0, e_last))
    return outs


def _rwkv_body(pc_ref, par_ref, w2_ref, a2_ref, g2_ref, o_ref, ht_ref, prev_ref,
               r_s, w_s, k_s, v_s, al_s, be_s, ry_s, y_s, bon_s, gate_s, g_s, c_s, el_s, *, rb, nseq):
    @pl.when(pl.program_id(1) == 0)
    def _():
        ht_ref[...] = jnp.zeros_like(ht_ref)
        prev_ref[...] = jnp.zeros_like(prev_ref)

    c = CHUNK
    mu_main = [par_ref[n:n + 1, :] for n in range(3)]
    mu_lo = par_ref[3:4, 0:RWKV_LORA]
    w0, a0 = par_ref[4:5, :], par_ref[5:6, :]
    k_k, k_a, r_k = par_ref[6:7, :], par_ref[7:8, :], par_ref[8:9, :]
    ln_w, ln_b = par_ref[9:10, :], par_ref[10:11, :]
    bd = _head_block_mask(GW)
    ones_bd = jnp.where(bd, 1.0, 0.0).astype(BF16)

    row0 = lax.broadcasted_iota(jnp.int32, (rb, 1), 0) == 0
    for q in range(nseq):
        blk = pl.ds(q * rb, rb)
        pc = pc_ref[q]
        prev = jnp.where(row0, prev_ref[q:q + 1, :], pltpu.roll(pc, 1, 0))
        prev_ref[q:q + 1, :] = pc[rb - 1:rb, :]
        xs = [pc[:, n * GW:(n + 1) * GW] for n in range(3)]
        ps = [prev[:, n * GW:(n + 1) * GW] for n in range(3)]
        r, k, v = [x + mu * (p - x) for x, p, mu in zip(xs, ps, mu_main)]
        lo, plo = pc[:, 3 * GW:], prev[:, 3 * GW:]
        lo = lo + mu_lo * (plo - lo)
        w_in = -(w0 + _dot(jnp.tanh(lo), w2_ref[...]))
        w_pre = -(jnp.maximum(w_in, 0.0) + _log1p_exp_neg_abs(w_in)) - 0.5
        a = _sigmoid(a0 + _dot(lo, a2_ref[...]))
        g = _dot(_sigmoid(lo), g2_ref[...])
        kk = k * k_k
        kk = kk / jnp.maximum(jnp.sqrt(_dot(kk * kk, ones_bd)), 1e-12)
        k = k * (1.0 + (a - 1.0) * k_a)
        r_s[blk, :] = r
        w_s[blk, :] = -jnp.exp(w_pre)
        k_s[blk, :] = k
        v_s[blk, :] = v
        al_s[blk, :] = -kk
        be_s[blk, :] = kk * a
        bon_s[blk, :] = _dot(r * k * r_k, ones_bd) * v
        gate_s[blk, :] = g

    tt = lax.broadcasted_iota(jnp.int32, (c, GW), 0)
    ss = lax.broadcasted_iota(jnp.int32, (c, GW), 1) % c
    strict_cat, incl_cat = tt > ss, tt >= ss
    eye_cat = jnp.where(tt == ss, 1.0, 0.0)

    nb = RWKV_MAP_BATCH

    def chunk_maps(jj, carry):
        js = [jj * nb + n for n in range(nb)]
        rows = [pl.ds(pl.multiple_of(j * c, c), c) for j in js]
        maps = _rwkv_chunk_maps([(r_s[rw, :], w_s[rw, :], k_s[rw, :], v_s[rw, :], al_s[rw, :], be_s[rw, :])
                                 for rw in rows], bd, strict_cat, incl_cat, eye_cat)
        for j, rw, (ry, y0, g, c0, e_last) in zip(js, rows, maps):
            ry_s[rw, :] = ry
            y_s[rw, :] = y0
            g_s[j] = g.astype(BF16)
            c_s[j] = c0
            el_s[j] = jnp.broadcast_to(e_last, (8, GW))
        return carry

    lax.fori_loop(0, nseq * rb // (c * nb), chunk_maps, 0)

    nchunk = rb // c

    def chunk_apply(j, carry):
        for q in range(nseq):
            jq = q * nchunk + j
            rows = pl.ds(pl.multiple_of(jq * c, c), c)
            ht = ht_ref[q]
            htb = ht.astype(BF16)
            y_s[rows, :] = y_s[rows, :] + _dot_nt(ry_s[rows, :], htb)
            ht_ref[q] = ht * el_s[jq][0:1, :] + jnp.dot(htb, g_s[jq], preferred_element_type=F32) + c_s[jq]
        return carry

    lax.fori_loop(0, nchunk, chunk_apply, 0, unroll=2)

    y = y_s[...]
    mean = _dot(y, ones_bd) * (1.0 / HEAD_DIM)
    yc = y - mean
    var = _dot(yc * yc, ones_bd) * (1.0 / HEAD_DIM)
    yn = yc * lax.rsqrt(var + RWKV_GN_EPS) * ln_w + ln_b
    o_ref[...] = ((yn + bon_s[...]) * gate_s[...]).astype(BF16).reshape(nseq, rb, GW)


def _rwkv(pc3, par, w2p, a2p, g2p, rb):
    b, s, _ = pc3.shape
    nseq = 2 if b % 2 == 0 else 1
    nchunk = nseq * rb // CHUNK
    full = lambda shp: pl.BlockSpec(shp, lambda bi, i: (0,) * len(shp))
    return pl.pallas_call(
        functools.partial(_rwkv_body, rb=rb, nseq=nseq),
        grid=(b // nseq, s // rb),
        in_specs=[pl.BlockSpec((nseq, rb, RWKV_COLS), lambda bi, i: (bi, i, 0)),
                  full((16, GW)), full((RWKV_LORA, GW)), full((RWKV_LORA, GW)), full((RWKV_LORA, GW))],
        out_specs=pl.BlockSpec((nseq, rb, GW), lambda bi, i: (bi, i, 0)),
        out_shape=jax.ShapeDtypeStruct((b, s, GW), BF16),
        scratch_shapes=[pltpu.VMEM((nseq, GW, GW), F32), pltpu.VMEM((nseq, RWKV_COLS), F32)]
        + [pltpu.VMEM((nseq * rb, GW), F32)] * 10
        + [pltpu.VMEM((nchunk, GW, GW), BF16), pltpu.VMEM((nchunk, GW, GW), F32),
           pltpu.VMEM((nchunk, 8, GW), F32)],
        compiler_params=_cparams("parallel", "arbitrary"),
        name="rwkv",
    )(pc3, par, w2p, a2p, g2p)


def _lru_body(x_ref, g_ref, par_ref, wa_ref, wx_ref, o_ref, ext_ref, h_ref, *, rb):
    @pl.when(pl.program_id(1) == 0)
    def _():
        ext_ref[0:8, :] = jnp.zeros((8, GW), F32)
        h_ref[...] = jnp.zeros_like(h_ref)

    conv_b, ba, bx, lam = par_ref[4:5, :], par_ref[5:6, :], par_ref[6:7, :], par_ref[7:8, :]
    x = x_ref[0]
    ext_ref[8:8 + rb, :] = x
    conv = conv_b + par_ref[0:1, :] * ext_ref[pl.ds(8 - 3, rb), :]
    for j in range(1, LRU_CONV):
        conv = conv + par_ref[j:j + 1, :] * ext_ref[pl.ds(8 - 3 + j, rb), :]
    ext_ref[0:8, :] = x[rb - 8:rb, :]
    gate_r = _sigmoid(_dot(conv, wa_ref[...]) + ba)
    gate_i = _sigmoid(_dot(conv, wx_ref[...]) + bx)
    log_a = -LRU_C * gate_r * _softplus(-lam)
    a = jnp.exp(log_a)
    inp = jnp.sqrt(-_expm1(2.0 * log_a)) * (gate_i * conv)
    row = lax.broadcasted_iota(jnp.int32, (rb, 1), 0) % CHUNK
    d = 1
    while d < CHUNK:
        m = row >= d
        inp = jnp.where(m, a * pltpu.roll(inp, d, 0) + inp, inp)
        a = jnp.where(m, a * pltpu.roll(a, d, 0), a)
        d *= 2
    h_prev, hs = h_ref[...], []
    for n in range(rb // CHUNK):
        seg = slice(n * CHUNK, (n + 1) * CHUNK)
        hs.append(inp[seg] + a[seg] * h_prev)
        h_prev = hs[-1][CHUNK - 1:CHUNK, :]
    h = jnp.concatenate(hs, axis=0)
    h_ref[...] = h_prev
    g = g_ref[0]
    gelu = 0.5 * g * (1.0 + jnp.tanh(math.sqrt(2.0 / math.pi) * (g + 0.044715 * (g * g * g))))
    o_ref[0] = (h * gelu).astype(BF16)


def _lru(pd3, par, wa_bd, wx_bd, rb):
    b, s, _ = pd3.shape
    full = lambda shp: pl.BlockSpec(shp, lambda bi, i: (0,) * len(shp))
    return pl.pallas_call(
        functools.partial(_lru_body, rb=rb),
        grid=(b, s // rb),
        in_specs=[pl.BlockSpec((1, rb, GW), lambda bi, i: (bi, i, 0)),
                  pl.BlockSpec((1, rb, GW), lambda bi, i: (bi, i, 1)),
                  full((8, GW)), full((GW, GW)), full((GW, GW))],
        out_specs=pl.BlockSpec((1, rb, GW), lambda bi, i: (bi, i, 0)),
        out_shape=jax.ShapeDtypeStruct((b, s, GW), BF16),
        scratch_shapes=[pltpu.VMEM((rb + 8, GW), F32), pltpu.VMEM((1, GW), F32)],
        compiler_params=_cparams("parallel", "arbitrary"),
        name="lru",
    )(pd3, pd3, par, wa_bd, wx_bd)


def _outmlp_body(ya_ref, yb_ref, yc_ref, yd_ref, x_ref, wo_ref, gains_ref, w1_ref, w2_ref, o_ref,
                 xm_ref, h_ref, acc_ref):
    kk = pl.program_id(1)

    @pl.when(kk == 0)
    def _():
        mix = jnp.concatenate([ya_ref[...], yb_ref[...], yc_ref[...], yd_ref[...]], axis=1)
        xm = x_ref[...] + _rms(jnp.dot(mix, wo_ref[...], preferred_element_type=F32), gains_ref[0:1, :])
        xm_ref[...] = xm
        h_ref[...] = _rms(xm, gains_ref[1:2, :]).astype(BF16)
        acc_ref[...] = jnp.zeros_like(acc_ref)

    a = jnp.dot(h_ref[...], w1_ref[...], preferred_element_type=F32)
    a = jnp.square(jnp.maximum(a, 0.0)).astype(BF16)
    acc_ref[...] += jnp.dot(a, w2_ref[...], preferred_element_type=F32)

    @pl.when(kk == pl.num_programs(1) - 1)
    def _():
        o_ref[...] = xm_ref[...] + _rms(acc_ref[...], gains_ref[2:3, :])


def _outmlp(ya, yb, yc, yd, x2, wo_all, gains, w1_all, w2_all, layer, tm, tk):
    t, d = x2.shape
    hid = w1_all.shape[2]
    mix = pl.BlockSpec((tm, GW), lambda i, k: (i, 0))
    return pl.pallas_call(
        _outmlp_body,
        grid=(t // tm, hid // tk),
        in_specs=[mix, mix, mix, mix,
                  pl.BlockSpec((tm, d), lambda i, k: (i, 0)),
                  pl.BlockSpec((None, 4 * GW, d), lambda i, k: (layer, 0, 0)),
                  pl.BlockSpec((8, d), lambda i, k: (0, 0)),
                  pl.BlockSpec((None, d, tk), lambda i, k: (layer, 0, k)),
                  pl.BlockSpec((None, tk, d), lambda i, k: (layer, k, 0))],
        out_specs=pl.BlockSpec((tm, d), lambda i, k: (i, 0)),
        out_shape=jax.ShapeDtypeStruct((t, d), F32),
        scratch_shapes=[pltpu.VMEM((tm, d), F32), pltpu.VMEM((tm, d), BF16), pltpu.VMEM((tm, d), F32)],
        compiler_params=_cparams("parallel", "arbitrary"),
        name="outmlp",
    )(ya, yb, yc, yd, x2, wo_all, gains, w1_all, w2_all)


def _rows(vectors, n_rows):
    tab = jnp.stack([v.astype(F32) for v in vectors])
    return jnp.pad(tab, ((0, n_rows - tab.shape[0]), (0, 0)))


def _block_diag(w):
    out = jnp.zeros((GW, GW), w.dtype)
    for n in range(w.shape[0]):
        out = out.at[n * HEAD_DIM:(n + 1) * HEAD_DIM, n * HEAD_DIM:(n + 1) * HEAD_DIM].set(w[n])
    return out


def _pick_block(n, want):
    while n % want:
        want //= 2
    return want


def kernel(x, norm_mix_pre, norm_mix_post, norm_mlp_pre, norm_mlp_post, w_in, w_out, attn_rel_bias, hgrn_lb_logits, hgrn_norm, rwkv_mu, rwkv_w0, rwkv_w2, rwkv_a0, rwkv_a2, rwkv_g2, rwkv_k_k, rwkv_k_a, rwkv_r_k, rwkv_ln_w, rwkv_ln_b, lru_conv_w, lru_conv_b, lru_wa, lru_ba, lru_wx, lru_bx, lru_lambda, mlp_w1, mlp_w2):
    b, s, d = x.shape
    depth = w_in.shape[0]
    t = b * s
    tm = _pick_block(t, 512)
    tm_mlp = _pick_block(t, 1024)
    rb = _pick_block(s, 512)
    rb_big = _pick_block(s, 1024)
    tk = _pick_block(mlp_w1.shape[-1], 1024)

    lb_sm = jax.nn.softmax(hgrn_lb_logits.astype(F32), axis=0)
    lb_all = jnp.maximum(jnp.cumsum(lb_sm, axis=0) - lb_sm[0:1], 0.0)

    w_in_b, w_out_b = w_in.astype(BF16), w_out.astype(BF16)
    w1_b, w2_b = mlp_w1.astype(BF16), mlp_w2.astype(BF16)

    x2 = x.reshape(t, d)
    for l in range(depth):
        pa, pb, pc, pd = _inproj(x2, norm_mix_pre[l].reshape(1, d), w_in_b, l, tm)

        ya = _attention(pa.reshape(b, s, ATTN_COLS), _attn_bias_table(attn_rel_bias[l]), rb_big)

        lb = lb_all[l]
        hgrn_par = _rows([lb, jnp.log1p(-lb), 1.0 - lb, hgrn_norm[l]], 8)
        yb = _hgrn(pb.reshape(b, s, HGRN_COLS), hgrn_par, rb_big)

        mu = rwkv_mu[l].astype(F32)
        mu_lo = jnp.pad(mu[3 * GW:], (0, GW - RWKV_LORA))
        rwkv_par = _rows([mu[0:GW], mu[GW:2 * GW], mu[2 * GW:3 * GW], mu_lo, rwkv_w0[l], rwkv_a0[l],
                          rwkv_k_k[l], rwkv_k_a[l], rwkv_r_k[l].reshape(GW), rwkv_ln_w[l], rwkv_ln_b[l]], 16)
        zeros = lambda n: jnp.zeros((n, GW), F32)
        w2p = jnp.concatenate([rwkv_w2[l].astype(F32), zeros(96)], axis=0).astype(BF16)
        a2p = jnp.concatenate([zeros(32), rwkv_a2[l].astype(F32), zeros(64)], axis=0).astype(BF16)
        g2p = jnp.concatenate([zeros(64), rwkv_g2[l].astype(F32)], axis=0).astype(BF16)
        yc = _rwkv(pc.reshape(b, s, RWKV_COLS), rwkv_par, w2p, a2p, g2p, rb)

        cw = lru_conv_w[l].astype(F32)
        lru_par = _rows([cw[0], cw[1], cw[2], cw[3], lru_conv_b[l], lru_ba[l], lru_bx[l], lru_lambda[l]], 8)
        yd = _lru(pd.reshape(b, s, LRU_COLS), lru_par, _block_diag(lru_wa[l]).astype(BF16),
                  _block_diag(lru_wx[l]).astype(BF16), rb)

        gains = jnp.pad(jnp.stack([norm_mix_post[l], norm_mlp_pre[l], norm_mlp_post[l]]).astype(F32),
                        ((0, 5), (0, 0)))
        x2 = _outmlp(ya.reshape(t, GW), yb.reshape(t, GW), yc.reshape(t, GW), yd.reshape(t, GW),
                     x2, w_out_b, gains, w1_b, w2_b, l, tm_mlp, tk)
    return x2.reshape(b, s, d)
```

```python
import functools
import math

import numpy as np

import jax
import jax.numpy as jnp
from jax import lax
from jax.experimental import pallas as pl
from jax.experimental.pallas import tpu as pltpu

F32 = jnp.float32
BF16 = jnp.bfloat16

GW = 256
HEAD_DIM = 64
N_HEADS = 4
CHUNK = 64
ATTN_LEFT_CHUNKS = 8
BAND = (ATTN_LEFT_CHUNKS + 1) * CHUNK
ATTN_PAD = ATTN_LEFT_CHUNKS * CHUNK
REL_CLIP = 256
ATTN_SCALE = HEAD_DIM ** -0.5
NEG_INF = -1e30
RMS_EPS = 1e-6
RWKV_GN_EPS = HEAD_DIM * 1e-5
RWKV_LORA = 128
LRU_C = 8.0
LRU_CONV = 4
HGRN_SUB = 16

ATTN_COLS = 3 * GW
HGRN_COLS = 4 * GW
RWKV_COLS = 3 * GW + RWKV_LORA
LRU_COLS = 2 * GW

VMEM_LIMIT = 56 * 1024 * 1024
RWKV_PASSES = (1, (3, 3, 3, 1, 1), 1)
RWKV_MAP_BATCH = 8
ATTN_BATCH = 8


def _cparams(*sem):
    return pltpu.CompilerParams(dimension_semantics=sem, vmem_limit_bytes=VMEM_LIMIT)


def _dot(a, b):
    return jnp.dot(a.astype(BF16), b.astype(BF16), preferred_element_type=F32)


def _dot_nt(a, b):
    return lax.dot_general(a.astype(BF16), b.astype(BF16), (((1,), (1,)), ((), ())),
                           preferred_element_type=F32)


def _dot_tn(a, b):
    return lax.dot_general(a.astype(BF16), b.astype(BF16), (((0,), (0,)), ((), ())),
                           preferred_element_type=F32)


def _split(a):
    hi = a.astype(BF16)
    lo = (a - hi.astype(F32)).astype(BF16)
    return hi, lo


def _mm(a, b, kind="nn", passes=1):
    f = {"nn": _dot, "nt": _dot_nt, "tn": _dot_tn}[kind]
    if passes == 1:
        return f(a, b)
    ah, al = _split(a)
    bh, bl = _split(b)
    if kind == "tn":
        return f(ah, bh) + (f(ah, bl) + f(al, bh))
    m = a.shape[0]
    top = f(jnp.concatenate([ah, al], axis=0), bh)
    return top[0:m] + (f(ah, bl) + top[m:])


def _rms(x, gain):
    return x * lax.rsqrt(jnp.mean(x * x, axis=-1, keepdims=True) + RMS_EPS) * gain


def _sigmoid(x):
    return 1.0 / (1.0 + jnp.exp(-x))


def _softplus(x):
    return jnp.maximum(x, 0.0) + jnp.log1p(jnp.exp(-jnp.abs(x)))


def _log1p_exp_neg_abs(x):
    return jnp.log(1.0 + jnp.exp(-jnp.abs(x)))


def _expm1(z):
    u = jnp.exp(z)
    um1 = u - 1.0
    near = um1 * z / jnp.where(u == 1.0, 1.0, jnp.log(u))
    return jnp.where(jnp.abs(z) > 0.5, um1, jnp.where(u == 1.0, z, near))


def _head_block_mask(n):
    r = lax.broadcasted_iota(jnp.int32, (n, n), 0) // HEAD_DIM
    c = lax.broadcasted_iota(jnp.int32, (n, n), 1) // HEAD_DIM
    return r == c


def _cumsum_rows(x):
    n = x.shape[0]
    row = lax.broadcasted_iota(jnp.int32, (n, 1), 0)
    d = 1
    while d < n:
        x = x + jnp.where(row >= d, pltpu.roll(x, d, 0), 0.0)
        d *= 2
    return x


def _stack_heads(x):
    lane_head = lax.broadcasted_iota(jnp.int32, (1, GW), 1) // HEAD_DIM
    return jnp.concatenate([jnp.where(lane_head == h, x, 0.0) for h in range(N_HEADS)], axis=0)


def _unstack_heads(xbd, c):
    return xbd[0:c] + xbd[c:2 * c] + xbd[2 * c:3 * c] + xbd[3 * c:4 * c]


def _inproj_body(x_ref, g_ref, w_ref, oa_ref, ob_ref, oc_ref, od_ref):
    h = _rms(x_ref[...], g_ref[...]).astype(BF16)
    o0, o1, o2 = ATTN_COLS, ATTN_COLS + HGRN_COLS, ATTN_COLS + HGRN_COLS + RWKV_COLS
    oa_ref[...] = jnp.dot(h, w_ref[:, 0:o0], preferred_element_type=F32).astype(BF16)
    ob_ref[...] = jnp.dot(h, w_ref[:, o0:o1], preferred_element_type=F32)
    oc_ref[...] = jnp.dot(h, w_ref[:, o1:o2], preferred_element_type=F32)
    od_ref[...] = jnp.dot(h, w_ref[:, o2:], preferred_element_type=F32)


def _inproj(x2, gain, w_all, layer, tm):
    t, d = x2.shape
    d_in = w_all.shape[2]
    return pl.pallas_call(
        _inproj_body,
        grid=(t // tm,),
        in_specs=[pl.BlockSpec((tm, d), lambda i: (i, 0)),
                  pl.BlockSpec((1, d), lambda i: (0, 0)),
                  pl.BlockSpec((None, d, d_in), lambda i: (layer, 0, 0))],
        out_specs=[pl.BlockSpec((tm, ATTN_COLS), lambda i: (i, 0)),
                   pl.BlockSpec((tm, HGRN_COLS), lambda i: (i, 0)),
                   pl.BlockSpec((tm, RWKV_COLS), lambda i: (i, 0)),
                   pl.BlockSpec((tm, LRU_COLS), lambda i: (i, 0))],
        out_shape=[jax.ShapeDtypeStruct((t, ATTN_COLS), BF16),
                   jax.ShapeDtypeStruct((t, HGRN_COLS), F32),
                   jax.ShapeDtypeStruct((t, RWKV_COLS), F32),
                   jax.ShapeDtypeStruct((t, LRU_COLS), F32)],
        compiler_params=_cparams("parallel"),
        name="inproj",
    )(x2, gain, w_all)


def _attn_body(q_ref, k_ref, v_ref, bias_ref, o_ref, kp_ref, vp_ref, *, chunks):
    i = pl.program_id(1)

    @pl.when(i == 0)
    def _():
        for src, dst in ((k_ref, kp_ref), (v_ref, vp_ref)):
            dst[0:ATTN_PAD, :] = jnp.zeros((ATTN_PAD, GW), BF16)
            dst[ATTN_PAD:, :] = src[0]

    lane_head = lax.broadcasted_iota(jnp.int32, (1, GW), 1) // HEAD_DIM
    kpos = lax.broadcasted_iota(jnp.int32, (1, BAND), 1)

    nb = ATTN_BATCH

    def chunk_group(jj, carry):
        cs = [i * chunks + jj * nb + n for n in range(nb)]
        rows = [pl.ds(pl.multiple_of((jj * nb + n) * CHUNK, CHUNK), CHUNK) for n in range(nb)]
        wins = [pl.ds(pl.multiple_of(c * CHUNK, CHUNK), BAND) for c in cs]
        ss = []
        for c, rw, win in zip(cs, rows, wins):
            q = q_ref[0, rw, :] * ATTN_SCALE
            qbd = jnp.concatenate([jnp.where(lane_head == h, q, jnp.zeros_like(q)) for h in range(N_HEADS)],
                                  axis=0)
            s = _dot_nt(qbd, kp_ref[win, :]) + bias_ref[...]
            ss.append(jnp.where(kpos >= (ATTN_LEFT_CHUNKS - c) * CHUNK, s, NEG_INF))
        ps = [jnp.exp(s - jnp.max(s, axis=-1, keepdims=True)) for s in ss]
        for rw, win, p in zip(rows, wins, ps):
            inv_l = 1.0 / jnp.sum(p, axis=-1, keepdims=True)
            obd = _dot(p, vp_ref[win, :]) * inv_l
            o = jnp.zeros((CHUNK, GW), F32)
            for h in range(N_HEADS):
                o = o + jnp.where(lane_head == h, obd[h * CHUNK:(h + 1) * CHUNK, :], 0.0)
            o_ref[0, rw, :] = o.astype(BF16)
        return carry

    lax.fori_loop(0, chunks // nb, chunk_group, 0)


def _attention(pa3, bias, rb):
    b, s, _ = pa3.shape
    return pl.pallas_call(
        functools.partial(_attn_body, chunks=rb // CHUNK),
        grid=(b, s // rb),
        in_specs=[pl.BlockSpec((1, rb, GW), lambda bi, i: (bi, i, 0)),
                  pl.BlockSpec((1, s, GW), lambda bi, i: (bi, 0, 1)),
                  pl.BlockSpec((1, s, GW), lambda bi, i: (bi, 0, 2)),
                  pl.BlockSpec((N_HEADS * CHUNK, BAND), lambda bi, i: (0, 0))],
        out_specs=pl.BlockSpec((1, rb, GW), lambda bi, i: (bi, i, 0)),
        out_shape=jax.ShapeDtypeStruct((b, s, GW), BF16),
        scratch_shapes=[pltpu.VMEM((s + ATTN_PAD, GW), BF16)] * 2,
        compiler_params=_cparams("parallel", "arbitrary"),
        name="attn",
    )(pa3, pa3, pa3, bias)


def _attn_bias_table(rel_bias):
    rel = np.arange(BAND + CHUNK - 1) - (CHUNK - 1) - ATTN_LEFT_CHUNKS * CHUNK
    ext = rel_bias.astype(F32)[:, np.clip(rel, -REL_CLIP, REL_CLIP) + REL_CLIP]
    tab = jnp.stack([ext[:, CHUNK - 1 - q:CHUNK - 1 - q + BAND] for q in range(CHUNK)], axis=1)
    return tab.reshape(N_HEADS * CHUNK, BAND)


def _hgrn_body(q_ref, f_ref, i_ref, g_ref, par_ref, o_ref, st_ref, oacc_ref, q_s, key_s, cum16_s, cum64_s,
               ckey_s, *, rb):
    @pl.when(pl.program_id(1) == 0)
    def _():
        st_ref[...] = jnp.zeros_like(st_ref)

    m = HGRN_SUB
    lb = par_ref[0:1, :]
    log_1m_lb = par_ref[1:2, :]
    one_m_lb = par_ref[2:3, :]
    gain = par_ref[3:4, :]
    bd = _head_block_mask(GW)
    ones_bd = jnp.where(bd, 1.0, 0.0).astype(BF16)
    trow = lax.broadcasted_iota(jnp.int32, (m, 1), 0)

    fr = f_ref[0]
    e = jnp.exp(-jnp.abs(fr))
    inv_1pe = 1.0 / (1.0 + e)
    sig = jnp.where(fr >= 0.0, inv_1pe, e * inv_1pe)
    log_1pe = jnp.log(1.0 + e)
    log_sig = jnp.minimum(fr, 0.0) - log_1pe
    log_key = log_1m_lb - jnp.maximum(fr, 0.0) - log_1pe
    log_f = jnp.maximum(jnp.log(lb + one_m_lb * sig), log_1m_lb + log_sig)
    row = lax.broadcasted_iota(jnp.int32, (rb, 1), 0) % CHUNK
    cum, d = log_f, 1
    while d < CHUNK:
        cum = cum + jnp.where(row >= d, pltpu.roll(cum, d, 0), 0.0)
        d *= 2
    cum64_s[...] = cum
    before = jnp.where(row == 0, 0.0, pltpu.roll(cum, 1, 0)).reshape(rb // m, m, GW)[:, 0:1, :]
    cum16 = cum - jnp.broadcast_to(before, (rb // m, m, GW)).reshape(rb, GW)
    cum16_s[...] = cum16
    ckey_s[...] = cum16 - log_key
    key_s[...] = one_m_lb * jnp.where(fr >= 0.0, e * inv_1pe, inv_1pe)
    qr = q_ref[0]
    q_s[...] = qr * _sigmoid(qr)

    nsub = CHUNK // m

    def chunk(j, carry):
        rows = pl.ds(pl.multiple_of(j * CHUNK, CHUNK), CHUNK)
        q, key, v = q_s[rows, :], key_s[rows, :], i_ref[0, rows, :]
        c16, c64, ck = cum16_s[rows, :], cum64_s[rows, :], ckey_s[rows, :]
        sub = lambda x, i: x[i * m:(i + 1) * m, :]
        st = st_ref[...]
        o = _dot_nt(q * jnp.exp(c64), st)
        a_exp = []
        for i in range(nsub):
            qi, ci, cki = sub(q, i), sub(c16, i), sub(ck, i)
            ps = []
            for s in range(m):
                ps.append(qi * jnp.exp(jnp.where(trow >= s, ci - cki[s:s + 1, :], NEG_INF)))
            a_exp.append(_dot(jnp.concatenate(ps, axis=0), ones_bd))
        o_sub = []
        for i in range(nsub):
            vi, oi = sub(v, i), sub(o, i)
            for s in range(m):
                oi = oi + a_exp[i][s * m:(s + 1) * m, :] * vi[s:s + 1, :]
            o_sub.append(oi)
        q_hat = q * jnp.exp(c16)
        k_hat = [sub(key, i) * jnp.exp(sub(c16, i)[m - 1:m, :] - sub(c16, i)) for i in range(nsub)]
        b_start = [sub(c64, i)[0:1, :] - sub(c16, i)[0:1, :] for i in range(nsub)]
        b_end = [sub(c64, i)[m - 1:m, :] for i in range(nsub)]
        v_bd = [_stack_heads(sub(v, i)) for i in range(nsub)]
        for i in range(1, nsub):
            k_bd = jnp.concatenate([_stack_heads(k_hat[jj] * jnp.exp(b_start[i] - b_end[jj])) for jj in range(i)],
                                   axis=0)
            a_cat = _dot_nt(sub(q_hat, i), k_bd)
            o_sub[i] = o_sub[i] + _dot(a_cat, jnp.concatenate(v_bd[0:i], axis=0))
        oacc_ref[rows, :] = jnp.concatenate(o_sub, axis=0)
        last = c64[CHUNK - 1:CHUNK, :]
        st_ref[...] = st * jnp.exp(last) + jnp.where(bd, _dot_tn(v, key * jnp.exp(last - c64)), 0.0)
        return carry

    lax.fori_loop(0, rb // CHUNK, chunk, 0)
    o = oacc_ref[...]
    ms = _dot(o * o, ones_bd) * (1.0 / HEAD_DIM)
    g = g_ref[0]
    o_ref[0] = (o * lax.rsqrt(ms + RMS_EPS) * gain * (g * _sigmoid(g))).astype(BF16)


def _hgrn(pb3, par, rb):
    b, s, _ = pb3.shape
    col = lambda n: pl.BlockSpec((1, rb, GW), lambda bi, i, n=n: (bi, i, n))
    return pl.pallas_call(
        functools.partial(_hgrn_body, rb=rb),
        grid=(b, s // rb),
        in_specs=[col(0), col(1), col(2), col(3),
                  pl.BlockSpec((8, GW), lambda bi, i: (0, 0))],
        out_specs=pl.BlockSpec((1, rb, GW), lambda bi, i: (bi, i, 0)),
        out_shape=jax.ShapeDtypeStruct((b, s, GW), BF16),
        scratch_shapes=[pltpu.VMEM((GW, GW), F32)] + [pltpu.VMEM((rb, GW), F32)] * 6,
        compiler_params=_cparams("parallel", "arbitrary"),
        name="hgrn",
    )(pb3, pb3, pb3, pb3, par)


def _expand_heads(x_cat, bd):
    return jnp.where(bd, jnp.concatenate([x_cat] * N_HEADS, axis=0), 0.0)


def _mm_cat(a, x_cat, bd, passes):
    if passes == 1:
        return _dot(a, _expand_heads(x_cat.astype(BF16), bd))
    ah, al = _split(a)
    xh, xl = _split(x_cat)
    m = a.shape[0]
    top = _dot(jnp.concatenate([ah, al], axis=0), _expand_heads(xh, bd))
    return top[0:m] + (_dot(ah, _expand_heads(xl, bd)) + top[m:])


def _rwkv_chunk_maps(chunks, bd, strict_cat, incl_cat, eye_cat):
    c = CHUNK
    p_pair, p_inv, p_app = RWKV_PASSES
    pre = []
    for r, logw, k, v, alpha, beta in chunks:
        cum = _cumsum_rows(logw)
        e_last = jnp.exp(cum[c - 1:c, :])
        a_t = alpha * jnp.exp(cum - logw)
        r_t = r * jnp.exp(cum)
        e_inv = jnp.exp(-cum)
        b_t = beta * e_inv
        k_t = k * e_inv
        ar = jnp.concatenate([a_t, r_t], axis=0)
        bbd, kbd, vbd = _stack_heads(b_t), _stack_heads(k_t), _stack_heads(v)
        pair = _mm(ar, jnp.concatenate([bbd, kbd], axis=0), "nt", p_pair)
        l_ab = jnp.where(strict_cat, pair[0:c, 0:GW], 0.0)
        lm_k = jnp.where(jnp.concatenate([strict_cat, incl_cat], axis=0), pair[:, GW:], 0.0)
        m_rb = jnp.where(incl_cat, pair[c:, 0:GW], 0.0)
        wy0 = _mm(lm_k, vbd, "nn", p_app)
        pre.append((e_last, a_t, r_t, b_t, k_t, v, l_ab, m_rb, wy0))
    xs = [p[6] for p in pre]
    invs = [eye_cat + x for x in xs]
    xs = [_mm_cat(x, x, bd, p_inv[0]) for x in xs]
    for n in range(len(p_inv)):
        last = n == len(p_inv) - 1
        p_step = p_inv[n] if last else max(p_inv[n], p_inv[n + 1])
        if last:
            invs = [inv + _mm_cat(inv, x, bd, p_step) for inv, x in zip(invs, xs)]
        else:
            both = [_mm_cat(jnp.concatenate([inv, x], axis=0), x, bd, p_step) for inv, x in zip(invs, xs)]
            invs = [inv + bo[0:c] for inv, bo in zip(invs, both)]
            xs = [bo[c:] for bo in both]
    outs = []
    for (e_last, a_t, r_t, b_t, k_t, v, l_ab, m_rb, wy0), inv in zip(pre, invs):
        sol = _mm(inv, jnp.concatenate([_stack_heads(a_t), _stack_heads(wy0[0:c])], axis=1), "nn", p_app)
        a_s, u0 = sol[:, 0:GW], sol[:, GW:]
        out = _mm(m_rb, jnp.concatenate([_stack_heads(a_s), _stack_heads(u0)], axis=1), "nn", p_app)
        ry = r_t + out[:, 0:GW]
        y0 = wy0[c:] + out[:, GW:]
        bh, kh = b_t * e_last, k_t * e_last
        g = jnp.where(bd, _mm(a_s, bh, "tn", p_app), 0.0)
        c0 = jnp.where(bd, _mm(jnp.concatenate([u0, v], axis=0), jnp.concatenate([bh, kh], axis=0), "tn", p_app),
                       0.0)
        outs.append((ry, y0, g, c0, e_last))
    return outs


def _rwkv_body(pc_ref, par_ref, w2_ref, a2_ref, g2_ref, o_ref, ht_ref, prev_ref,
               r_s, w_s, k_s, v_s, al_s, be_s, ry_s, y_s, bon_s, gate_s, g_s, c_s, el_s, *, rb, nseq):
    @pl.when(pl.program_id(1) == 0)
    def _():
        ht_ref[...] = jnp.zeros_like(ht_ref)
        prev_ref[...] = jnp.zeros_like(prev_ref)

    c = CHUNK
    mu_main = [par_ref[n:n + 1, :] for n in range(3)]
    mu_lo = par_ref[3:4, 0:RWKV_LORA]
    w0, a0 = par_ref[4:5, :], par_ref[5:6, :]
    k_k, k_a, r_k = par_ref[6:7, :], par_ref[7:8, :], par_ref[8:9, :]
    ln_w, ln_b = par_ref[9:10, :], par_ref[10:11, :]
    bd = _head_block_mask(GW)
    ones_bd = jnp.where(bd, 1.0, 0.0).astype(BF16)

    row0 = lax.broadcasted_iota(jnp.int32, (rb, 1), 0) == 0
    for q in range(nseq):
        blk = pl.ds(q * rb, rb)
        pc = pc_ref[q]
        prev = jnp.where(row0, prev_ref[q:q + 1, :], pltpu.roll(pc, 1, 0))
        prev_ref[q:q + 1, :] = pc[rb - 1:rb, :]
        xs = [pc[:, n * GW:(n + 1) * GW] for n in range(3)]
        ps = [prev[:, n * GW:(n + 1) * GW] for n in range(3)]
        r, k, v = [x + mu * (p - x) for x, p, mu in zip(xs, ps, mu_main)]
        lo, plo = pc[:, 3 * GW:], prev[:, 3 * GW:]
        lo = lo + mu_lo * (plo - lo)
        w_in = -(w0 + _dot(jnp.tanh(lo), w2_ref[...]))
        w_pre = -(jnp.maximum(w_in, 0.0) + _log1p_exp_neg_abs(w_in)) - 0.5
        a = _sigmoid(a0 + _dot(lo, a2_ref[...]))
        g = _dot(_sigmoid(lo), g2_ref[...])
        kk = k * k_k
        kk = kk / jnp.maximum(jnp.sqrt(_dot(kk * kk, ones_bd)), 1e-12)
        k = k * (1.0 + (a - 1.0) * k_a)
        r_s[blk, :] = r
        w_s[blk, :] = -jnp.exp(w_pre)
        k_s[blk, :] = k
        v_s[blk, :] = v
        al_s[blk, :] = -kk
        be_s[blk, :] = kk * a
        bon_s[blk, :] = _dot(r * k * r_k, ones_bd) * v
        gate_s[blk, :] = g

    tt = lax.broadcasted_iota(jnp.int32, (c, GW), 0)
    ss = lax.broadcasted_iota(jnp.int32, (c, GW), 1) % c
    strict_cat, incl_cat = tt > ss, tt >= ss
    eye_cat = jnp.where(tt == ss, 1.0, 0.0)

    nb = RWKV_MAP_BATCH

    def chunk_maps(jj, carry):
        js = [jj * nb + n for n in range(nb)]
        rows = [pl.ds(pl.multiple_of(j * c, c), c) for j in js]
        maps = _rwkv_chunk_maps([(r_s[rw, :], w_s[rw, :], k_s[rw, :], v_s[rw, :], al_s[rw, :], be_s[rw, :])
                                 for rw in rows], bd, strict_cat, incl_cat, eye_cat)
        for j, rw, (ry, y0, g, c0, e_last) in zip(js, rows, maps):
            ry_s[rw, :] = ry
            y_s[rw, :] = y0
            g_s[j] = g.astype(BF16)
            c_s[j] = c0
            el_s[j] = jnp.broadcast_to(e_last, (8, GW))
        return carry

    lax.fori_loop(0, nseq * rb // (c * nb), chunk_maps, 0)

    nchunk = rb // c

    def chunk_apply(j, carry):
        for q in range(nseq):
            jq = q * nchunk + j
            rows = pl.ds(pl.multiple_of(jq * c, c), c)
            ht = ht_ref[q]
            htb = ht.astype(BF16)
            y_s[rows, :] = y_s[rows, :] + _dot_nt(ry_s[rows, :], htb)
            ht_ref[q] = ht * el_s[jq][0:1, :] + jnp.dot(htb, g_s[jq], preferred_element_type=F32) + c_s[jq]
        return carry

    lax.fori_loop(0, nchunk, chunk_apply, 0, unroll=2)

    y = y_s[...]
    mean = _dot(y, ones_bd) * (1.0 / HEAD_DIM)
    yc = y - mean
    var = _dot(yc * yc, ones_bd) * (1.0 / HEAD_DIM)
    yn = yc * lax.rsqrt(var + RWKV_GN_EPS) * ln_w + ln_b
    o_ref[...] = ((yn + bon_s[...]) * gate_s[...]).astype(BF16).reshape(nseq, rb, GW)


def _rwkv(pc3, par, w2p, a2p, g2p, rb):
    b, s, _ = pc3.shape
    nseq = 2 if b % 2 == 0 else 1
    nchunk = nseq * rb // CHUNK
    full = lambda shp: pl.BlockSpec(shp, lambda bi, i: (0,) * len(shp))
    return pl.pallas_call(
        functools.partial(_rwkv_body, rb=rb, nseq=nseq),
        grid=(b // nseq, s // rb),
        in_specs=[pl.BlockSpec((nseq, rb, RWKV_COLS), lambda bi, i: (bi, i, 0)),
                  full((16, GW)), full((RWKV_LORA, GW)), full((RWKV_LORA, GW)), full((RWKV_LORA, GW))],
        out_specs=pl.BlockSpec((nseq, rb, GW), lambda bi, i: (bi, i, 0)),
        out_shape=jax.ShapeDtypeStruct((b, s, GW), BF16),
        scratch_shapes=[pltpu.VMEM((nseq, GW, GW), F32), pltpu.VMEM((nseq, RWKV_COLS), F32)]
        + [pltpu.VMEM((nseq * rb, GW), F32)] * 10
        + [pltpu.VMEM((nchunk, GW, GW), BF16), pltpu.VMEM((nchunk, GW, GW), F32),
           pltpu.VMEM((nchunk, 8, GW), F32)],
        compiler_params=_cparams("parallel", "arbitrary"),
        name="rwkv",
    )(pc3, par, w2p, a2p, g2p)


def _lru_body(x_ref, g_ref, par_ref, wa_ref, wx_ref, o_ref, ext_ref, h_ref, *, rb):
    @pl.when(pl.program_id(1) == 0)
    def _():
        ext_ref[0:8, :] = jnp.zeros((8, GW), F32)
        h_ref[...] = jnp.zeros_like(h_ref)

    conv_b, ba, bx, lam = par_ref[4:5, :], par_ref[5:6, :], par_ref[6:7, :], par_ref[7:8, :]
    x = x_ref[0]
    ext_ref[8:8 + rb, :] = x
    conv = conv_b + par_ref[0:1, :] * ext_ref[pl.ds(8 - 3, rb), :]
    for j in range(1, LRU_CONV):
        conv = conv + par_ref[j:j + 1, :] * ext_ref[pl.ds(8 - 3 + j, rb), :]
    ext_ref[0:8, :] = x[rb - 8:rb, :]
    gate_r = _sigmoid(_dot(conv, wa_ref[...]) + ba)
    gate_i = _sigmoid(_dot(conv, wx_ref[...]) + bx)
    log_a = -LRU_C * gate_r * _softplus(-lam)
    a = jnp.exp(log_a)
    inp = jnp.sqrt(-_expm1(2.0 * log_a)) * (gate_i * conv)
    row = lax.broadcasted_iota(jnp.int32, (rb, 1), 0) % CHUNK
    d = 1
    while d < CHUNK:
        m = row >= d
        inp = jnp.where(m, a * pltpu.roll(inp, d, 0) + inp, inp)
        a = jnp.where(m, a * pltpu.roll(a, d, 0), a)
        d *= 2
    h_prev, hs = h_ref[...], []
    for n in range(rb // CHUNK):
        seg = slice(n * CHUNK, (n + 1) * CHUNK)
        hs.append(inp[seg] + a[seg] * h_prev)
        h_prev = hs[-1][CHUNK - 1:CHUNK, :]
    h = jnp.concatenate(hs, axis=0)
    h_ref[...] = h_prev
    g = g_ref[0]
    gelu = 0.5 * g * (1.0 + jnp.tanh(math.sqrt(2.0 / math.pi) * (g + 0.044715 * (g * g * g))))
    o_ref[0] = (h * gelu).astype(BF16)


def _lru(pd3, par, wa_bd, wx_bd, rb):
    b, s, _ = pd3.shape
    full = lambda shp: pl.BlockSpec(shp, lambda bi, i: (0,) * len(shp))
    return pl.pallas_call(
        functools.partial(_lru_body, rb=rb),
        grid=(b, s // rb),
        in_specs=[pl.BlockSpec((1, rb, GW), lambda bi, i: (bi, i, 0)),
                  pl.BlockSpec((1, rb, GW), lambda bi, i: (bi, i, 1)),
                  full((8, GW)), full((GW, GW)), full((GW, GW))],
        out_specs=pl.BlockSpec((1, rb, GW), lambda bi, i: (bi, i, 0)),
        out_shape=jax.ShapeDtypeStruct((b, s, GW), BF16),
        scratch_shapes=[pltpu.VMEM((rb + 8, GW), F32), pltpu.VMEM((1, GW), F32)],
        compiler_params=_cparams("parallel", "arbitrary"),
        name="lru",
    )(pd3, pd3, par, wa_bd, wx_bd)


def _outmlp_body(ya_ref, yb_ref, yc_ref, yd_ref, x_ref, wo_ref, gains_ref, w1_ref, w2_ref, o_ref,
                 xm_ref, h_ref, acc_ref):
    kk = pl.program_id(1)

    @pl.when(kk == 0)
    def _():
        mix = jnp.concatenate([ya_ref[...], yb_ref[...], yc_ref[...], yd_ref[...]], axis=1)
        xm = x_ref[...] + _rms(jnp.dot(mix, wo_ref[...], preferred_element_type=F32), gains_ref[0:1, :])
        xm_ref[...] = xm
        h_ref[...] = _rms(xm, gains_ref[1:2, :]).astype(BF16)
        acc_ref[...] = jnp.zeros_like(acc_ref)

    a = jnp.dot(h_ref[...], w1_ref[...], preferred_element_type=F32)
    a = jnp.square(jnp.maximum(a, 0.0)).astype(BF16)
    acc_ref[...] += jnp.dot(a, w2_ref[...], preferred_element_type=F32)

    @pl.when(kk == pl.num_programs(1) - 1)
    def _():
        o_ref[...] = xm_ref[...] + _rms(acc_ref[...], gains_ref[2:3, :])


def _outmlp(ya, yb, yc, yd, x2, wo_all, gains, w1_all, w2_all, layer, tm, tk):
    t, d = x2.shape
    hid = w1_all.shape[2]
    mix = pl.BlockSpec((tm, GW), lambda i, k: (i, 0))
    return pl.pallas_call(
        _outmlp_body,
        grid=(t // tm, hid // tk),
        in_specs=[mix, mix, mix, mix,
                  pl.BlockSpec((tm, d), lambda i, k: (i, 0)),
                  pl.BlockSpec((None, 4 * GW, d), lambda i, k: (layer, 0, 0)),
                  pl.BlockSpec((8, d), lambda i, k: (0, 0)),
                  pl.BlockSpec((None, d, tk), lambda i, k: (layer, 0, k)),
                  pl.BlockSpec((None, tk, d), lambda i, k: (layer, k, 0))],
        out_specs=pl.BlockSpec((tm, d), lambda i, k: (i, 0)),
        out_shape=jax.ShapeDtypeStruct((t, d), F32),
        scratch_shapes=[pltpu.VMEM((tm, d), F32), pltpu.VMEM((tm, d), BF16), pltpu.VMEM((tm, d), F32)],
        compiler_params=_cparams("parallel", "arbitrary"),
        name="outmlp",
    )(ya, yb, yc, yd, x2, wo_all, gains, w1_all, w2_all)


def _rows(vectors, n_rows):
    tab = jnp.stack([v.astype(F32) for v in vectors])
    return jnp.pad(tab, ((0, n_rows - tab.shape[0]), (0, 0)))


def _block_diag(w):
    out = jnp.zeros((GW, GW), w.dtype)
    for n in range(w.shape[0]):
        out = out.at[n * HEAD_DIM:(n + 1) * HEAD_DIM, n * HEAD_DIM:(n + 1) * HEAD_DIM].set(w[n])
    return out


def _pick_block(n, want):
    while n % want:
        want //= 2
    return want


def kernel(x, norm_mix_pre, norm_mix_post, norm_mlp_pre, norm_mlp_post, w_in, w_out, attn_rel_bias, hgrn_lb_logits, hgrn_norm, rwkv_mu, rwkv_w0, rwkv_w2, rwkv_a0, rwkv_a2, rwkv_g2, rwkv_k_k, rwkv_k_a, rwkv_r_k, rwkv_ln_w, rwkv_ln_b, lru_conv_w, lru_conv_b, lru_wa, lru_ba, lru_wx, lru_bx, lru_lambda, mlp_w1, mlp_w2):
    b, s, d = x.shape
    depth = w_in.shape[0]
    t = b * s
    tm = _pick_block(t, 512)
    tm_mlp = _pick_block(t, 1024)
    rb = _pick_block(s, 512)
    rb_big = _pick_block(s, 1024)
    tk = _pick_block(mlp_w1.shape[-1], 1024)

    lb_sm = jax.nn.softmax(hgrn_lb_logits.astype(F32), axis=0)
    lb_all = jnp.maximum(jnp.cumsum(lb_sm, axis=0) - lb_sm[0:1], 0.0)

    w_in_b, w_out_b = w_in.astype(BF16), w_out.astype(BF16)
    w1_b, w2_b = mlp_w1.astype(BF16), mlp_w2.astype(BF16)

    x2 = x.reshape(t, d)
    for l in range(depth):
        pa, pb, pc, pd = _inproj(x2, norm_mix_pre[l].reshape(1, d), w_in_b, l, tm)

        ya = _attention(pa.reshape(b, s, ATTN_COLS), _attn_bias_table(attn_rel_bias[l]), rb_big)

        lb = lb_all[l]
        hgrn_par = _rows([lb, jnp.log1p(-lb), 1.0 - lb, hgrn_norm[l]], 8)
        yb = _hgrn(pb.reshape(b, s, HGRN_COLS), hgrn_par, rb_big)

        mu = rwkv_mu[l].astype(F32)
        mu_lo = jnp.pad(mu[3 * GW:], (0, GW - RWKV_LORA))
        rwkv_par = _rows([mu[0:GW], mu[GW:2 * GW], mu[2 * GW:3 * GW], mu_lo, rwkv_w0[l], rwkv_a0[l],
                          rwkv_k_k[l], rwkv_k_a[l], rwkv_r_k[l].reshape(GW), rwkv_ln_w[l], rwkv_ln_b[l]], 16)
        zeros = lambda n: jnp.zeros((n, GW), F32)
        w2p = jnp.concatenate([rwkv_w2[l].astype(F32), zeros(96)], axis=0).astype(BF16)
        a2p = jnp.concatenate([zeros(32), rwkv_a2[l].astype(F32), zeros(64)], axis=0).astype(BF16)
        g2p = jnp.concatenate([zeros(64), rwkv_g2[l].astype(F32)], axis=0).astype(BF16)
        yc = _rwkv(pc.reshape(b, s, RWKV_COLS), rwkv_par, w2p, a2p, g2p, rb)

        cw = lru_conv_w[l].astype(F32)
        lru_par = _rows([cw[0], cw[1], cw[2], cw[3], lru_conv_b[l], lru_ba[l], lru_bx[l], lru_lambda[l]], 8)
        yd = _lru(pd.reshape(b, s, LRU_COLS), lru_par, _block_diag(lru_wa[l]).astype(BF16),
                  _block_diag(lru_wx[l]).astype(BF16), rb)

        gains = jnp.pad(jnp.stack([norm_mix_post[l], norm_mlp_pre[l], norm_mlp_post[l]]).astype(F32),
                        ((0, 5), (0, 0)))
        x2 = _outmlp(ya.reshape(t, GW), yb.reshape(t, GW), yc.reshape(t, GW), yd.reshape(t, GW),
                     x2, w_out_b, gains, w1_b, w2_b, l, tm_mlp, tk)
    return x2.reshape(b, s, d)
```

```python
import functools
import math

import numpy as np

import jax
import jax.numpy as jnp
from jax import lax
from jax.experimental import pallas as pl
from jax.experimental.pallas import tpu as pltpu

F32 = jnp.float32
BF16 = jnp.bfloat16

GW = 256
HEAD_DIM = 64
N_HEADS = 4
CHUNK = 64
ATTN_LEFT_CHUNKS = 8
BAND = (ATTN_LEFT_CHUNKS + 1) * CHUNK
ATTN_PAD = ATTN_LEFT_CHUNKS * CHUNK
REL_CLIP = 256
ATTN_SCALE = HEAD_DIM ** -0.5
NEG_INF = -1e30
RMS_EPS = 1e-6
RWKV_GN_EPS = HEAD_DIM * 1e-5
RWKV_LORA = 128
LRU_C = 8.0
LRU_CONV = 4
HGRN_SUB = 16

ATTN_COLS = 3 * GW
HGRN_COLS = 4 * GW
RWKV_COLS = 3 * GW + RWKV_LORA
LRU_COLS = 2 * GW

VMEM_LIMIT = 56 * 1024 * 1024
RWKV_PASSES = (1, (3, 3, 3, 1, 1), 1)
RWKV_MAP_BATCH = 8
ATTN_BATCH = 8


def _cparams(*sem):
    return pltpu.CompilerParams(dimension_semantics=sem, vmem_limit_bytes=VMEM_LIMIT)


def _dot(a, b):
    return jnp.dot(a.astype(BF16), b.astype(BF16), preferred_element_type=F32)


def _dot_nt(a, b):
    return lax.dot_general(a.astype(BF16), b.astype(BF16), (((1,), (1,)), ((), ())),
                           preferred_element_type=F32)


def _dot_tn(a, b):
    return lax.dot_general(a.astype(BF16), b.astype(BF16), (((0,), (0,)), ((), ())),
                           preferred_element_type=F32)


def _split(a):
    hi = a.astype(BF16)
    lo = (a - hi.astype(F32)).astype(BF16)
    return hi, lo


def _mm(a, b, kind="nn", passes=1):
    f = {"nn": _dot, "nt": _dot_nt, "tn": _dot_tn}[kind]
    if passes == 1:
        return f(a, b)
    ah, al = _split(a)
    bh, bl = _split(b)
    if kind == "tn":
        return f(ah, bh) + (f(ah, bl) + f(al, bh))
    m = a.shape[0]
    top = f(jnp.concatenate([ah, al], axis=0), bh)
    return top[0:m] + (f(ah, bl) + top[m:])


def _rms(x, gain):
    return x * lax.rsqrt(jnp.mean(x * x, axis=-1, keepdims=True) + RMS_EPS) * gain


def _sigmoid(x):
    return 1.0 / (1.0 + jnp.exp(-x))


def _softplus(x):
    return jnp.maximum(x, 0.0) + jnp.log1p(jnp.exp(-jnp.abs(x)))


def _log1p_exp_neg_abs(x):
    return jnp.log(1.0 + jnp.exp(-jnp.abs(x)))


def _head_block_mask(n):
    r = lax.broadcasted_iota(jnp.int32, (n, n), 0) // HEAD_DIM
    c = lax.broadcasted_iota(jnp.int32, (n, n), 1) // HEAD_DIM
    return r == c


def _cumsum_rows(x):
    n = x.shape[0]
    row = lax.broadcasted_iota(jnp.int32, (n, 1), 0)
    d = 1
    while d < n:
        x = x + jnp.where(row >= d, pltpu.roll(x, d, 0), 0.0)
        d *= 2
    return x


def _stack_heads(x):
    lane_head = lax.broadcasted_iota(jnp.int32, (1, GW), 1) // HEAD_DIM
    return jnp.concatenate([jnp.where(lane_head == h, x, 0.0) for h in range(N_HEADS)], axis=0)


def _inproj_body(x_ref, g_ref, w_ref, oa_ref, ob_ref, oc_ref, od_ref):
    h = _rms(x_ref[...], g_ref[...]).astype(BF16)
    o0, o1, o2 = ATTN_COLS, ATTN_COLS + HGRN_COLS, ATTN_COLS + HGRN_COLS + RWKV_COLS
    oa_ref[...] = jnp.dot(h, w_ref[:, 0:o0], preferred_element_type=F32).astype(BF16)
    ob_ref[...] = jnp.dot(h, w_ref[:, o0:o1], preferred_element_type=F32)
    oc_ref[...] = jnp.dot(h, w_ref[:, o1:o2], preferred_element_type=F32)
    od_ref[...] = jnp.dot(h, w_ref[:, o2:], preferred_element_type=F32)


def _inproj(x2, gain, w_all, layer, tm):
    t, d = x2.shape
    d_in = w_all.shape[2]
    return pl.pallas_call(
        _inproj_body,
        grid=(t // tm,),
        in_specs=[pl.BlockSpec((tm, d), lambda i: (i, 0)),
                  pl.BlockSpec((1, d), lambda i: (0, 0)),
                  pl.BlockSpec((None, d, d_in), lambda i: (layer, 0, 0))],
        out_specs=[pl.BlockSpec((tm, ATTN_COLS), lambda i: (i, 0)),
                   pl.BlockSpec((tm, HGRN_COLS), lambda i: (i, 0)),
                   pl.BlockSpec((tm, RWKV_COLS), lambda i: (i, 0)),
                   pl.BlockSpec((tm, LRU_COLS), lambda i: (i, 0))],
        out_shape=[jax.ShapeDtypeStruct((t, ATTN_COLS), BF16),
                   jax.ShapeDtypeStruct((t, HGRN_COLS), F32),
                   jax.ShapeDtypeStruct((t, RWKV_COLS), F32),
                   jax.ShapeDtypeStruct((t, LRU_COLS), F32)],
        compiler_params=_cparams("parallel"),
        name="inproj",
    )(x2, gain, w_all)


def _attn_body(q_ref, k_ref, v_ref, bias_ref, o_ref, kp_ref, vp_ref, *, chunks):
    i = pl.program_id(1)

    @pl.when(i == 0)
    def _():
        for src, dst in ((k_ref, kp_ref), (v_ref, vp_ref)):
            dst[0:ATTN_PAD, :] = jnp.zeros((ATTN_PAD, GW), BF16)
            dst[ATTN_PAD:, :] = src[0]

    lane_head = lax.broadcasted_iota(jnp.int32, (1, GW), 1) // HEAD_DIM
    kpos = lax.broadcasted_iota(jnp.int32, (1, BAND), 1)

    nb = ATTN_BATCH

    def chunk_group(jj, carry):
        cs = [i * chunks + jj * nb + n for n in range(nb)]
        rows = [pl.ds(pl.multiple_of((jj * nb + n) * CHUNK, CHUNK), CHUNK) for n in range(nb)]
        wins = [pl.ds(pl.multiple_of(c * CHUNK, CHUNK), BAND) for c in cs]
        ss = []
        for c, rw, win in zip(cs, rows, wins):
            q = q_ref[0, rw, :] * ATTN_SCALE
            qbd = jnp.concatenate([jnp.where(lane_head == h, q, jnp.zeros_like(q)) for h in range(N_HEADS)],
                                  axis=0)
            s = _dot_nt(qbd, kp_ref[win, :]) + bias_ref[...]
            ss.append(jnp.where(kpos >= (ATTN_LEFT_CHUNKS - c) * CHUNK, s, NEG_INF))
        ps = [jnp.exp(s - jnp.max(s, axis=-1, keepdims=True)) for s in ss]
        for rw, win, p in zip(rows, wins, ps):
            inv_l = 1.0 / jnp.sum(p, axis=-1, keepdims=True)
            obd = _dot(p, vp_ref[win, :]) * inv_l
            o = jnp.zeros((CHUNK, GW), F32)
            for h in range(N_HEADS):
                o = o + jnp.where(lane_head == h, obd[h * CHUNK:(h + 1) * CHUNK, :], 0.0)
            o_ref[0, rw, :] = o.astype(BF16)
        return carry

    lax.fori_loop(0, chunks // nb, chunk_group, 0)


def _attention(pa3, bias, rb):
    b, s, _ = pa3.shape
    return pl.pallas_call(
        functools.partial(_attn_body, chunks=rb // CHUNK),
        grid=(b, s // rb),
        in_specs=[pl.BlockSpec((1, rb, GW), lambda bi, i: (bi, i, 0)),
                  pl.BlockSpec((1, s, GW), lambda bi, i: (bi, 0, 1)),
                  pl.BlockSpec((1, s, GW), lambda bi, i: (bi, 0, 2)),
                  pl.BlockSpec((N_HEADS * CHUNK, BAND), lambda bi, i: (0, 0))],
        out_specs=pl.BlockSpec((1, rb, GW), lambda bi, i: (bi, i, 0)),
        out_shape=jax.ShapeDtypeStruct((b, s, GW), BF16),
        scratch_shapes=[pltpu.VMEM((s + ATTN_PAD, GW), BF16)] * 2,
        compiler_params=_cparams("parallel", "arbitrary"),
        name="attn",
    )(pa3, pa3, pa3, bias)


def _attn_bias_table(rel_bias):
    rel = np.arange(BAND + CHUNK - 1) - (CHUNK - 1) - ATTN_LEFT_CHUNKS * CHUNK
    ext = rel_bias.astype(F32)[:, np.clip(rel, -REL_CLIP, REL_CLIP) + REL_CLIP]
    tab = jnp.stack([ext[:, CHUNK - 1 - q:CHUNK - 1 - q + BAND] for q in range(CHUNK)], axis=1)
    return tab.reshape(N_HEADS * CHUNK, BAND)


def _hgrn_body(q_ref, f_ref, i_ref, g_ref, par_ref, o_ref, st_ref, oacc_ref, q_s, key_s, cum16_s, cum64_s,
               ckey_s, *, rb):
    @pl.when(pl.program_id(1) == 0)
    def _():
        st_ref[...] = jnp.zeros_like(st_ref)

    m = HGRN_SUB
    lb = par_ref[0:1, :]
    log_1m_lb = par_ref[1:2, :]
    one_m_lb = par_ref[2:3, :]
    gain = par_ref[3:4, :]
    bd = _head_block_mask(GW)
    ones_bd = jnp.where(bd, 1.0, 0.0).astype(BF16)
    trow = lax.broadcasted_iota(jnp.int32, (m, 1), 0)

    fr = f_ref[0]
    e = jnp.exp(-jnp.abs(fr))
    inv_1pe = 1.0 / (1.0 + e)
    sig = jnp.where(fr >= 0.0, inv_1pe, e * inv_1pe)
    log_1pe = jnp.log(1.0 + e)
    log_sig = jnp.minimum(fr, 0.0) - log_1pe
    log_key = log_1m_lb - jnp.maximum(fr, 0.0) - log_1pe
    log_f = jnp.maximum(jnp.log(lb + one_m_lb * sig), log_1m_lb + log_sig)
    row = lax.broadcasted_iota(jnp.int32, (rb, 1), 0) % CHUNK
    cum, d = log_f, 1
    while d < CHUNK:
        cum = cum + jnp.where(row >= d, pltpu.roll(cum, d, 0), 0.0)
        d *= 2
    cum64_s[...] = cum
    before = jnp.where(row == 0, 0.0, pltpu.roll(cum, 1, 0)).reshape(rb // m, m, GW)[:, 0:1, :]
    cum16 = cum - jnp.broadcast_to(before, (rb // m, m, GW)).reshape(rb, GW)
    cum16_s[...] = cum16
    ckey_s[...] = cum16 - log_key
    key_s[...] = one_m_lb * jnp.where(fr >= 0.0, e * inv_1pe, inv_1pe)
    qr = q_ref[0]
    q_s[...] = qr * _sigmoid(qr)

    nsub = CHUNK // m

    def chunk(j, carry):
        rows = pl.ds(pl.multiple_of(j * CHUNK, CHUNK), CHUNK)
        q, key, v = q_s[rows, :], key_s[rows, :], i_ref[0, rows, :]
        c16, c64, ck = cum16_s[rows, :], cum64_s[rows, :], ckey_s[rows, :]
        sub = lambda x, i: x[i * m:(i + 1) * m, :]
        st = st_ref[...]
        o = _dot_nt(q * jnp.exp(c64), st)
        a_exp = []
        for i in range(nsub):
            qi, ci, cki = sub(q, i), sub(c16, i), sub(ck, i)
            ps = []
            for s in range(m):
                ps.append(qi * jnp.exp(jnp.where(trow >= s, ci - cki[s:s + 1, :], NEG_INF)))
            a_exp.append(_dot(jnp.concatenate(ps, axis=0), ones_bd))
        o_sub = []
        for i in range(nsub):
            vi, oi = sub(v, i), sub(o, i)
            for s in range(m):
                oi = oi + a_exp[i][s * m:(s + 1) * m, :] * vi[s:s + 1, :]
            o_sub.append(oi)
        q_hat = q * jnp.exp(c16)
        k_hat = [sub(key, i) * jnp.exp(sub(c16, i)[m - 1:m, :] - sub(c16, i)) for i in range(nsub)]
        b_start = [sub(c64, i)[0:1, :] - sub(c16, i)[0:1, :] for i in range(nsub)]
        b_end = [sub(c64, i)[m - 1:m, :] for i in range(nsub)]
        v_bd = [_stack_heads(sub(v, i)) for i in range(nsub)]
        for i in range(1, nsub):
            k_bd = jnp.concatenate([_stack_heads(k_hat[jj] * jnp.exp(b_start[i] - b_end[jj])) for jj in range(i)],
                                   axis=0)
            a_cat = _dot_nt(sub(q_hat, i), k_bd)
            o_sub[i] = o_sub[i] + _dot(a_cat, jnp.concatenate(v_bd[0:i], axis=0))
        oacc_ref[rows, :] = jnp.concatenate(o_sub, axis=0)
        last = c64[CHUNK - 1:CHUNK, :]
        st_ref[...] = st * jnp.exp(last) + jnp.where(bd, _dot_tn(v, key * jnp.exp(last - c64)), 0.0)
        return carry

    lax.fori_loop(0, rb // CHUNK, chunk, 0)
    o = oacc_ref[...]
    ms = _dot(o * o, ones_bd) * (1.0 / HEAD_DIM)
    g = g_ref[0]
    o_ref[0] = (o * lax.rsqrt(ms + RMS_EPS) * gain * (g * _sigmoid(g))).astype(BF16)


def _hgrn(pb3, par, rb):
    b, s, _ = pb3.shape
    col = lambda n: pl.BlockSpec((1, rb, GW), lambda bi, i, n=n: (bi, i, n))
    return pl.pallas_call(
        functools.partial(_hgrn_body, rb=rb),
        grid=(b, s // rb),
        in_specs=[col(0), col(1), col(2), col(3),
                  pl.BlockSpec((8, GW), lambda bi, i: (0, 0))],
        out_specs=pl.BlockSpec((1, rb, GW), lambda bi, i: (bi, i, 0)),
        out_shape=jax.ShapeDtypeStruct((b, s, GW), BF16),
        scratch_shapes=[pltpu.VMEM((GW, GW), F32)] + [pltpu.VMEM((rb, GW), F32)] * 6,
        compiler_params=_cparams("parallel", "arbitrary"),
        name="hgrn",
    )(pb3, pb3, pb3, pb3, par)


def _expand_heads(x_cat, bd):
    return jnp.where(bd, jnp.concatenate([x_cat] * N_HEADS, axis=0), 0.0)


def _mm_cat(a, x_cat, bd, passes):
    if passes == 1:
        return _dot(a, _expand_heads(x_cat.astype(BF16), bd))
    ah, al = _split(a)
    xh, xl = _split(x_cat)
    m = a.shape[0]
    top = _dot(jnp.concatenate([ah, al], axis=0), _expand_heads(xh, bd))
    return top[0:m] + (_dot(ah, _expand_heads(xl, bd)) + top[m:])


def _rwkv_chunk_maps(chunks, bd, strict_cat, incl_cat, eye_cat):
    c = CHUNK
    p_pair, p_inv, p_app = RWKV_PASSES
    pre = []
    for r, logw, k, v, alpha, beta in chunks:
        cum = _cumsum_rows(logw)
        e_last = jnp.exp(cum[c - 1:c, :])
        a_t = alpha * jnp.exp(cum - logw)
        r_t = r * jnp.exp(cum)
        e_inv = jnp.exp(-cum)
        b_t = beta * e_inv
        k_t = k * e_inv
        ar = jnp.concatenate([a_t, r_t], axis=0)
        bbd, kbd, vbd = _stack_heads(b_t), _stack_heads(k_t), _stack_heads(v)
        pair = _mm(ar, jnp.concatenate([bbd, kbd], axis=0), "nt", p_pair)
        l_ab = jnp.where(strict_cat, pair[0:c, 0:GW], 0.0)
        lm_k = jnp.where(jnp.concatenate([strict_cat, incl_cat], axis=0), pair[:, GW:], 0.0)
        m_rb = jnp.where(incl_cat, pair[c:, 0:GW], 0.0)
        wy0 = _mm(lm_k, vbd, "nn", p_app)
        pre.append((e_last, a_t, r_t, b_t, k_t, v, l_ab, m_rb, wy0))
    xs = [p[6] for p in pre]
    invs = [eye_cat + x for x in xs]
    xs = [_mm_cat(x, x, bd, p_inv[0]) for x in xs]
    for n in range(len(p_inv)):
        last = n == len(p_inv) - 1
        p_step = p_inv[n] if last else max(p_inv[n], p_inv[n + 1])
        if last:
            invs = [inv + _mm_cat(inv, x, bd, p_step) for inv, x in zip(invs, xs)]
        else:
            both = [_mm_cat(jnp.concatenate([inv, x], axis=0), x, bd, p_step) for inv, x in zip(invs, xs)]
            invs = [inv + bo[0:c] for inv, bo in zip(invs, both)]
            xs = [bo[c:] for bo in both]
    outs = []
    for (e_last, a_t, r_t, b_t, k_t, v, l_ab, m_rb, wy0), inv in zip(pre, invs):
        sol = _mm(inv, jnp.concatenate([_stack_heads(a_t), _stack_heads(wy0[0:c])], axis=1), "nn", p_app)
        a_s, u0 = sol[:, 0:GW], sol[:, GW:]
        out = _mm(m_rb, jnp.concatenate([_stack_heads(a_s), _stack_heads(u0)], axis=1), "nn", p_app)
        ry = r_t + out[:, 0:GW]
        y0 = wy0[c:] + out[:, GW:]
        bh, kh = b_t * e_last, k_t * e_last
        g = jnp.where(bd, _mm(a_s, bh, "tn", p_app), 0.0)
        c0 = jnp.where(bd, _mm(jnp.concatenate([u0, v], axis=0), jnp.concatenate([bh, kh], axis=0), "tn", p_app),
                       0.0)
        outs.append((ry, y0, g, c0, e_last))
    return outs


def _rwkv_body(pc_ref, par_ref, w2_ref, a2_ref, g2_ref, o_ref, ht_ref, prev_ref,
               r_s, w_s, k_s, v_s, al_s, be_s, ry_s, y_s, bon_s, gate_s, g_s, c_s, el_s, *, rb, nseq):
    @pl.when(pl.program_id(1) == 0)
    def _():
        ht_ref[...] = jnp.zeros_like(ht_ref)
        prev_ref[...] = jnp.zeros_like(prev_ref)

    c = CHUNK
    mu_main = [par_ref[n:n + 1, :] for n in range(3)]
    mu_lo = par_ref[3:4, 0:RWKV_LORA]
    w0, a0 = par_ref[4:5, :], par_ref[5:6, :]
    k_k, k_a, r_k = par_ref[6:7, :], par_ref[7:8, :], par_ref[8:9, :]
    ln_w, ln_b = par_ref[9:10, :], par_ref[10:11, :]
    bd = _head_block_mask(GW)
    ones_bd = jnp.where(bd, 1.0, 0.0).astype(BF16)

    row0 = lax.broadcasted_iota(jnp.int32, (rb, 1), 0) == 0
    for q in range(nseq):
        blk = pl.ds(q * rb, rb)
        pc = pc_ref[q]
        prev = jnp.where(row0, prev_ref[q:q + 1, :], pltpu.roll(pc, 1, 0))
        prev_ref[q:q + 1, :] = pc[rb - 1:rb, :]
        xs = [pc[:, n * GW:(n + 1) * GW] for n in range(3)]
        ps = [prev[:, n * GW:(n + 1) * GW] for n in range(3)]
        r, k, v = [x + mu * (p - x) for x, p, mu in zip(xs, ps, mu_main)]
        lo, plo = pc[:, 3 * GW:], prev[:, 3 * GW:]
        lo = lo + mu_lo * (plo - lo)
        w_in = -(w0 + _dot(jnp.tanh(lo), w2_ref[...]))
        w_pre = -(jnp.maximum(w_in, 0.0) + _log1p_exp_neg_abs(w_in)) - 0.5
        a = _sigmoid(a0 + _dot(lo, a2_ref[...]))
        g = _dot(_sigmoid(lo), g2_ref[...])
        kk = k * k_k
        kk = kk / jnp.maximum(jnp.sqrt(_dot(kk * kk, ones_bd)), 1e-12)
        k = k * (1.0 + (a - 1.0) * k_a)
        r_s[blk, :] = r
        w_s[blk, :] = -jnp.exp(w_pre)
        k_s[blk, :] = k
        v_s[blk, :] = v
        al_s[blk, :] = -kk
        be_s[blk, :] = kk * a
        bon_s[blk, :] = _dot(r * k * r_k, ones_bd) * v
        gate_s[blk, :] = g

    tt = lax.broadcasted_iota(jnp.int32, (c, GW), 0)
    ss = lax.broadcasted_iota(jnp.int32, (c, GW), 1) % c
    strict_cat, incl_cat = tt > ss, tt >= ss
    eye_cat = jnp.where(tt == ss, 1.0, 0.0)

    nb = RWKV_MAP_BATCH

    def chunk_maps(jj, carry):
        js = [jj * nb + n for n in range(nb)]
        rows = [pl.ds(pl.multiple_of(j * c, c), c) for j in js]
        maps = _rwkv_chunk_maps([(r_s[rw, :], w_s[rw, :], k_s[rw, :], v_s[rw, :], al_s[rw, :], be_s[rw, :])
                                 for rw in rows], bd, strict_cat, incl_cat, eye_cat)
        for j, rw, (ry, y0, g, c0, e_last) in zip(js, rows, maps):
            ry_s[rw, :] = ry
            y_s[rw, :] = y0
            g_s[j] = g.astype(BF16)
            c_s[j] = c0
            el_s[j] = jnp.broadcast_to(e_last, (8, GW))
        return carry

    lax.fori_loop(0, nseq * rb // (c * nb), chunk_maps, 0)

    nchunk = rb // c

    def chunk_apply(j, carry):
        for q in range(nseq):
            jq = q * nchunk + j
            rows = pl.ds(pl.multiple_of(jq * c, c), c)
            ht = ht_ref[q]
            htb = ht.astype(BF16)
            y_s[rows, :] = y_s[rows, :] + _dot_nt(ry_s[rows, :], htb)
            ht_ref[q] = ht * el_s[jq][0:1, :] + jnp.dot(htb, g_s[jq], preferred_element_type=F32) + c_s[jq]
        return carry

    lax.fori_loop(0, nchunk, chunk_apply, 0, unroll=2)

    y = y_s[...]
    mean = _dot(y, ones_bd) * (1.0 / HEAD_DIM)
    yc = y - mean
    var = _dot(yc * yc, ones_bd) * (1.0 / HEAD_DIM)
    yn = yc * lax.rsqrt(var + RWKV_GN_EPS) * ln_w + ln_b
    o_ref[...] = ((yn + bon_s[...]) * gate_s[...]).astype(BF16).reshape(nseq, rb, GW)


def _rwkv(pc3, par, w2p, a2p, g2p, rb):
    b, s, _ = pc3.shape
    nseq = 2 if b % 2 == 0 else 1
    nchunk = nseq * rb // CHUNK
    full = lambda shp: pl.BlockSpec(shp, lambda bi, i: (0,) * len(shp))
    return pl.pallas_call(
        functools.partial(_rwkv_body, rb=rb, nseq=nseq),
        grid=(b // nseq, s // rb),
        in_specs=[pl.BlockSpec((nseq, rb, RWKV_COLS), lambda bi, i: (bi, i, 0)),
                  full((16, GW)), full((RWKV_LORA, GW)), full((RWKV_LORA, GW)), full((RWKV_LORA, GW))],
        out_specs=pl.BlockSpec((nseq, rb, GW), lambda bi, i: (bi, i, 0)),
        out_shape=jax.ShapeDtypeStruct((b, s, GW), BF16),
        scratch_shapes=[pltpu.VMEM((nseq, GW, GW), F32), pltpu.VMEM((nseq, RWKV_COLS), F32)]
        + [pltpu.VMEM((nseq * rb, GW), F32)] * 10
        + [pltpu.VMEM((nchunk, GW, GW), BF16), pltpu.VMEM((nchunk, GW, GW), F32),
           pltpu.VMEM((nchunk, 8, GW), F32)],
        compiler_params=_cparams("parallel", "arbitrary"),
        name="rwkv",
    )(pc3, par, w2p, a2p, g2p)


def _lru_body(x_ref, g_ref, par_ref, wa_ref, wx_ref, o_ref, ext_ref, h_ref, *, rb):
    @pl.when(pl.program_id(1) == 0)
    def _():
        ext_ref[0:8, :] = jnp.zeros((8, GW), F32)
        h_ref[...] = jnp.zeros_like(h_ref)

    conv_b, ba, bx, lam = par_ref[4:5, :], par_ref[5:6, :], par_ref[6:7, :], par_ref[7:8, :]
    x = x_ref[0]
    ext_ref[8:8 + rb, :] = x
    conv = conv_b + par_ref[0:1, :] * ext_ref[pl.ds(8 - 3, rb), :]
    for j in range(1, LRU_CONV):
        conv = conv + par_ref[j:j + 1, :] * ext_ref[pl.ds(8 - 3 + j, rb), :]
    ext_ref[0:8, :] = x[rb - 8:rb, :]
    gate_r = _sigmoid(_dot(conv, wa_ref[...]) + ba)
    gate_i = _sigmoid(_dot(conv, wx_ref[...]) + bx)
    log_a = -LRU_C * gate_r * _softplus(-lam)
    a = jnp.exp(log_a)
    inp = jnp.sqrt((1.0 - a) * (1.0 + a)) * (gate_i * conv)
    row = lax.broadcasted_iota(jnp.int32, (rb, 1), 0) % CHUNK
    d = 1
    while d < CHUNK:
        m = row >= d
        inp = jnp.where(m, a * pltpu.roll(inp, d, 0) + inp, inp)
        a = jnp.where(m, a * pltpu.roll(a, d, 0), a)
        d *= 2
    h_prev, hs = h_ref[...], []
    for n in range(rb // CHUNK):
        seg = slice(n * CHUNK, (n + 1) * CHUNK)
        hs.append(inp[seg] + a[seg] * h_prev)
        h_prev = hs[-1][CHUNK - 1:CHUNK, :]
    h = jnp.concatenate(hs, axis=0)
    h_ref[...] = h_prev
    g = g_ref[0]
    gelu = 0.5 * g * (1.0 + jnp.tanh(math.sqrt(2.0 / math.pi) * (g + 0.044715 * (g * g * g))))
    o_ref[0] = (h * gelu).astype(BF16)


def _lru(pd3, par, wa_bd, wx_bd, rb):
    b, s, _ = pd3.shape
    full = lambda shp: pl.BlockSpec(shp, lambda bi, i: (0,) * len(shp))
    return pl.pallas_call(
        functools.partial(_lru_body, rb=rb),
        grid=(b, s // rb),
        in_specs=[pl.BlockSpec((1, rb, GW), lambda bi, i: (bi, i, 0)),
                  pl.BlockSpec((1, rb, GW), lambda bi, i: (bi, i, 1)),
                  full((8, GW)), full((GW, GW)), full((GW, GW))],
        out_specs=pl.BlockSpec((1, rb, GW), lambda bi, i: (bi, i, 0)),
        out_shape=jax.ShapeDtypeStruct((b, s, GW), BF16),
        scratch_shapes=[pltpu.VMEM((rb + 8, GW), F32), pltpu.VMEM((1, GW), F32)],
        compiler_params=_cparams("parallel", "arbitrary"),
        name="lru",
    )(pd3, pd3, par, wa_bd, wx_bd)


def _outmlp_body(ya_ref, yb_ref, yc_ref, yd_ref, x_ref, wo_ref, gains_ref, w1_ref, w2_ref, o_ref,
                 xm_ref, h_ref, acc_ref):
    kk = pl.program_id(1)

    @pl.when(kk == 0)
    def _():
        mix = jnp.concatenate([ya_ref[...], yb_ref[...], yc_ref[...], yd_ref[...]], axis=1)
        xm = x_ref[...] + _rms(jnp.dot(mix, wo_ref[...], preferred_element_type=F32), gains_ref[0:1, :])
        xm_ref[...] = xm
        h_ref[...] = _rms(xm, gains_ref[1:2, :]).astype(BF16)
        acc_ref[...] = jnp.zeros_like(acc_ref)

    a = jnp.dot(h_ref[...], w1_ref[...], preferred_element_type=F32)
    a = jnp.square(jnp.maximum(a, 0.0)).astype(BF16)
    acc_ref[...] += jnp.dot(a, w2_ref[...], preferred_element_type=F32)

    @pl.when(kk == pl.num_programs(1) - 1)
    def _():
        o_ref[...] = xm_ref[...] + _rms(acc_ref[...], gains_ref[2:3, :])


def _outmlp(ya, yb, yc, yd, x2, wo_all, gains, w1_all, w2_all, layer, tm, tk):
    t, d = x2.shape
    hid = w1_all.shape[2]
    mix = pl.BlockSpec((tm, GW), lambda i, k: (i, 0))
    return pl.pallas_call(
        _outmlp_body,
        grid=(t // tm, hid // tk),
        in_specs=[mix, mix, mix, mix,
                  pl.BlockSpec((tm, d), lambda i, k: (i, 0)),
                  pl.BlockSpec((None, 4 * GW, d), lambda i, k: (layer, 0, 0)),
                  pl.BlockSpec((8, d), lambda i, k: (0, 0)),
                  pl.BlockSpec((None, d, tk), lambda i, k: (layer, 0, k)),
                  pl.BlockSpec((None, tk, d), lambda i, k: (layer, k, 0))],
        out_specs=pl.BlockSpec((tm, d), lambda i, k: (i, 0)),
        out_shape=jax.ShapeDtypeStruct((t, d), F32),
        scratch_shapes=[pltpu.VMEM((tm, d), F32), pltpu.VMEM((tm, d), BF16), pltpu.VMEM((tm, d), F32)],
        compiler_params=_cparams("parallel", "arbitrary"),
        name="outmlp",
    )(ya, yb, yc, yd, x2, wo_all, gains, w1_all, w2_all)


def _rows(vectors, n_rows):
    tab = jnp.stack([v.astype(F32) for v in vectors])
    return jnp.pad(tab, ((0, n_rows - tab.shape[0]), (0, 0)))


def _block_diag(w):
    out = jnp.zeros((GW, GW), w.dtype)
    for n in range(w.shape[0]):
        out = out.at[n * HEAD_DIM:(n + 1) * HEAD_DIM, n * HEAD_DIM:(n + 1) * HEAD_DIM].set(w[n])
    return out


def _pick_block(n, want):
    while n % want:
        want //= 2
    return want


def kernel(x, norm_mix_pre, norm_mix_post, norm_mlp_pre, norm_mlp_post, w_in, w_out, attn_rel_bias, hgrn_lb_logits, hgrn_norm, rwkv_mu, rwkv_w0, rwkv_w2, rwkv_a0, rwkv_a2, rwkv_g2, rwkv_k_k, rwkv_k_a, rwkv_r_k, rwkv_ln_w, rwkv_ln_b, lru_conv_w, lru_conv_b, lru_wa, lru_ba, lru_wx, lru_bx, lru_lambda, mlp_w1, mlp_w2):
    b, s, d = x.shape
    depth = w_in.shape[0]
    t = b * s
    tm = _pick_block(t, 512)
    tm_mlp = _pick_block(t, 1024)
    rb = _pick_block(s, 512)
    rb_big = _pick_block(s, 1024)
    tk = _pick_block(mlp_w1.shape[-1], 1024)

    lb_sm = jax.nn.softmax(hgrn_lb_logits.astype(F32), axis=0)
    lb_all = jnp.maximum(jnp.cumsum(lb_sm, axis=0) - lb_sm[0:1], 0.0)

    w_in_b, w_out_b = w_in.astype(BF16), w_out.astype(BF16)
    w1_b, w2_b = mlp_w1.astype(BF16), mlp_w2.astype(BF16)

    x2 = x.reshape(t, d)
    for l in range(depth):
        pa, pb, pc, pd = _inproj(x2, norm_mix_pre[l].reshape(1, d), w_in_b, l, tm)

        ya = _attention(pa.reshape(b, s, ATTN_COLS), _attn_bias_table(attn_rel_bias[l]), rb_big)

        lb = lb_all[l]
        hgrn_par = _rows([lb, jnp.log1p(-lb), 1.0 - lb, hgrn_norm[l]], 8)
        yb = _hgrn(pb.reshape(b, s, HGRN_COLS), hgrn_par, rb_big)

        mu = rwkv_mu[l].astype(F32)
        mu_lo = jnp.pad(mu[3 * GW:], (0, GW - RWKV_LORA))
        rwkv_par = _rows([mu[0:GW], mu[GW:2 * GW], mu[2 * GW:3 * GW], mu_lo, rwkv_w0[l], rwkv_a0[l],
                          rwkv_k_k[l], rwkv_k_a[l], rwkv_r_k[l].reshape(GW), rwkv_ln_w[l], rwkv_ln_b[l]], 16)
        zeros = lambda n: jnp.zeros((n, GW), F32)
        w2p = jnp.concatenate([rwkv_w2[l].astype(F32), zeros(96)], axis=0).astype(BF16)
        a2p = jnp.concatenate([zeros(32), rwkv_a2[l].astype(F32), zeros(64)], axis=0).astype(BF16)
        g2p = jnp.concatenate([zeros(64), rwkv_g2[l].astype(F32)], axis=0).astype(BF16)
        yc = _rwkv(pc.reshape(b, s, RWKV_COLS), rwkv_par, w2p, a2p, g2p, rb)

        cw = lru_conv_w[l].astype(F32)
        lru_par = _rows([cw[0], cw[1], cw[2], cw[3], lru_conv_b[l], lru_ba[l], lru_bx[l], lru_lambda[l]], 8)
        yd = _lru(pd.reshape(b, s, LRU_COLS), lru_par, _block_diag(lru_wa[l]).astype(BF16),
                  _block_diag(lru_wx[l]).astype(BF16), rb)

        gains = jnp.pad(jnp.stack([norm_mix_post[l], norm_mlp_pre[l], norm_mlp_post[l]]).astype(F32),
                        ((0, 5), (0, 0)))
        x2 = _outmlp(ya.reshape(t, GW), yb.reshape(t, GW), yc.reshape(t, GW), yd.reshape(t, GW),
                     x2, w_out_b, gains, w1_b, w2_b, l, tm_mlp, tk)
    return x2.reshape(b, s, d)
```

```python
import functools
import math

import numpy as np

import jax
import jax.numpy as jnp
from jax import lax
from jax.experimental import pallas as pl
from jax.experimental.pallas import tpu as pltpu

F32 = jnp.float32
BF16 = jnp.bfloat16

GW = 256
HEAD_DIM = 64
N_HEADS = 4
CHUNK = 64
ATTN_LEFT_CHUNKS = 8
BAND = (ATTN_LEFT_CHUNKS + 1) * CHUNK
ATTN_PAD = ATTN_LEFT_CHUNKS * CHUNK
REL_CLIP = 256
ATTN_SCALE = HEAD_DIM ** -0.5
NEG_INF = -1e30
RMS_EPS = 1e-6
RWKV_GN_EPS = HEAD_DIM * 1e-5
RWKV_LORA = 128
LRU_C = 8.0
LRU_CONV = 4
HGRN_SUB = 16

ATTN_COLS = 3 * GW
HGRN_COLS = 4 * GW
RWKV_COLS = 3 * GW + RWKV_LORA
LRU_COLS = 2 * GW

VMEM_LIMIT = 56 * 1024 * 1024
RWKV_PASSES = (1, (3, 3, 3, 1, 1), 1)
RWKV_MAP_BATCH = 8
ATTN_BATCH = 8
CHUNK_UNROLL = 8


def _cparams(*sem):
    return pltpu.CompilerParams(dimension_semantics=sem, vmem_limit_bytes=VMEM_LIMIT)


def _dot(a, b):
    return jnp.dot(a.astype(BF16), b.astype(BF16), preferred_element_type=F32)


def _dot_nt(a, b):
    return lax.dot_general(a.astype(BF16), b.astype(BF16), (((1,), (1,)), ((), ())),
                           preferred_element_type=F32)


def _dot_tn(a, b):
    return lax.dot_general(a.astype(BF16), b.astype(BF16), (((0,), (0,)), ((), ())),
                           preferred_element_type=F32)


def _split(a):
    hi = a.astype(BF16)
    lo = (a - hi.astype(F32)).astype(BF16)
    return hi, lo


def _mm(a, b, kind="nn", passes=1):
    f = {"nn": _dot, "nt": _dot_nt, "tn": _dot_tn}[kind]
    if passes == 1:
        return f(a, b)
    ah, al = _split(a)
    bh, bl = _split(b)
    if kind == "tn":
        return f(ah, bh) + (f(ah, bl) + f(al, bh))
    m = a.shape[0]
    top = f(jnp.concatenate([ah, al], axis=0), bh)
    return top[0:m] + (f(ah, bl) + top[m:])


def _rms(x, gain):
    return x * lax.rsqrt(jnp.mean(x * x, axis=-1, keepdims=True) + RMS_EPS) * gain


def _sigmoid(x):
    return 1.0 / (1.0 + jnp.exp(-x))


def _softplus(x):
    return jnp.maximum(x, 0.0) + jnp.log1p(jnp.exp(-jnp.abs(x)))


def _log1p_exp_neg_abs(x):
    return jnp.log(1.0 + jnp.exp(-jnp.abs(x)))


def _head_block_mask(n):
    r = lax.broadcasted_iota(jnp.int32, (n, n), 0) // HEAD_DIM
    c = lax.broadcasted_iota(jnp.int32, (n, n), 1) // HEAD_DIM
    return r == c


def _cumsum_rows(x):
    n = x.shape[0]
    row = lax.broadcasted_iota(jnp.int32, (n, 1), 0)
    d = 1
    while d < n:
        x = x + jnp.where(row >= d, pltpu.roll(x, d, 0), 0.0)
        d *= 2
    return x


def _stack_heads(x):
    lane_head = lax.broadcasted_iota(jnp.int32, (1, GW), 1) // HEAD_DIM
    return jnp.concatenate([jnp.where(lane_head == h, x, 0.0) for h in range(N_HEADS)], axis=0)


def _inproj_body(x_ref, g_ref, w_ref, oa_ref, ob_ref, oc_ref, od_ref):
    h = _rms(x_ref[...], g_ref[...]).astype(BF16)
    o0, o1, o2 = ATTN_COLS, ATTN_COLS + HGRN_COLS, ATTN_COLS + HGRN_COLS + RWKV_COLS
    oa_ref[...] = jnp.dot(h, w_ref[:, 0:o0], preferred_element_type=F32).astype(BF16)
    ob_ref[...] = jnp.dot(h, w_ref[:, o0:o1], preferred_element_type=F32)
    oc_ref[...] = jnp.dot(h, w_ref[:, o1:o2], preferred_element_type=F32)
    od_ref[...] = jnp.dot(h, w_ref[:, o2:], preferred_element_type=F32)


def _inproj(x2, gain, w_all, layer, tm):
    t, d = x2.shape
    d_in = w_all.shape[2]
    return pl.pallas_call(
        _inproj_body,
        grid=(t // tm,),
        in_specs=[pl.BlockSpec((tm, d), lambda i: (i, 0)),
                  pl.BlockSpec((1, d), lambda i: (0, 0)),
                  pl.BlockSpec((None, d, d_in), lambda i: (layer, 0, 0))],
        out_specs=[pl.BlockSpec((tm, ATTN_COLS), lambda i: (i, 0)),
                   pl.BlockSpec((tm, HGRN_COLS), lambda i: (i, 0)),
                   pl.BlockSpec((tm, RWKV_COLS), lambda i: (i, 0)),
                   pl.BlockSpec((tm, LRU_COLS), lambda i: (i, 0))],
        out_shape=[jax.ShapeDtypeStruct((t, ATTN_COLS), BF16),
                   jax.ShapeDtypeStruct((t, HGRN_COLS), F32),
                   jax.ShapeDtypeStruct((t, RWKV_COLS), F32),
                   jax.ShapeDtypeStruct((t, LRU_COLS), F32)],
        compiler_params=_cparams("parallel"),
        name="inproj",
    )(x2, gain, w_all)


def _attn_body(q_ref, k_ref, v_ref, bias_ref, o_ref, kp_ref, vp_ref, *, chunks):
    i = pl.program_id(1)

    @pl.when(i == 0)
    def _():
        for src, dst in ((k_ref, kp_ref), (v_ref, vp_ref)):
            dst[0:ATTN_PAD, :] = jnp.zeros((ATTN_PAD, GW), BF16)
            dst[ATTN_PAD:, :] = src[0]

    lane_head = lax.broadcasted_iota(jnp.int32, (1, GW), 1) // HEAD_DIM
    kpos = lax.broadcasted_iota(jnp.int32, (1, BAND), 1)

    nb = ATTN_BATCH

    def chunk_group(jj, carry):
        cs = [i * chunks + jj * nb + n for n in range(nb)]
        rows = [pl.ds(pl.multiple_of((jj * nb + n) * CHUNK, CHUNK), CHUNK) for n in range(nb)]
        wins = [pl.ds(pl.multiple_of(c * CHUNK, CHUNK), BAND) for c in cs]
        ss = []
        for c, rw, win in zip(cs, rows, wins):
            q = q_ref[0, rw, :] * ATTN_SCALE
            qbd = jnp.concatenate([jnp.where(lane_head == h, q, jnp.zeros_like(q)) for h in range(N_HEADS)],
                                  axis=0)
            s = _dot_nt(qbd, kp_ref[win, :]) + bias_ref[...]
            ss.append(jnp.where(kpos >= (ATTN_LEFT_CHUNKS - c) * CHUNK, s, NEG_INF))
        ps = [jnp.exp(s - jnp.max(s, axis=-1, keepdims=True)) for s in ss]
        for rw, win, p in zip(rows, wins, ps):
            inv_l = 1.0 / jnp.sum(p, axis=-1, keepdims=True)
            obd = _dot(p, vp_ref[win, :]) * inv_l
            o = jnp.zeros((CHUNK, GW), F32)
            for h in range(N_HEADS):
                o = o + jnp.where(lane_head == h, obd[h * CHUNK:(h + 1) * CHUNK, :], 0.0)
            o_ref[0, rw, :] = o.astype(BF16)
        return carry

    lax.fori_loop(0, chunks // nb, chunk_group, 0, unroll=2)


def _attention(pa3, bias, rb):
    b, s, _ = pa3.shape
    return pl.pallas_call(
        functools.partial(_attn_body, chunks=rb // CHUNK),
        grid=(b, s // rb),
        in_specs=[pl.BlockSpec((1, rb, GW), lambda bi, i: (bi, i, 0)),
                  pl.BlockSpec((1, s, GW), lambda bi, i: (bi, 0, 1)),
                  pl.BlockSpec((1, s, GW), lambda bi, i: (bi, 0, 2)),
                  pl.BlockSpec((N_HEADS * CHUNK, BAND), lambda bi, i: (0, 0))],
        out_specs=pl.BlockSpec((1, rb, GW), lambda bi, i: (bi, i, 0)),
        out_shape=jax.ShapeDtypeStruct((b, s, GW), BF16),
        scratch_shapes=[pltpu.VMEM((s + ATTN_PAD, GW), BF16)] * 2,
        compiler_params=_cparams("parallel", "arbitrary"),
        name="attn",
    )(pa3, pa3, pa3, bias)


def _attn_bias_table(rel_bias):
    rel = np.arange(BAND + CHUNK - 1) - (CHUNK - 1) - ATTN_LEFT_CHUNKS * CHUNK
    ext = rel_bias.astype(F32)[:, np.clip(rel, -REL_CLIP, REL_CLIP) + REL_CLIP]
    tab = jnp.stack([ext[:, CHUNK - 1 - q:CHUNK - 1 - q + BAND] for q in range(CHUNK)], axis=1)
    return tab.reshape(N_HEADS * CHUNK, BAND)


def _hgrn_body(q_ref, f_ref, i_ref, g_ref, par_ref, o_ref, st_ref, oacc_ref, q_s, key_s, cum16_s, cum64_s,
               ckey_s, *, rb):
    @pl.when(pl.program_id(1) == 0)
    def _():
        st_ref[...] = jnp.zeros_like(st_ref)

    m = HGRN_SUB
    lb = par_ref[0:1, :]
    log_1m_lb = par_ref[1:2, :]
    one_m_lb = par_ref[2:3, :]
    gain = par_ref[3:4, :]
    bd = _head_block_mask(GW)
    ones_bd = jnp.where(bd, 1.0, 0.0).astype(BF16)
    trow = lax.broadcasted_iota(jnp.int32, (m, 1), 0)

    fr = f_ref[0]
    e = jnp.exp(-jnp.abs(fr))
    inv_1pe = 1.0 / (1.0 + e)
    sig = jnp.where(fr >= 0.0, inv_1pe, e * inv_1pe)
    log_1pe = jnp.log(1.0 + e)
    log_sig = jnp.minimum(fr, 0.0) - log_1pe
    log_key = log_1m_lb - jnp.maximum(fr, 0.0) - log_1pe
    log_f = jnp.maximum(jnp.log(lb + one_m_lb * sig), log_1m_lb + log_sig)
    row = lax.broadcasted_iota(jnp.int32, (rb, 1), 0) % CHUNK
    cum, d = log_f, 1
    while d < CHUNK:
        cum = cum + jnp.where(row >= d, pltpu.roll(cum, d, 0), 0.0)
        d *= 2
    cum64_s[...] = cum
    before = jnp.where(row == 0, 0.0, pltpu.roll(cum, 1, 0)).reshape(rb // m, m, GW)[:, 0:1, :]
    cum16 = cum - jnp.broadcast_to(before, (rb // m, m, GW)).reshape(rb, GW)
    cum16_s[...] = cum16
    ckey_s[...] = cum16 - log_key
    key_s[...] = one_m_lb * jnp.where(fr >= 0.0, e * inv_1pe, inv_1pe)
    qr = q_ref[0]
    q_s[...] = qr * _sigmoid(qr)

    nsub = CHUNK // m

    def chunk(j, carry):
        rows = pl.ds(pl.multiple_of(j * CHUNK, CHUNK), CHUNK)
        q, key, v = q_s[rows, :], key_s[rows, :], i_ref[0, rows, :]
        c16, c64, ck = cum16_s[rows, :], cum64_s[rows, :], ckey_s[rows, :]
        sub = lambda x, i: x[i * m:(i + 1) * m, :]
        st = st_ref[...]
        o = _dot_nt(q * jnp.exp(c64), st)
        a_exp = []
        for i in range(nsub):
            qi, ci, cki = sub(q, i), sub(c16, i), sub(ck, i)
            ps = []
            for s in range(m):
                ps.append(qi * jnp.exp(jnp.where(trow >= s, ci - cki[s:s + 1, :], NEG_INF)))
            a_exp.append(_dot(jnp.concatenate(ps, axis=0), ones_bd))
        o_sub = []
        for i in range(nsub):
            vi, oi = sub(v, i), sub(o, i)
            for s in range(m):
                oi = oi + a_exp[i][s * m:(s + 1) * m, :] * vi[s:s + 1, :]
            o_sub.append(oi)
        q_hat = q * jnp.exp(c16)
        k_hat = [sub(key, i) * jnp.exp(sub(c16, i)[m - 1:m, :] - sub(c16, i)) for i in range(nsub)]
        b_start = [sub(c64, i)[0:1, :] - sub(c16, i)[0:1, :] for i in range(nsub)]
        b_end = [sub(c64, i)[m - 1:m, :] for i in range(nsub)]
        v_bd = [_stack_heads(sub(v, i)) for i in range(nsub)]
        for i in range(1, nsub):
            k_bd = jnp.concatenate([_stack_heads(k_hat[jj] * jnp.exp(b_start[i] - b_end[jj])) for jj in range(i)],
                                   axis=0)
            a_cat = _dot_nt(sub(q_hat, i), k_bd)
            o_sub[i] = o_sub[i] + _dot(a_cat, jnp.concatenate(v_bd[0:i], axis=0))
        oacc_ref[rows, :] = jnp.concatenate(o_sub, axis=0)
        last = c64[CHUNK - 1:CHUNK, :]
        st_ref[...] = st * jnp.exp(last) + jnp.where(bd, _dot_tn(v, key * jnp.exp(last - c64)), 0.0)
        return carry

    lax.fori_loop(0, rb // CHUNK, chunk, 0, unroll=CHUNK_UNROLL)
    o = oacc_ref[...]
    ms = _dot(o * o, ones_bd) * (1.0 / HEAD_DIM)
    g = g_ref[0]
    o_ref[0] = (o * lax.rsqrt(ms + RMS_EPS) * gain * (g * _sigmoid(g))).astype(BF16)


def _hgrn(pb3, par, rb):
    b, s, _ = pb3.shape
    col = lambda n: pl.BlockSpec((1, rb, GW), lambda bi, i, n=n: (bi, i, n))
    return pl.pallas_call(
        functools.partial(_hgrn_body, rb=rb),
        grid=(b, s // rb),
        in_specs=[col(0), col(1), col(2), col(3),
                  pl.BlockSpec((8, GW), lambda bi, i: (0, 0))],
        out_specs=pl.BlockSpec((1, rb, GW), lambda bi, i: (bi, i, 0)),
        out_shape=jax.ShapeDtypeStruct((b, s, GW), BF16),
        scratch_shapes=[pltpu.VMEM((GW, GW), F32)] + [pltpu.VMEM((rb, GW), F32)] * 6,
        compiler_params=_cparams("parallel", "arbitrary"),
        name="hgrn",
    )(pb3, pb3, pb3, pb3, par)


def _expand_heads(x_cat, bd):
    return jnp.where(bd, jnp.concatenate([x_cat] * N_HEADS, axis=0), 0.0)


def _mm_cat(a, x_cat, bd, passes):
    if passes == 1:
        return _dot(a, _expand_heads(x_cat.astype(BF16), bd))
    ah, al = _split(a)
    xh, xl = _split(x_cat)
    m = a.shape[0]
    top = _dot(jnp.concatenate([ah, al], axis=0), _expand_heads(xh, bd))
    return top[0:m] + (_dot(ah, _expand_heads(xl, bd)) + top[m:])


def _rwkv_chunk_maps(chunks, bd, strict_cat, incl_cat, eye_cat):
    c = CHUNK
    p_pair, p_inv, p_app = RWKV_PASSES
    pre = []
    for r, logw, k, v, alpha, beta in chunks:
        cum = _cumsum_rows(logw)
        e_last = jnp.exp(cum[c - 1:c, :])
        a_t = alpha * jnp.exp(cum - logw)
        r_t = r * jnp.exp(cum)
        e_inv = jnp.exp(-cum)
        b_t = beta * e_inv
        k_t = k * e_inv
        ar = jnp.concatenate([a_t, r_t], axis=0)
        bbd, kbd, vbd = _stack_heads(b_t), _stack_heads(k_t), _stack_heads(v)
        pair = _mm(ar, jnp.concatenate([bbd, kbd], axis=0), "nt", p_pair)
        l_ab = jnp.where(strict_cat, pair[0:c, 0:GW], 0.0)
        lm_k = jnp.where(jnp.concatenate([strict_cat, incl_cat], axis=0), pair[:, GW:], 0.0)
        m_rb = jnp.where(incl_cat, pair[c:, 0:GW], 0.0)
        wy0 = _mm(lm_k, vbd, "nn", p_app)
        pre.append((e_last, a_t, r_t, b_t, k_t, v, l_ab, m_rb, wy0))
    xs = [p[6] for p in pre]
    invs = [eye_cat + x for x in xs]
    xs = [_mm_cat(x, x, bd, p_inv[0]) for x in xs]
    for n in range(len(p_inv)):
        last = n == len(p_inv) - 1
        p_step = p_inv[n] if last else max(p_inv[n], p_inv[n + 1])
        if last:
            invs = [inv + _mm_cat(inv, x, bd, p_step) for inv, x in zip(invs, xs)]
        else:
            both = [_mm_cat(jnp.concatenate([inv, x], axis=0), x, bd, p_step) for inv, x in zip(invs, xs)]
            invs = [inv + bo[0:c] for inv, bo in zip(invs, both)]
            xs = [bo[c:] for bo in both]
    outs = []
    for (e_last, a_t, r_t, b_t, k_t, v, l_ab, m_rb, wy0), inv in zip(pre, invs):
        sol = _mm(inv, jnp.concatenate([_stack_heads(a_t), _stack_heads(wy0[0:c])], axis=1), "nn", p_app)
        a_s, u0 = sol[:, 0:GW], sol[:, GW:]
        out = _mm(m_rb, jnp.concatenate([_stack_heads(a_s), _stack_heads(u0)], axis=1), "nn", p_app)
        ry = r_t + out[:, 0:GW]
        y0 = wy0[c:] + out[:, GW:]
        bh, kh = b_t * e_last, k_t * e_last
        g = jnp.where(bd, _mm(a_s, bh, "tn", p_app), 0.0)
        c0 = jnp.where(bd, _mm(jnp.concatenate([u0, v], axis=0), jnp.concatenate([bh, kh], axis=0), "tn", p_app),
                       0.0)
        outs.append((ry, y0, g, c0, e_last))
    return outs


def _rwkv_body(pc_ref, par_ref, w2_ref, a2_ref, g2_ref, o_ref, ht_ref, prev_ref,
               r_s, w_s, k_s, v_s, al_s, be_s, ry_s, y_s, bon_s, gate_s, g_s, c_s, el_s, *, rb, nseq):
    @pl.when(pl.program_id(1) == 0)
    def _():
        ht_ref[...] = jnp.zeros_like(ht_ref)
        prev_ref[...] = jnp.zeros_like(prev_ref)

    c = CHUNK
    mu_main = [par_ref[n:n + 1, :] for n in range(3)]
    mu_lo = par_ref[3:4, 0:RWKV_LORA]
    w0, a0 = par_ref[4:5, :], par_ref[5:6, :]
    k_k, k_a, r_k = par_ref[6:7, :], par_ref[7:8, :], par_ref[8:9, :]
    ln_w, ln_b = par_ref[9:10, :], par_ref[10:11, :]
    bd = _head_block_mask(GW)
    ones_bd = jnp.where(bd, 1.0, 0.0).astype(BF16)

    row0 = lax.broadcasted_iota(jnp.int32, (rb, 1), 0) == 0
    for q in range(nseq):
        blk = pl.ds(q * rb, rb)
        pc = pc_ref[q]
        prev = jnp.where(row0, prev_ref[q:q + 1, :], pltpu.roll(pc, 1, 0))
        prev_ref[q:q + 1, :] = pc[rb - 1:rb, :]
        xs = [pc[:, n * GW:(n + 1) * GW] for n in range(3)]
        ps = [prev[:, n * GW:(n + 1) * GW] for n in range(3)]
        r, k, v = [x + mu * (p - x) for x, p, mu in zip(xs, ps, mu_main)]
        lo, plo = pc[:, 3 * GW:], prev[:, 3 * GW:]
        lo = lo + mu_lo * (plo - lo)
        w_in = -(w0 + _dot(jnp.tanh(lo), w2_ref[...]))
        w_pre = -(jnp.maximum(w_in, 0.0) + _log1p_exp_neg_abs(w_in)) - 0.5
        a = _sigmoid(a0 + _dot(lo, a2_ref[...]))
        g = _dot(_sigmoid(lo), g2_ref[...])
        kk = k * k_k
        kk = kk / jnp.maximum(jnp.sqrt(_dot(kk * kk, ones_bd)), 1e-12)
        k = k * (1.0 + (a - 1.0) * k_a)
        r_s[blk, :] = r
        w_s[blk, :] = -jnp.exp(w_pre)
        k_s[blk, :] = k
        v_s[blk, :] = v
        al_s[blk, :] = -kk
        be_s[blk, :] = kk * a
        bon_s[blk, :] = _dot(r * k * r_k, ones_bd) * v
        gate_s[blk, :] = g

    tt = lax.broadcasted_iota(jnp.int32, (c, GW), 0)
    ss = lax.broadcasted_iota(jnp.int32, (c, GW), 1) % c
    strict_cat, incl_cat = tt > ss, tt >= ss
    eye_cat = jnp.where(tt == ss, 1.0, 0.0)

    nb = RWKV_MAP_BATCH

    def chunk_maps(jj, carry):
        js = [jj * nb + n for n in range(nb)]
        rows = [pl.ds(pl.multiple_of(j * c, c), c) for j in js]
        maps = _rwkv_chunk_maps([(r_s[rw, :], w_s[rw, :], k_s[rw, :], v_s[rw, :], al_s[rw, :], be_s[rw, :])
                                 for rw in rows], bd, strict_cat, incl_cat, eye_cat)
        for j, rw, (ry, y0, g, c0, e_last) in zip(js, rows, maps):
            ry_s[rw, :] = ry
            y_s[rw, :] = y0
            g_s[j] = g.astype(BF16)
            c_s[j] = c0
            el_s[j] = jnp.broadcast_to(e_last, (8, GW))
        return carry

    lax.fori_loop(0, nseq * rb // (c * nb), chunk_maps, 0)

    nchunk = rb // c

    def chunk_apply(j, carry):
        for q in range(nseq):
            jq = q * nchunk + j
            rows = pl.ds(pl.multiple_of(jq * c, c), c)
            ht = ht_ref[q]
            htb = ht.astype(BF16)
            y_s[rows, :] = y_s[rows, :] + _dot_nt(ry_s[rows, :], htb)
            ht_ref[q] = ht * el_s[jq][0:1, :] + jnp.dot(htb, g_s[jq], preferred_element_type=F32) + c_s[jq]
        return carry

    lax.fori_loop(0, nchunk, chunk_apply, 0, unroll=CHUNK_UNROLL)

    y = y_s[...]
    mean = _dot(y, ones_bd) * (1.0 / HEAD_DIM)
    yc = y - mean
    var = _dot(yc * yc, ones_bd) * (1.0 / HEAD_DIM)
    yn = yc * lax.rsqrt(var + RWKV_GN_EPS) * ln_w + ln_b
    o_ref[...] = ((yn + bon_s[...]) * gate_s[...]).astype(BF16).reshape(nseq, rb, GW)


def _rwkv(pc3, par, w2p, a2p, g2p, rb):
    b, s, _ = pc3.shape
    nseq = 2 if b % 2 == 0 else 1
    nchunk = nseq * rb // CHUNK
    full = lambda shp: pl.BlockSpec(shp, lambda bi, i: (0,) * len(shp))
    return pl.pallas_call(
        functools.partial(_rwkv_body, rb=rb, nseq=nseq),
        grid=(b // nseq, s // rb),
        in_specs=[pl.BlockSpec((nseq, rb, RWKV_COLS), lambda bi, i: (bi, i, 0)),
                  full((16, GW)), full((RWKV_LORA, GW)), full((RWKV_LORA, GW)), full((RWKV_LORA, GW))],
        out_specs=pl.BlockSpec((nseq, rb, GW), lambda bi, i: (bi, i, 0)),
        out_shape=jax.ShapeDtypeStruct((b, s, GW), BF16),
        scratch_shapes=[pltpu.VMEM((nseq, GW, GW), F32), pltpu.VMEM((nseq, RWKV_COLS), F32)]
        + [pltpu.VMEM((nseq * rb, GW), F32)] * 10
        + [pltpu.VMEM((nchunk, GW, GW), BF16), pltpu.VMEM((nchunk, GW, GW), F32),
           pltpu.VMEM((nchunk, 8, GW), F32)],
        compiler_params=_cparams("parallel", "arbitrary"),
        name="rwkv",
    )(pc3, par, w2p, a2p, g2p)


def _lru_body(x_ref, g_ref, par_ref, wa_ref, wx_ref, o_ref, ext_ref, h_ref, *, rb):
    @pl.when(pl.program_id(1) == 0)
    def _():
        ext_ref[0:8, :] = jnp.zeros((8, GW), F32)
        h_ref[...] = jnp.zeros_like(h_ref)

    conv_b, ba, bx, lam = par_ref[4:5, :], par_ref[5:6, :], par_ref[6:7, :], par_ref[7:8, :]
    x = x_ref[0]
    ext_ref[8:8 + rb, :] = x
    conv = conv_b + par_ref[0:1, :] * ext_ref[pl.ds(8 - 3, rb), :]
    for j in range(1, LRU_CONV):
        conv = conv + par_ref[j:j + 1, :] * ext_ref[pl.ds(8 - 3 + j, rb), :]
    ext_ref[0:8, :] = x[rb - 8:rb, :]
    gate_r = _sigmoid(_dot(conv, wa_ref[...]) + ba)
    gate_i = _sigmoid(_dot(conv, wx_ref[...]) + bx)
    log_a = -LRU_C * gate_r * _softplus(-lam)
    a = jnp.exp(log_a)
    inp = jnp.sqrt((1.0 - a) * (1.0 + a)) * (gate_i * conv)
    row = lax.broadcasted_iota(jnp.int32, (rb, 1), 0) % CHUNK
    d = 1
    while d < CHUNK:
        m = row >= d
        inp = jnp.where(m, a * pltpu.roll(inp, d, 0) + inp, inp)
        a = jnp.where(m, a * pltpu.roll(a, d, 0), a)
        d *= 2
    h_prev, hs = h_ref[...], []
    for n in range(rb // CHUNK):
        seg = slice(n * CHUNK, (n + 1) * CHUNK)
        hs.append(inp[seg] + a[seg] * h_prev)
        h_prev = hs[-1][CHUNK - 1:CHUNK, :]
    h = jnp.concatenate(hs, axis=0)
    h_ref[...] = h_prev
    g = g_ref[0]
    gelu = 0.5 * g * (1.0 + jnp.tanh(math.sqrt(2.0 / math.pi) * (g + 0.044715 * (g * g * g))))
    o_ref[0] = (h * gelu).astype(BF16)


def _lru(pd3, par, wa_bd, wx_bd, rb):
    b, s, _ = pd3.shape
    full = lambda shp: pl.BlockSpec(shp, lambda bi, i: (0,) * len(shp))
    return pl.pallas_call(
        functools.partial(_lru_body, rb=rb),
        grid=(b, s // rb),
        in_specs=[pl.BlockSpec((1, rb, GW), lambda bi, i: (bi, i, 0)),
                  pl.BlockSpec((1, rb, GW), lambda bi, i: (bi, i, 1)),
                  full((8, GW)), full((GW, GW)), full((GW, GW))],
        out_specs=pl.BlockSpec((1, rb, GW), lambda bi, i: (bi, i, 0)),
        out_shape=jax.ShapeDtypeStruct((b, s, GW), BF16),
        scratch_shapes=[pltpu.VMEM((rb + 8, GW), F32), pltpu.VMEM((1, GW), F32)],
        compiler_params=_cparams("parallel", "arbitrary"),
        name="lru",
    )(pd3, pd3, par, wa_bd, wx_bd)


def _outmlp_body(ya_ref, yb_ref, yc_ref, yd_ref, x_ref, wo_ref, gains_ref, w1_ref, w2_ref, o_ref,
                 xm_ref, h_ref, acc_ref):
    kk = pl.program_id(1)

    @pl.when(kk == 0)
    def _():
        mix = jnp.concatenate([ya_ref[...], yb_ref[...], yc_ref[...], yd_ref[...]], axis=1)
        xm = x_ref[...] + _rms(jnp.dot(mix, wo_ref[...], preferred_element_type=F32), gains_ref[0:1, :])
        xm_ref[...] = xm
        h_ref[...] = _rms(xm, gains_ref[1:2, :]).astype(BF16)
        acc_ref[...] = jnp.zeros_like(acc_ref)

    a = jnp.dot(h_ref[...], w1_ref[...], preferred_element_type=F32)
    a = jnp.square(jnp.maximum(a, 0.0)).astype(BF16)
    acc_ref[...] += jnp.dot(a, w2_ref[...], preferred_element_type=F32)

    @pl.when(kk == pl.num_programs(1) - 1)
    def _():
        o_ref[...] = xm_ref[...] + _rms(acc_ref[...], gains_ref[2:3, :])


def _outmlp(ya, yb, yc, yd, x2, wo_all, gains, w1_all, w2_all, layer, tm, tk):
    t, d = x2.shape
    hid = w1_all.shape[2]
    mix = pl.BlockSpec((tm, GW), lambda i, k: (i, 0))
    return pl.pallas_call(
        _outmlp_body,
        grid=(t // tm, hid // tk),
        in_specs=[mix, mix, mix, mix,
                  pl.BlockSpec((tm, d), lambda i, k: (i, 0)),
                  pl.BlockSpec((None, 4 * GW, d), lambda i, k: (layer, 0, 0)),
                  pl.BlockSpec((8, d), lambda i, k: (0, 0)),
                  pl.BlockSpec((None, d, tk), lambda i, k: (layer, 0, k)),
                  pl.BlockSpec((None, tk, d), lambda i, k: (layer, k, 0))],
        out_specs=pl.BlockSpec((tm, d), lambda i, k: (i, 0)),
        out_shape=jax.ShapeDtypeStruct((t, d), F32),
        scratch_shapes=[pltpu.VMEM((tm, d), F32), pltpu.VMEM((tm, d), BF16), pltpu.VMEM((tm, d), F32)],
        compiler_params=_cparams("parallel", "arbitrary"),
        name="outmlp",
    )(ya, yb, yc, yd, x2, wo_all, gains, w1_all, w2_all)


def _rows(vectors, n_rows):
    tab = jnp.stack([v.astype(F32) for v in vectors])
    return jnp.pad(tab, ((0, n_rows - tab.shape[0]), (0, 0)))


def _block_diag(w):
    out = jnp.zeros((GW, GW), w.dtype)
    for n in range(w.shape[0]):
        out = out.at[n * HEAD_DIM:(n + 1) * HEAD_DIM, n * HEAD_DIM:(n + 1) * HEAD_DIM].set(w[n])
    return out


def _pick_block(n, want):
    while n % want:
        want //= 2
    return want


def kernel(x, norm_mix_pre, norm_mix_post, norm_mlp_pre, norm_mlp_post, w_in, w_out, attn_rel_bias, hgrn_lb_logits, hgrn_norm, rwkv_mu, rwkv_w0, rwkv_w2, rwkv_a0, rwkv_a2, rwkv_g2, rwkv_k_k, rwkv_k_a, rwkv_r_k, rwkv_ln_w, rwkv_ln_b, lru_conv_w, lru_conv_b, lru_wa, lru_ba, lru_wx, lru_bx, lru_lambda, mlp_w1, mlp_w2):
    b, s, d = x.shape
    depth = w_in.shape[0]
    t = b * s
    tm = _pick_block(t, 512)
    tm_mlp = _pick_block(t, 1024)
    rb = _pick_block(s, 512)
    rb_big = _pick_block(s, 1024)
    tk = _pick_block(mlp_w1.shape[-1], 1024)

    lb_sm = jax.nn.softmax(hgrn_lb_logits.astype(F32), axis=0)
    lb_all = jnp.maximum(jnp.cumsum(lb_sm, axis=0) - lb_sm[0:1], 0.0)

    w_in_b, w_out_b = w_in.astype(BF16), w_out.astype(BF16)
    w1_b, w2_b = mlp_w1.astype(BF16), mlp_w2.astype(BF16)

    x2 = x.reshape(t, d)
    for l in range(depth):
        pa, pb, pc, pd = _inproj(x2, norm_mix_pre[l].reshape(1, d), w_in_b, l, tm)

        ya = _attention(pa.reshape(b, s, ATTN_COLS), _attn_bias_table(attn_rel_bias[l]), rb_big)

        lb = lb_all[l]
        hgrn_par = _rows([lb, jnp.log1p(-lb), 1.0 - lb, hgrn_norm[l]], 8)
        yb = _hgrn(pb.reshape(b, s, HGRN_COLS), hgrn_par, rb_big)

        mu = rwkv_mu[l].astype(F32)
        mu_lo = jnp.pad(mu[3 * GW:], (0, GW - RWKV_LORA))
        rwkv_par = _rows([mu[0:GW], mu[GW:2 * GW], mu[2 * GW:3 * GW], mu_lo, rwkv_w0[l], rwkv_a0[l],
                          rwkv_k_k[l], rwkv_k_a[l], rwkv_r_k[l].reshape(GW), rwkv_ln_w[l], rwkv_ln_b[l]], 16)
        zeros = lambda n: jnp.zeros((n, GW), F32)
        w2p = jnp.concatenate([rwkv_w2[l].astype(F32), zeros(96)], axis=0).astype(BF16)
        a2p = jnp.concatenate([zeros(32), rwkv_a2[l].astype(F32), zeros(64)], axis=0).astype(BF16)
        g2p = jnp.concatenate([zeros(64), rwkv_g2[l].astype(F32)], axis=0).astype(BF16)
        yc = _rwkv(pc.reshape(b, s, RWKV_COLS), rwkv_par, w2p, a2p, g2p, rb)

        cw = lru_conv_w[l].astype(F32)
        lru_par = _rows([cw[0], cw[1], cw[2], cw[3], lru_conv_b[l], lru_ba[l], lru_bx[l], lru_lambda[l]], 8)
        yd = _lru(pd.reshape(b, s, LRU_COLS), lru_par, _block_diag(lru_wa[l]).astype(BF16),
                  _block_diag(lru_wx[l]).astype(BF16), rb)

        gains = jnp.pad(jnp.stack([norm_mix_post[l], norm_mlp_pre[l], norm_mlp_post[l]]).astype(F32),
                        ((0, 5), (0, 0)))
        x2 = _outmlp(ya.reshape(t, GW), yb.reshape(t, GW), yc.reshape(t, GW), yd.reshape(t, GW),
                     x2, w_out_b, gains, w1_b, w2_b, l, tm_mlp, tk)
    return x2.reshape(b, s, d)
```

```python
import functools
import math

import numpy as np

import jax
import jax.numpy as jnp
from jax import lax
from jax.experimental import pallas as pl
from jax.experimental.pallas import tpu as pltpu

F32 = jnp.float32
BF16 = jnp.bfloat16

GW = 256
HEAD_DIM = 64
N_HEADS = 4
CHUNK = 64
ATTN_LEFT_CHUNKS = 8
BAND = (ATTN_LEFT_CHUNKS + 1) * CHUNK
ATTN_PAD = ATTN_LEFT_CHUNKS * CHUNK
REL_CLIP = 256
ATTN_SCALE = HEAD_DIM ** -0.5
NEG_INF = -1e30
RMS_EPS = 1e-6
RWKV_GN_EPS = HEAD_DIM * 1e-5
RWKV_LORA = 128
LRU_C = 8.0
LRU_CONV = 4
HGRN_SUB = 16

ATTN_COLS = 3 * GW
HGRN_COLS = 4 * GW
RWKV_COLS = 3 * GW + RWKV_LORA
LRU_COLS = 2 * GW

VMEM_LIMIT = 56 * 1024 * 1024
RWKV_PASSES = (1, (3, 3, 3, 1, 1), 1)
RWKV_MAP_BATCH = 8
ATTN_BATCH = 8
CHUNK_UNROLL = 8


def _cparams(*sem):
    return pltpu.CompilerParams(dimension_semantics=sem, vmem_limit_bytes=VMEM_LIMIT)


def _dot(a, b):
    return jnp.dot(a.astype(BF16), b.astype(BF16), preferred_element_type=F32)


def _dot_nt(a, b):
    return lax.dot_general(a.astype(BF16), b.astype(BF16), (((1,), (1,)), ((), ())),
                           preferred_element_type=F32)


def _dot_tn(a, b):
    return lax.dot_general(a.astype(BF16), b.astype(BF16), (((0,), (0,)), ((), ())),
                           preferred_element_type=F32)


def _split(a):
    hi = a.astype(BF16)
    lo = (a - hi.astype(F32)).astype(BF16)
    return hi, lo


def _mm(a, b, kind="nn", passes=1):
    f = {"nn": _dot, "nt": _dot_nt, "tn": _dot_tn}[kind]
    if passes == 1:
        return f(a, b)
    ah, al = _split(a)
    bh, bl = _split(b)
    if kind == "tn":
        return f(ah, bh) + (f(ah, bl) + f(al, bh))
    m = a.shape[0]
    top = f(jnp.concatenate([ah, al], axis=0), bh)
    return top[0:m] + (f(ah, bl) + top[m:])


def _rms(x, gain):
    return x * lax.rsqrt(jnp.mean(x * x, axis=-1, keepdims=True) + RMS_EPS) * gain


def _sigmoid(x):
    return 1.0 / (1.0 + jnp.exp(-x))


def _softplus(x):
    return jnp.maximum(x, 0.0) + jnp.log1p(jnp.exp(-jnp.abs(x)))


def _log1p_exp_neg_abs(x):
    return jnp.log(1.0 + jnp.exp(-jnp.abs(x)))


def _head_block_mask(n):
    r = lax.broadcasted_iota(jnp.int32, (n, n), 0) // HEAD_DIM
    c = lax.broadcasted_iota(jnp.int32, (n, n), 1) // HEAD_DIM
    return r == c


def _cumsum_rows(x):
    n = x.shape[0]
    row = lax.broadcasted_iota(jnp.int32, (n, 1), 0)
    d = 1
    while d < n:
        x = x + jnp.where(row >= d, pltpu.roll(x, d, 0), 0.0)
        d *= 2
    return x


def _stack_heads(x):
    lane_head = lax.broadcasted_iota(jnp.int32, (1, GW), 1) // HEAD_DIM
    return jnp.concatenate([jnp.where(lane_head == h, x, 0.0) for h in range(N_HEADS)], axis=0)


def _inproj_body(x_ref, g_ref, w_ref, oa_ref, ob_ref, oc_ref, od_ref):
    h = _rms(x_ref[...], g_ref[...]).astype(BF16)
    o0, o1, o2 = ATTN_COLS, ATTN_COLS + HGRN_COLS, ATTN_COLS + HGRN_COLS + RWKV_COLS
    oa_ref[...] = jnp.dot(h, w_ref[:, 0:o0], preferred_element_type=F32).astype(BF16)
    ob_ref[...] = jnp.dot(h, w_ref[:, o0:o1], preferred_element_type=F32)
    oc_ref[...] = jnp.dot(h, w_ref[:, o1:o2], preferred_element_type=F32)
    od_ref[...] = jnp.dot(h, w_ref[:, o2:], preferred_element_type=F32)


def _inproj(x2, gain, w_all, layer, tm):
    t, d = x2.shape
    d_in = w_all.shape[2]
    return pl.pallas_call(
        _inproj_body,
        grid=(t // tm,),
        in_specs=[pl.BlockSpec((tm, d), lambda i: (i, 0)),
                  pl.BlockSpec((1, d), lambda i: (0, 0)),
                  pl.BlockSpec((None, d, d_in), lambda i: (layer, 0, 0), pipeline_mode=pl.Buffered(1))],
        out_specs=[pl.BlockSpec((tm, ATTN_COLS), lambda i: (i, 0)),
                   pl.BlockSpec((tm, HGRN_COLS), lambda i: (i, 0)),
                   pl.BlockSpec((tm, RWKV_COLS), lambda i: (i, 0)),
                   pl.BlockSpec((tm, LRU_COLS), lambda i: (i, 0))],
        out_shape=[jax.ShapeDtypeStruct((t, ATTN_COLS), BF16),
                   jax.ShapeDtypeStruct((t, HGRN_COLS), F32),
                   jax.ShapeDtypeStruct((t, RWKV_COLS), F32),
                   jax.ShapeDtypeStruct((t, LRU_COLS), F32)],
        compiler_params=_cparams("parallel"),
        name="inproj",
    )(x2, gain, w_all)


def _attn_body(q_ref, k_ref, v_ref, bias_ref, o_ref, kp_ref, vp_ref, *, chunks):
    i = pl.program_id(1)

    @pl.when(i == 0)
    def _():
        for src, dst in ((k_ref, kp_ref), (v_ref, vp_ref)):
            dst[0:ATTN_PAD, :] = jnp.zeros((ATTN_PAD, GW), BF16)
            dst[ATTN_PAD:, :] = src[0]

    lane_head = lax.broadcasted_iota(jnp.int32, (1, GW), 1) // HEAD_DIM
    kpos = lax.broadcasted_iota(jnp.int32, (1, BAND), 1)

    nb = ATTN_BATCH

    def chunk_group(jj, carry):
        cs = [i * chunks + jj * nb + n for n in range(nb)]
        rows = [pl.ds(pl.multiple_of((jj * nb + n) * CHUNK, CHUNK), CHUNK) for n in range(nb)]
        wins = [pl.ds(pl.multiple_of(c * CHUNK, CHUNK), BAND) for c in cs]
        ss = []
        for c, rw, win in zip(cs, rows, wins):
            q = q_ref[0, rw, :] * ATTN_SCALE
            qbd = jnp.concatenate([jnp.where(lane_head == h, q, jnp.zeros_like(q)) for h in range(N_HEADS)],
                                  axis=0)
            s = _dot_nt(qbd, kp_ref[win, :]) + bias_ref[...]
            ss.append(jnp.where(kpos >= (ATTN_LEFT_CHUNKS - c) * CHUNK, s, NEG_INF))
        ps = [jnp.exp(s - jnp.max(s, axis=-1, keepdims=True)) for s in ss]
        for rw, win, p in zip(rows, wins, ps):
            inv_l = 1.0 / jnp.sum(p, axis=-1, keepdims=True)
            obd = _dot(p, vp_ref[win, :]) * inv_l
            o = jnp.zeros((CHUNK, GW), F32)
            for h in range(N_HEADS):
                o = o + jnp.where(lane_head == h, obd[h * CHUNK:(h + 1) * CHUNK, :], 0.0)
            o_ref[0, rw, :] = o.astype(BF16)
        return carry

    lax.fori_loop(0, chunks // nb, chunk_group, 0, unroll=2)


def _attention(pa3, bias, rb):
    b, s, _ = pa3.shape
    return pl.pallas_call(
        functools.partial(_attn_body, chunks=rb // CHUNK),
        grid=(b, s // rb),
        in_specs=[pl.BlockSpec((1, rb, GW), lambda bi, i: (bi, i, 0)),
                  pl.BlockSpec((1, s, GW), lambda bi, i: (bi, 0, 1)),
                  pl.BlockSpec((1, s, GW), lambda bi, i: (bi, 0, 2)),
                  pl.BlockSpec((N_HEADS * CHUNK, BAND), lambda bi, i: (0, 0))],
        out_specs=pl.BlockSpec((1, rb, GW), lambda bi, i: (bi, i, 0)),
        out_shape=jax.ShapeDtypeStruct((b, s, GW), BF16),
        scratch_shapes=[pltpu.VMEM((s + ATTN_PAD, GW), BF16)] * 2,
        compiler_params=_cparams("parallel", "arbitrary"),
        name="attn",
    )(pa3, pa3, pa3, bias)


def _attn_bias_table(rel_bias):
    rel = np.arange(BAND + CHUNK - 1) - (CHUNK - 1) - ATTN_LEFT_CHUNKS * CHUNK
    ext = rel_bias.astype(F32)[:, np.clip(rel, -REL_CLIP, REL_CLIP) + REL_CLIP]
    tab = jnp.stack([ext[:, CHUNK - 1 - q:CHUNK - 1 - q + BAND] for q in range(CHUNK)], axis=1)
    return tab.reshape(N_HEADS * CHUNK, BAND)


def _hgrn_body(q_ref, f_ref, i_ref, g_ref, par_ref, o_ref, st_ref, oacc_ref, q_s, key_s, cum16_s, cum64_s,
               ckey_s, *, rb):
    @pl.when(pl.program_id(1) == 0)
    def _():
        st_ref[...] = jnp.zeros_like(st_ref)

    m = HGRN_SUB
    lb = par_ref[0:1, :]
    log_1m_lb = par_ref[1:2, :]
    one_m_lb = par_ref[2:3, :]
    gain = par_ref[3:4, :]
    bd = _head_block_mask(GW)
    ones_bd = jnp.where(bd, 1.0, 0.0).astype(BF16)
    trow = lax.broadcasted_iota(jnp.int32, (m, 1), 0)

    fr = f_ref[0]
    e = jnp.exp(-jnp.abs(fr))
    inv_1pe = 1.0 / (1.0 + e)
    sig = jnp.where(fr >= 0.0, inv_1pe, e * inv_1pe)
    log_1pe = jnp.log(1.0 + e)
    log_sig = jnp.minimum(fr, 0.0) - log_1pe
    log_key = log_1m_lb - jnp.maximum(fr, 0.0) - log_1pe
    log_f = jnp.maximum(jnp.log(lb + one_m_lb * sig), log_1m_lb + log_sig)
    row = lax.broadcasted_iota(jnp.int32, (rb, 1), 0) % CHUNK
    cum, d = log_f, 1
    while d < CHUNK:
        cum = cum + jnp.where(row >= d, pltpu.roll(cum, d, 0), 0.0)
        d *= 2
    cum64_s[...] = cum
    before = jnp.where(row == 0, 0.0, pltpu.roll(cum, 1, 0)).reshape(rb // m, m, GW)[:, 0:1, :]
    cum16 = cum - jnp.broadcast_to(before, (rb // m, m, GW)).reshape(rb, GW)
    cum16_s[...] = cum16
    ckey_s[...] = cum16 - log_key
    key_s[...] = one_m_lb * jnp.where(fr >= 0.0, e * inv_1pe, inv_1pe)
    qr = q_ref[0]
    q_s[...] = qr * _sigmoid(qr)

    nsub = CHUNK // m

    def chunk(j, carry):
        rows = pl.ds(pl.multiple_of(j * CHUNK, CHUNK), CHUNK)
        q, key, v = q_s[rows, :], key_s[rows, :], i_ref[0, rows, :]
        c16, c64, ck = cum16_s[rows, :], cum64_s[rows, :], ckey_s[rows, :]
        sub = lambda x, i: x[i * m:(i + 1) * m, :]
        st = st_ref[...]
        o = _dot_nt(q * jnp.exp(c64), st)
        a_exp = []
        for i in range(nsub):
            qi, ci, cki = sub(q, i), sub(c16, i), sub(ck, i)
            ps = []
            for s in range(m):
                ps.append(qi * jnp.exp(jnp.where(trow >= s, ci - cki[s:s + 1, :], NEG_INF)))
            a_exp.append(_dot(jnp.concatenate(ps, axis=0), ones_bd))
        o_sub = []
        for i in range(nsub):
            vi, oi = sub(v, i), sub(o, i)
            for s in range(m):
                oi = oi + a_exp[i][s * m:(s + 1) * m, :] * vi[s:s + 1, :]
            o_sub.append(oi)
        q_hat = q * jnp.exp(c16)
        k_hat = [sub(key, i) * jnp.exp(sub(c16, i)[m - 1:m, :] - sub(c16, i)) for i in range(nsub)]
        b_start = [sub(c64, i)[0:1, :] - sub(c16, i)[0:1, :] for i in range(nsub)]
        b_end = [sub(c64, i)[m - 1:m, :] for i in range(nsub)]
        v_bd = [_stack_heads(sub(v, i)) for i in range(nsub)]
        for i in range(1, nsub):
            k_bd = jnp.concatenate([_stack_heads(k_hat[jj] * jnp.exp(b_start[i] - b_end[jj])) for jj in range(i)],
                                   axis=0)
            a_cat = _dot_nt(sub(q_hat, i), k_bd)
            o_sub[i] = o_sub[i] + _dot(a_cat, jnp.concatenate(v_bd[0:i], axis=0))
        oacc_ref[rows, :] = jnp.concatenate(o_sub, axis=0)
        last = c64[CHUNK - 1:CHUNK, :]
        st_ref[...] = st * jnp.exp(last) + jnp.where(bd, _dot_tn(v, key * jnp.exp(last - c64)), 0.0)
        return carry

    lax.fori_loop(0, rb // CHUNK, chunk, 0, unroll=CHUNK_UNROLL)
    o = oacc_ref[...]
    ms = _dot(o * o, ones_bd) * (1.0 / HEAD_DIM)
    g = g_ref[0]
    o_ref[0] = (o * lax.rsqrt(ms + RMS_EPS) * gain * (g * _sigmoid(g))).astype(BF16)


def _hgrn(pb3, par, rb):
    b, s, _ = pb3.shape
    col = lambda n: pl.BlockSpec((1, rb, GW), lambda bi, i, n=n: (bi, i, n))
    return pl.pallas_call(
        functools.partial(_hgrn_body, rb=rb),
        grid=(b, s // rb),
        in_specs=[col(0), col(1), col(2), col(3),
                  pl.BlockSpec((8, GW), lambda bi, i: (0, 0))],
        out_specs=pl.BlockSpec((1, rb, GW), lambda bi, i: (bi, i, 0)),
        out_shape=jax.ShapeDtypeStruct((b, s, GW), BF16),
        scratch_shapes=[pltpu.VMEM((GW, GW), F32)] + [pltpu.VMEM((rb, GW), F32)] * 6,
        compiler_params=_cparams("parallel", "arbitrary"),
        name="hgrn",
    )(pb3, pb3, pb3, pb3, par)


def _expand_heads(x_cat, bd):
    return jnp.where(bd, jnp.concatenate([x_cat] * N_HEADS, axis=0), 0.0)


def _mm_cat(a, x_cat, bd, passes):
    if passes == 1:
        return _dot(a, _expand_heads(x_cat.astype(BF16), bd))
    ah, al = _split(a)
    xh, xl = _split(x_cat)
    m = a.shape[0]
    top = _dot(jnp.concatenate([ah, al], axis=0), _expand_heads(xh, bd))
    return top[0:m] + (_dot(ah, _expand_heads(xl, bd)) + top[m:])


def _rwkv_chunk_maps(chunks, bd, strict_cat, incl_cat, eye_cat):
    c = CHUNK
    p_pair, p_inv, p_app = RWKV_PASSES
    pre = []
    for r, logw, k, v, alpha, beta in chunks:
        cum = _cumsum_rows(logw)
        e_last = jnp.exp(cum[c - 1:c, :])
        a_t = alpha * jnp.exp(cum - logw)
        r_t = r * jnp.exp(cum)
        e_inv = jnp.exp(-cum)
        b_t = beta * e_inv
        k_t = k * e_inv
        ar = jnp.concatenate([a_t, r_t], axis=0)
        bbd, kbd, vbd = _stack_heads(b_t), _stack_heads(k_t), _stack_heads(v)
        pair = _mm(ar, jnp.concatenate([bbd, kbd], axis=0), "nt", p_pair)
        l_ab = jnp.where(strict_cat, pair[0:c, 0:GW], 0.0)
        lm_k = jnp.where(jnp.concatenate([strict_cat, incl_cat], axis=0), pair[:, GW:], 0.0)
        m_rb = jnp.where(incl_cat, pair[c:, 0:GW], 0.0)
        wy0 = _mm(lm_k, vbd, "nn", p_app)
        pre.append((e_last, a_t, r_t, b_t, k_t, v, l_ab, m_rb, wy0))
    xs = [p[6] for p in pre]
    invs = [eye_cat + x for x in xs]
    xs = [_mm_cat(x, x, bd, p_inv[0]) for x in xs]
    for n in range(len(p_inv)):
        last = n == len(p_inv) - 1
        p_step = p_inv[n] if last else max(p_inv[n], p_inv[n + 1])
        if last:
            invs = [inv + _mm_cat(inv, x, bd, p_step) for inv, x in zip(invs, xs)]
        else:
            both = [_mm_cat(jnp.concatenate([inv, x], axis=0), x, bd, p_step) for inv, x in zip(invs, xs)]
            invs = [inv + bo[0:c] for inv, bo in zip(invs, both)]
            xs = [bo[c:] for bo in both]
    outs = []
    for (e_last, a_t, r_t, b_t, k_t, v, l_ab, m_rb, wy0), inv in zip(pre, invs):
        sol = _mm(inv, jnp.concatenate([_stack_heads(a_t), _stack_heads(wy0[0:c])], axis=1), "nn", p_app)
        a_s, u0 = sol[:, 0:GW], sol[:, GW:]
        out = _mm(m_rb, jnp.concatenate([_stack_heads(a_s), _stack_heads(u0)], axis=1), "nn", p_app)
        ry = r_t + out[:, 0:GW]
        y0 = wy0[c:] + out[:, GW:]
        bh, kh = b_t * e_last, k_t * e_last
        g = jnp.where(bd, _mm(a_s, bh, "tn", p_app), 0.0)
        c0 = jnp.where(bd, _mm(jnp.concatenate([u0, v], axis=0), jnp.concatenate([bh, kh], axis=0), "tn", p_app),
                       0.0)
        outs.append((ry, y0, g, c0, e_last))
    return outs


def _rwkv_body(pc_ref, par_ref, w2_ref, a2_ref, g2_ref, o_ref, ht_ref, prev_ref,
               r_s, w_s, k_s, v_s, al_s, be_s, ry_s, y_s, bon_s, gate_s, g_s, c_s, el_s, *, rb, nseq):
    @pl.when(pl.program_id(1) == 0)
    def _():
        ht_ref[...] = jnp.zeros_like(ht_ref)
        prev_ref[...] = jnp.zeros_like(prev_ref)

    c = CHUNK
    mu_main = [par_ref[n:n + 1, :] for n in range(3)]
    mu_lo = par_ref[3:4, 0:RWKV_LORA]
    w0, a0 = par_ref[4:5, :], par_ref[5:6, :]
    k_k, k_a, r_k = par_ref[6:7, :], par_ref[7:8, :], par_ref[8:9, :]
    ln_w, ln_b = par_ref[9:10, :], par_ref[10:11, :]
    bd = _head_block_mask(GW)
    ones_bd = jnp.where(bd, 1.0, 0.0).astype(BF16)

    row0 = lax.broadcasted_iota(jnp.int32, (rb, 1), 0) == 0
    for q in range(nseq):
        blk = pl.ds(q * rb, rb)
        pc = pc_ref[q]
        prev = jnp.where(row0, prev_ref[q:q + 1, :], pltpu.roll(pc, 1, 0))
        prev_ref[q:q + 1, :] = pc[rb - 1:rb, :]
        xs = [pc[:, n * GW:(n + 1) * GW] for n in range(3)]
        ps = [prev[:, n * GW:(n + 1) * GW] for n in range(3)]
        r, k, v = [x + mu * (p - x) for x, p, mu in zip(xs, ps, mu_main)]
        lo, plo = pc[:, 3 * GW:], prev[:, 3 * GW:]
        lo = lo + mu_lo * (plo - lo)
        w_in = -(w0 + _dot(jnp.tanh(lo), w2_ref[...]))
        w_pre = -(jnp.maximum(w_in, 0.0) + _log1p_exp_neg_abs(w_in)) - 0.5
        a = _sigmoid(a0 + _dot(lo, a2_ref[...]))
        g = _dot(_sigmoid(lo), g2_ref[...])
        kk = k * k_k
        kk = kk / jnp.maximum(jnp.sqrt(_dot(kk * kk, ones_bd)), 1e-12)
        k = k * (1.0 + (a - 1.0) * k_a)
        r_s[blk, :] = r
        w_s[blk, :] = -jnp.exp(w_pre)
        k_s[blk, :] = k
        v_s[blk, :] = v
        al_s[blk, :] = -kk
        be_s[blk, :] = kk * a
        bon_s[blk, :] = _dot(r * k * r_k, ones_bd) * v
        gate_s[blk, :] = g

    tt = lax.broadcasted_iota(jnp.int32, (c, GW), 0)
    ss = lax.broadcasted_iota(jnp.int32, (c, GW), 1) % c
    strict_cat, incl_cat = tt > ss, tt >= ss
    eye_cat = jnp.where(tt == ss, 1.0, 0.0)

    nb = RWKV_MAP_BATCH

    def chunk_maps(jj, carry):
        js = [jj * nb + n for n in range(nb)]
        rows = [pl.ds(pl.multiple_of(j * c, c), c) for j in js]
        maps = _rwkv_chunk_maps([(r_s[rw, :], w_s[rw, :], k_s[rw, :], v_s[rw, :], al_s[rw, :], be_s[rw, :])
                                 for rw in rows], bd, strict_cat, incl_cat, eye_cat)
        for j, rw, (ry, y0, g, c0, e_last) in zip(js, rows, maps):
            ry_s[rw, :] = ry
            y_s[rw, :] = y0
            g_s[j] = g.astype(BF16)
            c_s[j] = c0
            el_s[j] = jnp.broadcast_to(e_last, (8, GW))
        return carry

    lax.fori_loop(0, nseq * rb // (c * nb), chunk_maps, 0)

    nchunk = rb // c

    def chunk_apply(j, carry):
        for q in range(nseq):
            jq = q * nchunk + j
            rows = pl.ds(pl.multiple_of(jq * c, c), c)
            ht = ht_ref[q]
            htb = ht.astype(BF16)
            y_s[rows, :] = y_s[rows, :] + _dot_nt(ry_s[rows, :], htb)
            ht_ref[q] = ht * el_s[jq][0:1, :] + jnp.dot(htb, g_s[jq], preferred_element_type=F32) + c_s[jq]
        return carry

    lax.fori_loop(0, nchunk, chunk_apply, 0, unroll=CHUNK_UNROLL)

    y = y_s[...]
    mean = _dot(y, ones_bd) * (1.0 / HEAD_DIM)
    yc = y - mean
    var = _dot(yc * yc, ones_bd) * (1.0 / HEAD_DIM)
    yn = yc * lax.rsqrt(var + RWKV_GN_EPS) * ln_w + ln_b
    o_ref[...] = ((yn + bon_s[...]) * gate_s[...]).astype(BF16).reshape(nseq, rb, GW)


def _rwkv(pc3, par, w2p, a2p, g2p, rb):
    b, s, _ = pc3.shape
    nseq = 2 if b % 2 == 0 else 1
    nchunk = nseq * rb // CHUNK
    full = lambda shp: pl.BlockSpec(shp, lambda bi, i: (0,) * len(shp))
    return pl.pallas_call(
        functools.partial(_rwkv_body, rb=rb, nseq=nseq),
        grid=(b // nseq, s // rb),
        in_specs=[pl.BlockSpec((nseq, rb, RWKV_COLS), lambda bi, i: (bi, i, 0)),
                  full((16, GW)), full((RWKV_LORA, GW)), full((RWKV_LORA, GW)), full((RWKV_LORA, GW))],
        out_specs=pl.BlockSpec((nseq, rb, GW), lambda bi, i: (bi, i, 0)),
        out_shape=jax.ShapeDtypeStruct((b, s, GW), BF16),
        scratch_shapes=[pltpu.VMEM((nseq, GW, GW), F32), pltpu.VMEM((nseq, RWKV_COLS), F32)]
        + [pltpu.VMEM((nseq * rb, GW), F32)] * 10
        + [pltpu.VMEM((nchunk, GW, GW), BF16), pltpu.VMEM((nchunk, GW, GW), F32),
           pltpu.VMEM((nchunk, 8, GW), F32)],
        compiler_params=_cparams("parallel", "arbitrary"),
        name="rwkv",
    )(pc3, par, w2p, a2p, g2p)


def _lru_body(x_ref, g_ref, par_ref, wa_ref, wx_ref, o_ref, ext_ref, h_ref, *, rb):
    @pl.when(pl.program_id(1) == 0)
    def _():
        ext_ref[0:8, :] = jnp.zeros((8, GW), F32)
        h_ref[...] = jnp.zeros_like(h_ref)

    conv_b, ba, bx, lam = par_ref[4:5, :], par_ref[5:6, :], par_ref[6:7, :], par_ref[7:8, :]
    x = x_ref[0]
    ext_ref[8:8 + rb, :] = x
    conv = conv_b + par_ref[0:1, :] * ext_ref[pl.ds(8 - 3, rb), :]
    for j in range(1, LRU_CONV):
        conv = conv + par_ref[j:j + 1, :] * ext_ref[pl.ds(8 - 3 + j, rb), :]
    ext_ref[0:8, :] = x[rb - 8:rb, :]
    gate_r = _sigmoid(_dot(conv, wa_ref[...]) + ba)
    gate_i = _sigmoid(_dot(conv, wx_ref[...]) + bx)
    log_a = -LRU_C * gate_r * _softplus(-lam)
    a = jnp.exp(log_a)
    inp = jnp.sqrt((1.0 - a) * (1.0 + a)) * (gate_i * conv)
    row = lax.broadcasted_iota(jnp.int32, (rb, 1), 0) % CHUNK
    d = 1
    while d < CHUNK:
        m = row >= d
        inp = jnp.where(m, a * pltpu.roll(inp, d, 0) + inp, inp)
        a = jnp.where(m, a * pltpu.roll(a, d, 0), a)
        d *= 2
    h_prev, hs = h_ref[...], []
    for n in range(rb // CHUNK):
        seg = slice(n * CHUNK, (n + 1) * CHUNK)
        hs.append(inp[seg] + a[seg] * h_prev)
        h_prev = hs[-1][CHUNK - 1:CHUNK, :]
    h = jnp.concatenate(hs, axis=0)
    h_ref[...] = h_prev
    g = g_ref[0]
    gelu = 0.5 * g * (1.0 + jnp.tanh(math.sqrt(2.0 / math.pi) * (g + 0.044715 * (g * g * g))))
    o_ref[0] = (h * gelu).astype(BF16)


def _lru(pd3, par, wa_bd, wx_bd, rb):
    b, s, _ = pd3.shape
    full = lambda shp: pl.BlockSpec(shp, lambda bi, i: (0,) * len(shp))
    return pl.pallas_call(
        functools.partial(_lru_body, rb=rb),
        grid=(b, s // rb),
        in_specs=[pl.BlockSpec((1, rb, GW), lambda bi, i: (bi, i, 0)),
                  pl.BlockSpec((1, rb, GW), lambda bi, i: (bi, i, 1)),
                  full((8, GW)), full((GW, GW)), full((GW, GW))],
        out_specs=pl.BlockSpec((1, rb, GW), lambda bi, i: (bi, i, 0)),
        out_shape=jax.ShapeDtypeStruct((b, s, GW), BF16),
        scratch_shapes=[pltpu.VMEM((rb + 8, GW), F32), pltpu.VMEM((1, GW), F32)],
        compiler_params=_cparams("parallel", "arbitrary"),
        name="lru",
    )(pd3, pd3, par, wa_bd, wx_bd)


def _outmlp_body(ya_ref, yb_ref, yc_ref, yd_ref, x_ref, wo_ref, gains_ref, w1_ref, w2_ref, o_ref,
                 xm_ref, h_ref, acc_ref):
    kk = pl.program_id(1)

    @pl.when(kk == 0)
    def _():
        mix = jnp.concatenate([ya_ref[...], yb_ref[...], yc_ref[...], yd_ref[...]], axis=1)
        xm = x_ref[...] + _rms(jnp.dot(mix, wo_ref[...], preferred_element_type=F32), gains_ref[0:1, :])
        xm_ref[...] = xm
        h_ref[...] = _rms(xm, gains_ref[1:2, :]).astype(BF16)
        acc_ref[...] = jnp.zeros_like(acc_ref)

    a = jnp.dot(h_ref[...], w1_ref[...], preferred_element_type=F32)
    a = jnp.square(jnp.maximum(a, 0.0)).astype(BF16)
    acc_ref[...] += jnp.dot(a, w2_ref[...], preferred_element_type=F32)

    @pl.when(kk == pl.num_programs(1) - 1)
    def _():
        o_ref[...] = xm_ref[...] + _rms(acc_ref[...], gains_ref[2:3, :])


def _outmlp(ya, yb, yc, yd, x2, wo_all, gains, w1_all, w2_all, layer, tm, tk):
    t, d = x2.shape
    hid = w1_all.shape[2]
    mix = pl.BlockSpec((tm, GW), lambda i, k: (i, 0))
    return pl.pallas_call(
        _outmlp_body,
        grid=(t // tm, hid // tk),
        in_specs=[mix, mix, mix, mix,
                  pl.BlockSpec((tm, d), lambda i, k: (i, 0)),
                  pl.BlockSpec((None, 4 * GW, d), lambda i, k: (layer, 0, 0)),
                  pl.BlockSpec((8, d), lambda i, k: (0, 0)),
                  pl.BlockSpec((None, d, tk), lambda i, k: (layer, 0, k)),
                  pl.BlockSpec((None, tk, d), lambda i, k: (layer, k, 0))],
        out_specs=pl.BlockSpec((tm, d), lambda i, k: (i, 0)),
        out_shape=jax.ShapeDtypeStruct((t, d), F32),
        scratch_shapes=[pltpu.VMEM((tm, d), F32), pltpu.VMEM((tm, d), BF16), pltpu.VMEM((tm, d), F32)],
        compiler_params=_cparams("parallel", "arbitrary"),
        name="outmlp",
    )(ya, yb, yc, yd, x2, wo_all, gains, w1_all, w2_all)


def _rows(vectors, n_rows):
    tab = jnp.stack([v.astype(F32) for v in vectors])
    return jnp.pad(tab, ((0, n_rows - tab.shape[0]), (0, 0)))


def _block_diag(w):
    out = jnp.zeros((GW, GW), w.dtype)
    for n in range(w.shape[0]):
        out = out.at[n * HEAD_DIM:(n + 1) * HEAD_DIM, n * HEAD_DIM:(n + 1) * HEAD_DIM].set(w[n])
    return out


def _pick_block(n, want):
    while n % want:
        want //= 2
    return want


def kernel(x, norm_mix_pre, norm_mix_post, norm_mlp_pre, norm_mlp_post, w_in, w_out, attn_rel_bias, hgrn_lb_logits, hgrn_norm, rwkv_mu, rwkv_w0, rwkv_w2, rwkv_a0, rwkv_a2, rwkv_g2, rwkv_k_k, rwkv_k_a, rwkv_r_k, rwkv_ln_w, rwkv_ln_b, lru_conv_w, lru_conv_b, lru_wa, lru_ba, lru_wx, lru_bx, lru_lambda, mlp_w1, mlp_w2):
    b, s, d = x.shape
    depth = w_in.shape[0]
    t = b * s
    tm = _pick_block(t, 512)
    tm_mlp = _pick_block(t, 1024)
    rb = _pick_block(s, 512)
    rb_big = _pick_block(s, 1024)
    tk = _pick_block(mlp_w1.shape[-1], 1024)

    lb_sm = jax.nn.softmax(hgrn_lb_logits.astype(F32), axis=0)
    lb_all = jnp.maximum(jnp.cumsum(lb_sm, axis=0) - lb_sm[0:1], 0.0)

    w_in_b, w_out_b = w_in.astype(BF16), w_out.astype(BF16)
    w1_b, w2_b = mlp_w1.astype(BF16), mlp_w2.astype(BF16)

    x2 = x.reshape(t, d)
    for l in range(depth):
        pa, pb, pc, pd = _inproj(x2, norm_mix_pre[l].reshape(1, d), w_in_b, l, tm_mlp)

        ya = _attention(pa.reshape(b, s, ATTN_COLS), _attn_bias_table(attn_rel_bias[l]), rb_big)

        lb = lb_all[l]
        hgrn_par = _rows([lb, jnp.log1p(-lb), 1.0 - lb, hgrn_norm[l]], 8)
        yb = _hgrn(pb.reshape(b, s, HGRN_COLS), hgrn_par, rb_big)

        mu = rwkv_mu[l].astype(F32)
        mu_lo = jnp.pad(mu[3 * GW:], (0, GW - RWKV_LORA))
        rwkv_par = _rows([mu[0:GW], mu[GW:2 * GW], mu[2 * GW:3 * GW], mu_lo, rwkv_w0[l], rwkv_a0[l],
                          rwkv_k_k[l], rwkv_k_a[l], rwkv_r_k[l].reshape(GW), rwkv_ln_w[l], rwkv_ln_b[l]], 16)
        zeros = lambda n: jnp.zeros((n, GW), F32)
        w2p = jnp.concatenate([rwkv_w2[l].astype(F32), zeros(96)], axis=0).astype(BF16)
        a2p = jnp.concatenate([zeros(32), rwkv_a2[l].astype(F32), zeros(64)], axis=0).astype(BF16)
        g2p = jnp.concatenate([zeros(64), rwkv_g2[l].astype(F32)], axis=0).astype(BF16)
        yc = _rwkv(pc.reshape(b, s, RWKV_COLS), rwkv_par, w2p, a2p, g2p, rb)

        cw = lru_conv_w[l].astype(F32)
        lru_par = _rows([cw[0], cw[1], cw[2], cw[3], lru_conv_b[l], lru_ba[l], lru_bx[l], lru_lambda[l]], 8)
        yd = _lru(pd.reshape(b, s, LRU_COLS), lru_par, _block_diag(lru_wa[l]).astype(BF16),
                  _block_diag(lru_wx[l]).astype(BF16), rb)

        gains = jnp.pad(jnp.stack([norm_mix_post[l], norm_mlp_pre[l], norm_mlp_post[l]]).astype(F32),
                        ((0, 5), (0, 0)))
        x2 = _outmlp(ya.reshape(t, GW), yb.reshape(t, GW), yc.reshape(t, GW), yd.reshape(t, GW),
                     x2, w_out_b, gains, w1_b, w2_b, l, tm_mlp, tk)
    return x2.reshape(b, s, d)
```
